```python
import math
import jax, jax.numpy as jnp
from jax import lax
import numpy as np

D_MODEL = 1024
BATCH = 8
SEQ = 4096
DEPTH = 1

HY_WIDTH = 512
HY_ORDER = 2
SHORT_CONV = 3
FILTER_EMB = 33
FILTER_HIDDEN = 64
DECAY_TARGET = 1e-2
FAST_DECAY = 0.3
SLOW_DECAY = 1.5
MIN_DECAY = math.log(DECAY_TARGET) / SLOW_DECAY
MAX_DECAY = math.log(DECAY_TARGET) / FAST_DECAY

MLA_HEADS = 8
QK_NOPE = 64
QK_ROPE = 32
QK_DIM = QK_NOPE + QK_ROPE
V_DIM = 64
Q_LORA = 384
KV_LORA = 256
ATTN_WIDTH = MLA_HEADS * V_DIM
ROPE_THETA = 10000.0
Q_BLOCK = 128

EPS = 1e-6
IN_COLS = 4 * HY_WIDTH + Q_LORA + KV_LORA + QK_ROPE + ATTN_WIDTH + 2 * D_MODEL

kernel_name = 'hyena_mla_gated_parallel_encoder'


def rms_norm(x, g):
    xf = x.astype(jnp.float32)
    y = xf * lax.rsqrt(jnp.mean(xf * xf, axis=-1, keepdims=True) + EPS)
    return (y * g.astype(jnp.float32)).astype(x.dtype)


def short_conv(u, w, b):
    L = u.shape[1]
    pad = SHORT_CONV // 2
    up = jnp.pad(u, ((0, 0), (pad, pad), (0, 0)))
    return sum(up[:, j:j + L] * w[j] for j in range(SHORT_CONV)) + b


def hyena_filter_spectra(L, w_f1, b_f1, freq_1, w_f2, b_f2, freq_2, w_f3):
    t = jnp.linspace(0.0, 1.0, L, dtype=jnp.float32)[:, None]
    bands = (FILTER_EMB - 1) // 2
    f = jnp.linspace(1e-4, bands - 1, bands, dtype=jnp.float32)
    ang = (2.0 * jnp.pi / L) * jnp.arange(L, dtype=jnp.float32)[:, None] * f[None, :]
    z = jnp.concatenate([t, jnp.cos(ang), -jnp.sin(ang)], axis=-1)
    h = jnp.sin(freq_1 * (z @ w_f1 + b_f1))
    h = jnp.sin(freq_2 * (h @ w_f2 + b_f2))
    k = (h @ w_f3).reshape(L, HY_ORDER, 2, HY_WIDTH)
    deltas = jnp.abs(jnp.linspace(MIN_DECAY, MAX_DECAY, HY_WIDTH, dtype=jnp.float32))
    decay = jnp.exp(-t * deltas)
    k = k * decay[:, None, None, :]
    k_full = jnp.concatenate([k[:, :, 0], k[::-1, :, 1]], axis=0)
    return jnp.fft.rfft(k_full.astype(jnp.float32), axis=0)


def long_conv(u, k_spec, bias):
    L = u.shape[1]
    uf = u.astype(jnp.float32)
    y = jnp.fft.irfft(jnp.fft.rfft(uf, n=2 * L, axis=1) * k_spec[None], n=2 * L, axis=1)[:, :L]
    return (y + uf * bias.astype(jnp.float32)).astype(u.dtype)


def rotary(t, cos, sin):
    t1, t2 = jnp.split(t, 2, axis=-1)
    c = cos[:, None, :].astype(t.dtype)
    s = sin[:, None, :].astype(t.dtype)
    return jnp.concatenate([t1 * c - t2 * s, t1 * s + t2 * c], axis=-1)


def dense_bidirectional_attention(q, k, v):
    B, L, H, Dk = q.shape
    nb = L // Q_BLOCK
    scale = 1.0 / math.sqrt(Dk)
    kt = k.transpose(0, 2, 1, 3)
    vt = v.transpose(0, 2, 1, 3)
    qb = q.reshape(B, nb, Q_BLOCK, H, Dk).transpose(1, 0, 3, 2, 4)

    def one_block(qblk):
        s = jnp.einsum('bhqd,bhkd->bhqk', qblk, kt).astype(jnp.float32) * scale
        p = jax.nn.softmax(s, axis=-1).astype(vt.dtype)
        return jnp.einsum('bhqk,bhkd->bqhd', p, vt)

    o = lax.map(one_block, qb)
    return o.transpose(1, 0, 2, 3, 4).reshape(B, L, H * v.shape[-1])


def setup_inputs(seed: int = 0) -> dict:
    key = jax.random.key(seed)
    ks = jax.random.split(key, 24)
    f32 = jnp.float32

    def nrm(k, shape, scale):
        return jax.random.normal(k, shape, f32) * scale

    n = DEPTH
    return {
        'x': nrm(ks[0], (BATCH, SEQ, D_MODEL), 1.0),
        'g_norm': 1.0 + nrm(ks[1], (n, D_MODEL), 0.02),
        'w_in': nrm(ks[2], (n, D_MODEL, IN_COLS), D_MODEL ** -0.5),
        'b_gate': nrm(ks[3], (n, 2 * D_MODEL), 0.02),
        'w_short': nrm(ks[4], (n, SHORT_CONV, 3 * HY_WIDTH), SHORT_CONV ** -0.5),
        'b_short': nrm(ks[5], (n, 3 * HY_WIDTH), 0.02),
        'w_f1': nrm(ks[6], (n, FILTER_EMB, FILTER_HIDDEN), FILTER_EMB ** -0.5),
        'b_f1': nrm(ks[7], (n, FILTER_HIDDEN), 0.02),
        'freq_1': 1.0 + nrm(ks[8], (n, FILTER_HIDDEN), 0.02),
        'w_f2': nrm(ks[9], (n, FILTER_HIDDEN, FILTER_HIDDEN), FILTER_HIDDEN ** -0.5),
        'b_f2': nrm(ks[10], (n, FILTER_HIDDEN), 0.02),
        'freq_2': 1.0 + nrm(ks[11], (n, FILTER_HIDDEN), 0.02),
        'w_f3': nrm(ks[12], (n, FILTER_HIDDEN, HY_ORDER * 2 * HY_WIDTH), 2.0 * (FILTER_HIDDEN * SEQ) ** -0.5),
        'hy_bias': nrm(ks[13], (n, HY_ORDER, HY_WIDTH), 0.1),
        'w_hy_out': nrm(ks[14], (n, HY_WIDTH, D_MODEL), HY_WIDTH ** -0.5),
        'g_cq': 1.0 + nrm(ks[15], (n, Q_LORA), 0.02),
        'w_uq': nrm(ks[16], (n, Q_LORA, MLA_HEADS * QK_DIM), Q_LORA ** -0.5),
        'g_ckv': 1.0 + nrm(ks[17], (n, KV_LORA), 0.02),
        'w_ukv': nrm(ks[18], (n, KV_LORA, MLA_HEADS * (QK_NOPE + V_DIM)), KV_LORA ** -0.5),
        'g_qn': 1.0 + nrm(ks[19], (n, QK_DIM), 0.02),
        'g_kn': 1.0 + nrm(ks[20], (n, QK_DIM), 0.02),
        'w_attn_out': nrm(ks[21], (n, ATTN_WIDTH, D_MODEL), ATTN_WIDTH ** -0.5),
        'w_out': nrm(ks[22], (n, D_MODEL, D_MODEL), D_MODEL ** -0.5),
    }


def reference(x, g_norm, w_in, b_gate, w_short, b_short, w_f1, b_f1, freq_1, w_f2, b_f2, freq_2,
              w_f3, hy_bias, w_hy_out, g_cq, w_uq, g_ckv, w_ukv, g_qn, g_kn, w_attn_out, w_out):
    B, L, _ = x.shape
    pos = jnp.arange(L, dtype=jnp.float32)
    inv_freq = ROPE_THETA ** (-jnp.arange(0, QK_ROPE, 2, dtype=jnp.float32) / QK_ROPE)
    ang = pos[:, None] * inv_freq[None, :]
    cos, sin = jnp.cos(ang), jnp.sin(ang)

    split_at = list(np.cumsum([3 * HY_WIDTH, HY_WIDTH, Q_LORA, KV_LORA, QK_ROPE, ATTN_WIDTH, D_MODEL]))

    for l in range(DEPTH):
        h = rms_norm(x, g_norm[l])
        proj = h @ w_in[l]
        hy_in, z_hy, c_q, c_kv, k_rope, z_attn, gate_hy, gate_attn = jnp.split(proj, split_at, axis=-1)

        u = short_conv(hy_in, w_short[l], b_short[l])
        v_h, x1, x2 = jnp.split(u, 3, axis=-1)
        k_spec = hyena_filter_spectra(L, w_f1[l], b_f1[l], freq_1[l], w_f2[l], b_f2[l], freq_2[l], w_f3[l])
        zc = v_h
        for o, gate in enumerate((x1, x2)):
            zc = gate * long_conv(zc, k_spec[:, o], hy_bias[l, o])
        u_hy = (zc * jax.nn.silu(z_hy)) @ w_hy_out[l]

        q = (rms_norm(c_q, g_cq[l]) @ w_uq[l]).reshape(B, L, MLA_HEADS, QK_DIM)
        kv = (rms_norm(c_kv, g_ckv[l]) @ w_ukv[l]).reshape(B, L, MLA_HEADS, QK_NOPE + V_DIM)
        k_nope, v_a = kv[..., :QK_NOPE], kv[..., QK_NOPE:]
        k_r = jnp.broadcast_to(k_rope[:, :, None, :], (B, L, MLA_HEADS, QK_ROPE))
        k = jnp.concatenate([k_nope, k_r], axis=-1)
        q = rms_norm(q, g_qn[l])
        k = rms_norm(k, g_kn[l])
        q = jnp.concatenate([q[..., :QK_NOPE], rotary(q[..., QK_NOPE:], cos, sin)], axis=-1)
        k = jnp.concatenate([k[..., :QK_NOPE], rotary(k[..., QK_NOPE:], cos, sin)], axis=-1)
        attn = dense_bidirectional_attention(q, k, v_a)
        u_attn = (attn * jax.nn.silu(z_attn)) @ w_attn_out[l]

        gates = jax.nn.sigmoid(jnp.concatenate([gate_hy, gate_attn], axis=-1) + b_gate[l])
        g_hy, g_at = jnp.split(gates, 2, axis=-1)
        merged = g_hy * u_hy + g_at * u_attn
        x = x + merged @ w_out[l]
    return x
```

```python
import functools
import math

import numpy as np
import jax
import jax.numpy as jnp
from jax import lax
from jax.experimental import pallas as pl
from jax.experimental.pallas import tpu as pltpu

D_MODEL = 1024
HY_WIDTH = 512
FILTER_EMB = 33
FILTER_HIDDEN = 64
DECAY_TARGET = 1e-2
FAST_DECAY = 0.3
SLOW_DECAY = 1.5
MIN_DECAY = math.log(DECAY_TARGET) / SLOW_DECAY
MAX_DECAY = math.log(DECAY_TARGET) / FAST_DECAY
MLA_HEADS = 8
HEAD_PAIRS = MLA_HEADS // 2
QK_NOPE = 64
QK_ROPE = 32
QK_DIM = QK_NOPE + QK_ROPE
V_DIM = 64
Q_LORA = 384
KV_LORA = 256
ATTN_WIDTH = MLA_HEADS * V_DIM
ROPE_THETA = 10000.0
EPS = 1e-6

LANES = 128
SUBLANES = 8
DFT_LANE = 256
KR_WIDE = 256
VMEM_LIMIT = 56 * 1024 * 1024

F32 = jnp.float32
BF16 = jnp.bfloat16
_NT = (((1,), (1,)), ((), ()))
_NN = (((1,), (0,)), ((), ()))


def _cparams(sem):
    return pltpu.CompilerParams(dimension_semantics=sem, vmem_limit_bytes=VMEM_LIMIT)


def _resident(shape):
    nd = len(shape)
    return pl.BlockSpec(shape, lambda *_: (0,) * nd, pipeline_mode=pl.Buffered(1))


def _add(a, b):
    if a is None:
        return b
    if b is None:
        return a
    return a + b


def _sub(a, b):
    if b is None:
        return a
    if a is None:
        return -b
    return a - b


def _cadd(x, y):
    return (_add(x[0], y[0]), _add(x[1], y[1]))


def _csub(x, y):
    return (_sub(x[0], y[0]), _sub(x[1], y[1]))


def _scale(a, s):
    if a is None or s == 0.0:
        return None
    if s == 1.0:
        return a
    if s == -1.0:
        return -a
    return a * s


def _cmulc(x, w):
    wr, wi = float(np.real(w)), float(np.imag(w))
    if abs(wr) < 1e-15:
        wr = 0.0
    if abs(wi) < 1e-15:
        wi = 0.0
    xr, xi = x
    re = _sub(_scale(xr, wr), _scale(xi, wi))
    im = _add(_scale(xr, wi), _scale(xi, wr))
    return (re, im)


def _cmul(x, y):
    xr, xi = x
    yr, yi = y
    re = _sub(None if xr is None else xr * yr, None if xi is None else xi * yi)
    im = _add(None if xr is None else xr * yi, None if xi is None else xi * yr)
    return (re, im)


def _fft_dif(xs, sign):
    n = len(xs)
    if n == 1:
        return xs
    half = n // 2
    a = [_cadd(xs[i], xs[i + half]) for i in range(half)]
    b = [_cmulc(_csub(xs[i], xs[i + half]), np.exp(sign * 2j * np.pi * i / n)) for i in range(half)]
    ev = _fft_dif(a, sign)
    od = _fft_dif(b, sign)
    out = [None] * n
    out[0::2] = ev
    out[1::2] = od
    return out


def _zero_like_none(p, ref):
    return jnp.zeros_like(ref) if p is None else p


def _dft_cat():
    n = np.arange(DFT_LANE)
    ang = 2.0 * np.pi * np.outer(n, n) / DFT_LANE
    return np.concatenate([np.cos(ang), -np.sin(ang)], axis=1).astype(np.float32)


def _twiddle(n1):
    n = n1 * DFT_LANE
    ang = 2.0 * np.pi * np.outer(np.arange(n1), np.arange(DFT_LANE)) / n
    tw = np.stack([np.cos(ang), -np.sin(ang)], axis=1)
    return np.ascontiguousarray(np.broadcast_to(tw[:, :, None, :], (n1, 2, SUBLANES, DFT_LANE))).astype(np.float32)


def _filter_features(L):
    t = np.linspace(0.0, 1.0, L)[:, None]
    bands = (FILTER_EMB - 1) // 2
    f = np.linspace(1e-4, bands - 1, bands)
    ang = (2.0 * np.pi / L) * np.arange(L)[:, None] * f[None, :]
    z = np.concatenate([t, np.cos(ang), -np.sin(ang)], axis=-1)
    zp = np.zeros((L, LANES), np.float32)
    zp[:, :FILTER_EMB] = z
    return zp, t[:, 0].astype(np.float32)


def _rope_tables(L):
    pos = np.arange(L, dtype=np.float64)
    inv_freq = ROPE_THETA ** (-np.arange(0, QK_ROPE, 2, dtype=np.float64) / QK_ROPE)
    ang = pos[:, None] * inv_freq[None, :]
    c, s = np.cos(ang), np.sin(ang)
    ctab = np.concatenate([np.ones((L, QK_NOPE)), c, c], axis=1)
    stab = np.concatenate([np.zeros((L, QK_NOPE)), -s, s], axis=1)
    return ctab.astype(np.float32), stab.astype(np.float32)


def _in_proj_kernel(x_ref, gn_ref, wt_hy_ref, w_cq_ref, w_ckv_ref, w_kr_ref, w_rest_ref, gcq_ref, gckv_ref,
                    hyt_ref, cq_ref, ckv_ref, kr_ref, zat_ref, gate_ref):
    x = x_ref[0]
    ms = jnp.mean(x * x, axis=-1, keepdims=True)
    h = (x * lax.rsqrt(ms + EPS) * gn_ref[...]).astype(BF16)
    hyt_ref[0] = lax.dot_general(wt_hy_ref[...], h, _NT, preferred_element_type=F32).astype(BF16)

    def latent(w_ref, g_ref):
        c = jnp.dot(h, w_ref[...], preferred_element_type=F32)
        r = lax.rsqrt(jnp.mean(c * c, axis=-1, keepdims=True) + EPS)
        return (c * r * g_ref[...]).astype(BF16)

    cq_ref[0] = latent(w_cq_ref, gcq_ref)
    ckv_ref[0] = latent(w_ckv_ref, gckv_ref)
    kr_ref[0] = jnp.dot(h, w_kr_ref[...], preferred_element_type=F32)
    rest = jnp.dot(h, w_rest_ref[...], preferred_element_type=F32)
    zat_ref[0] = rest[:, :ATTN_WIDTH].astype(BF16)
    gate_ref[0] = rest[:, ATTN_WIDTH:].astype(BF16)


def _in_proj(x, gn, wt_hy, w_cq, w_ckv, w_kr, w_rest, gcq, gckv, tl):
    B, L, _ = x.shape
    n_rest = w_rest.shape[1]
    return pl.pallas_call(
        _in_proj_kernel,
        grid=(B, L // tl),
        in_specs=[
            pl.BlockSpec((1, tl, D_MODEL), lambda b, i: (b, i, 0)),
            _resident(gn.shape), _resident(wt_hy.shape), _resident(w_cq.shape), _resident(w_ckv.shape),
            _resident(w_kr.shape), _resident(w_rest.shape), _resident(gcq.shape), _resident(gckv.shape),
        ],
        out_specs=[
            pl.BlockSpec((1, 4 * HY_WIDTH, tl), lambda b, i: (b, 0, i)),
            pl.BlockSpec((1, tl, Q_LORA), lambda b, i: (b, i, 0)),
            pl.BlockSpec((1, tl, KV_LORA), lambda b, i: (b, i, 0)),
            pl.BlockSpec((1, tl, KR_WIDE), lambda b, i: (b, i, 0)),
            pl.BlockSpec((1, tl, ATTN_WIDTH), lambda b, i: (b, i, 0)),
            pl.BlockSpec((1, tl, 2 * D_MODEL), lambda b, i: (b, i, 0)),
        ],
        out_shape=[
            jax.ShapeDtypeStruct((B, 4 * HY_WIDTH, L), BF16),
            jax.ShapeDtypeStruct((B, L, Q_LORA), BF16),
            jax.ShapeDtypeStruct((B, L, KV_LORA), BF16),
            jax.ShapeDtypeStruct((B, L, KR_WIDE), F32),
            jax.ShapeDtypeStruct((B, L, ATTN_WIDTH), BF16),
            jax.ShapeDtypeStruct((B, L, n_rest - ATTN_WIDTH), BF16),
        ],
        compiler_params=_cparams(("parallel", "parallel")),
        name="in_proj",
    )(x, gn, wt_hy, w_cq, w_ckv, w_kr, w_rest, gcq, gckv)


def _filter_kernel(z_ref, zr_ref, t_ref, tr_ref, w1_ref, b1_ref, f1_ref, w2_ref, b2_ref, f2_ref,
                   w3f_ref, w3b_ref, dl_ref, out_ref, *, L):
    hp = lax.Precision.HIGHEST

    def mlp(z):
        h = jnp.sin(f1_ref[...] * (jnp.dot(z, w1_ref[...], precision=hp, preferred_element_type=F32) + b1_ref[...]))
        return jnp.sin(f2_ref[...] * (jnp.dot(h, w2_ref[...], precision=hp, preferred_element_type=F32) + b2_ref[...]))

    dl = dl_ref[...]
    kf = lax.dot_general(w3f_ref[...], mlp(z_ref[...]), _NT, precision=hp, preferred_element_type=F32)
    out_ref[:, :L] = kf * jnp.exp(-(dl * t_ref[...]))
    kb = lax.dot_general(w3b_ref[...], mlp(zr_ref[...]), _NT, precision=hp, preferred_element_type=F32)
    out_ref[:, L:] = kb * jnp.exp(-(dl * tr_ref[...]))


def _filter(z, zr, t, tr, w1, b1, f1, w2, b2, f2, w3f, w3b, dl, L, rows):
    n_rows = w3f.shape[0]
    row_spec = lambda shape: pl.BlockSpec(shape, lambda i: (i, 0))
    return pl.pallas_call(
        functools.partial(_filter_kernel, L=L),
        grid=(n_rows // rows,),
        in_specs=[_resident(z.shape), _resident(zr.shape), _resident(t.shape), _resident(tr.shape),
                  _resident(w1.shape), _resident(b1.shape), _resident(f1.shape),
                  _resident(w2.shape), _resident(b2.shape), _resident(f2.shape),
                  row_spec((rows, FILTER_HIDDEN)), row_spec((rows, FILTER_HIDDEN)), row_spec((rows, 1))],
        out_specs=pl.BlockSpec((rows, 2 * L), lambda i: (i, 0)),
        out_shape=jax.ShapeDtypeStruct((n_rows, 2 * L), F32),
        compiler_params=_cparams(("parallel",)),
        name="filt",
    )(z, zr, t, tr, w1, b1, f1, w2, b2, f2, w3f, w3b, dl)


def _mxu_dft(lhs_f32, fcat, groups, rows, inverse, precision=None):
    lhs = lhs_f32.reshape(groups * 2 * rows, DFT_LANE)
    if precision is None:
        lhs = lhs.astype(BF16)
    res = jnp.dot(lhs, fcat, precision=precision, preferred_element_type=F32)
    res = res.reshape(groups, 2, rows, 2 * DFT_LANE)
    a = res[:, 0, :, :DFT_LANE]
    b = res[:, 0, :, DFT_LANE:]
    c = res[:, 1, :, :DFT_LANE]
    d = res[:, 1, :, DFT_LANE:]
    if inverse:
        return a + d, c - b
    return a - d, b + c


def _stage_a_forward(load, g_scr, tw_ref, n1, rows_total):
    def body(r, carry):
        rs = pl.ds(pl.multiple_of(r * SUBLANES, SUBLANES), SUBLANES)
        for half in range(DFT_LANE // LANES):
            ls = slice(half * LANES, (half + 1) * LANES)
            xs = [load(i, rs, ls) for i in range(n1)]
            xs = [(None, None) if x is None else x for x in xs]
            X = _fft_dif(xs, -1.0)
            for k1 in range(n1):
                tw = (tw_ref[k1, 0, :, ls], tw_ref[k1, 1, :, ls])
                yr, yi = _cmul(X[k1], tw)
                g_scr[k1, 0, rs, ls] = yr
                g_scr[k1, 1, rs, ls] = yi
        return carry

    lax.fori_loop(0, rows_total // SUBLANES, body, 0)


def _fspec_kernel(k_ref, fcat_ref, tw_ref, out_ref, g_scr, *, n1, rows, group):
    def load(i, rs, ls):
        return (k_ref[rs, i * DFT_LANE + ls.start:i * DFT_LANE + ls.stop], None)

    _stage_a_forward(load, g_scr, tw_ref, n1, rows)
    inv_n = 1.0 / (n1 * DFT_LANE)
    for g0 in range(0, n1, group):
        hr, hi = _mxu_dft(g_scr[g0:g0 + group], fcat_ref[...], group, rows, False, precision=lax.Precision.HIGHEST)
        out_ref[0, g0:g0 + group, 0] = hr * inv_n
        out_ref[0, g0:g0 + group, 1] = hi * inv_n


def _fspec(kt, fcat, tw, n1, rows, group):
    n_rows, n = kt.shape
    tiles_per_order = HY_WIDTH // rows
    return pl.pallas_call(
        functools.partial(_fspec_kernel, n1=n1, rows=rows, group=group),
        grid=(n_rows // rows,),
        in_specs=[pl.BlockSpec((rows, n), lambda i: (i, 0)), _resident(fcat.shape), _resident(tw.shape)],
        out_specs=pl.BlockSpec((1, n1, 2, rows, DFT_LANE),
                               lambda i: (i // tiles_per_order, 0, 0, i % tiles_per_order, 0)),
        out_shape=jax.ShapeDtypeStruct((n_rows // HY_WIDTH, n1, 2, HY_WIDTH, DFT_LANE), F32),
        scratch_shapes=[pltpu.VMEM((n1, 2, rows, DFT_LANE), F32)],
        compiler_params=_cparams(("parallel",)),
        name="fspec",
    )(kt, fcat, tw)


def _hyena_kernel(hy_ref, wsh_ref, bsh_ref, ksp_ref, bias_ref, fcat_ref, tw_ref, out_ref,
                  u_scr, x1_scr, x2_scr, g_scr, *, L, n1, ct, group):
    lane = lax.broadcasted_iota(jnp.int32, (ct, L), 1)
    first, last = lane == 0, lane == L - 1
    for bi in range(2):
        for g, dst in enumerate((u_scr, x1_scr, x2_scr)):
            p = hy_ref[bi, g].astype(F32)
            left = jnp.where(first, 0.0, pltpu.roll(p, 1, 1))
            right = jnp.where(last, 0.0, pltpu.roll(p, L - 1, 1))
            conv = wsh_ref[g, 0] * left + wsh_ref[g, 1] * p + wsh_ref[g, 2] * right + bsh_ref[g]
            if g == 2:
                z = hy_ref[bi, 3].astype(F32)
                conv = conv * (z * jax.nn.sigmoid(z))
            dst[bi] = conv

    half_blocks = n1 // 2
    fcat = fcat_ref[...]

    for order in range(2):
        def load(i, rs, ls):
            if i >= half_blocks:
                return None
            cs = slice(i * DFT_LANE + ls.start, i * DFT_LANE + ls.stop)
            return (u_scr[0, rs, cs], u_scr[1, rs, cs])

        _stage_a_forward(load, g_scr, tw_ref, n1, ct)

        for g0 in range(0, n1, group):
            hr, hi = _mxu_dft(g_scr[g0:g0 + group], fcat, group, ct, False)
            kr = ksp_ref[order, g0:g0 + group, 0]
            ki = ksp_ref[order, g0:g0 + group, 1]
            g_scr[g0:g0 + group, 0] = hr * kr - hi * ki
            g_scr[g0:g0 + group, 1] = hr * ki + hi * kr
            qr, qi = _mxu_dft(g_scr[g0:g0 + group], fcat, group, ct, True)
            twr = tw_ref[g0:g0 + group, 0, 0:1, :]
            twi = tw_ref[g0:g0 + group, 1, 0:1, :]
            g_scr[g0:g0 + group, 0] = qr * twr + qi * twi
            g_scr[g0:g0 + group, 1] = qi * twr - qr * twi

        def body(r, carry, order=order):
            rs = pl.ds(pl.multiple_of(r * SUBLANES, SUBLANES), SUBLANES)
            bias = bias_ref[order, rs, :]
            for half in range(DFT_LANE // LANES):
                ls = slice(half * LANES, (half + 1) * LANES)
                q = [(g_scr[k1, 0, rs, ls], g_scr[k1, 1, rs, ls]) for k1 in range(n1)]
                ev = _fft_dif(q[0::2], 1.0)
                od = _fft_dif(q[1::2], 1.0)
                for i in range(half_blocks):
                    y = _cadd(ev[i], _cmulc(od[i], np.exp(2j * np.pi * i / n1)))
                    cs = slice(i * DFT_LANE + ls.start, i * DFT_LANE + ls.stop)
                    for bi in range(2):
                        u = u_scr[bi, rs, cs]
                        conv = y[bi] + bias * u
                        if order == 0:
                            u_scr[bi, rs, cs] = x1_scr[bi, rs, cs] * conv
                        else:
                            out_ref[bi, rs, cs] = x2_scr[bi, rs, cs] * conv
            return carry

        lax.fori_loop(0, ct // SUBLANES, body, 0)


def _hyena(hyt4, wsh, bsh, kspec, bias, fcat, tw, n1, ct, group):
    B, _, C, L = hyt4.shape
    return pl.pallas_call(
        functools.partial(_hyena_kernel, L=L, n1=n1, ct=ct, group=group),
        grid=(C // ct, B // 2),
        in_specs=[
            pl.BlockSpec((2, 4, ct, L), lambda c, p: (p, 0, c, 0)),
            pl.BlockSpec((3, 3, ct, 1), lambda c, p: (0, 0, c, 0)),
            pl.BlockSpec((3, ct, 1), lambda c, p: (0, c, 0)),
            pl.BlockSpec((2, n1, 2, ct, DFT_LANE), lambda c, p: (0, 0, 0, c, 0), pipeline_mode=pl.Buffered(1)),
            pl.BlockSpec((2, ct, 1), lambda c, p: (0, c, 0)),
            _resident(fcat.shape), _resident(tw.shape),
        ],
        out_specs=pl.BlockSpec((2, ct, L), lambda c, p: (p, c, 0)),
        out_shape=jax.ShapeDtypeStruct((B, C, L), F32),
        scratch_shapes=[pltpu.VMEM((2, ct, L), F32), pltpu.VMEM((2, ct, L), F32), pltpu.VMEM((2, ct, L), F32),
                        pltpu.VMEM((n1, 2, ct, DFT_LANE), F32)],
        compiler_params=_cparams(("parallel", "parallel")),
        name="hyena",
    )(hyt4, wsh, bsh, kspec, bias, fcat, tw)


def _qkv_kernel(cq_ref, ckv_ref, kr_ref, ct_ref, st_ref, wq_ref, wqs_ref, wk_ref, wv_ref,
                gq_ref, gqs_ref, gk_ref, gks_ref, q_ref, k_ref, v_ref):
    cq = cq_ref[0]
    ckv = ckv_ref[0]
    ctab = ct_ref[...]
    stab = st_ref[...]
    kr = kr_ref[0]
    kr_a = kr[:, :QK_DIM]
    kr_s = kr[:, LANES:LANES + QK_DIM]
    scale = 1.0 / math.sqrt(QK_DIM)
    qc, qs = gq_ref[...] * ctab, gqs_ref[...] * stab
    kc, ks = gk_ref[...] * ctab, gks_ref[...] * stab
    for h in range(MLA_HEADS):
        qa = jnp.dot(cq, wq_ref[h], preferred_element_type=F32)
        qw = jnp.dot(cq, wqs_ref[h], preferred_element_type=F32)
        r = lax.rsqrt(jnp.mean(qa * qa, axis=-1, keepdims=True) + EPS) * scale
        q_ref[0, h] = ((qa * qc + qw * qs) * r).astype(BF16)
        ka = jnp.dot(ckv, wk_ref[h], preferred_element_type=F32) + kr_a
        rk = lax.rsqrt(jnp.mean(ka * ka, axis=-1, keepdims=True) + EPS)
        k_ref[0, h] = ((ka * kc + kr_s * ks) * rk).astype(BF16)
    for p in range(HEAD_PAIRS):
        v_ref[0, p] = jnp.dot(ckv, wv_ref[p], preferred_element_type=F32).astype(BF16)


def _qkv(cq, ckv, kr, ctab, stab, wq, wqs, wk, wv, gq, gqs, gk, gks, tl):
    B, L, _ = cq.shape
    tok = lambda w: pl.BlockSpec((1, tl, w), lambda b, i: (b, i, 0))
    tab = pl.BlockSpec((tl, QK_DIM), lambda b, i: (i, 0))
    return pl.pallas_call(
        _qkv_kernel,
        grid=(B, L // tl),
        in_specs=[tok(Q_LORA), tok(KV_LORA), tok(KR_WIDE), tab, tab,
                  _resident(wq.shape), _resident(wqs.shape), _resident(wk.shape), _resident(wv.shape),
                  _resident(gq.shape), _resident(gqs.shape), _resident(gk.shape), _resident(gks.shape)],
        out_specs=[pl.BlockSpec((1, MLA_HEADS, tl, QK_DIM), lambda b, i: (b, 0, i, 0)),
                   pl.BlockSpec((1, MLA_HEADS, tl, QK_DIM), lambda b, i: (b, 0, i, 0)),
                   pl.BlockSpec((1, HEAD_PAIRS, tl, 2 * V_DIM), lambda b, i: (b, 0, i, 0))],
        out_shape=[jax.ShapeDtypeStruct((B, MLA_HEADS, L, QK_DIM), BF16),
                   jax.ShapeDtypeStruct((B, MLA_HEADS, L, QK_DIM), BF16),
                   jax.ShapeDtypeStruct((B, HEAD_PAIRS, L, 2 * V_DIM), BF16)],
        compiler_params=_cparams(("parallel", "parallel")),
        name="qkv",
    )(cq, ckv, kr, ctab, stab, wq, wqs, wk, wv, gq, gqs, gk, gks)


def _attn_kernel(q_ref, k_ref, v_ref, o_ref):
    v = v_ref[0, 0]
    lane = lax.broadcasted_iota(jnp.int32, v.shape, 1)

    def head(j, acc):
        s = lax.dot_general(q_ref[0, j], k_ref[0, j], _NT, preferred_element_type=F32)
        m = jnp.max(s, axis=-1, keepdims=True)
        p = jnp.exp(s - m)
        l = jnp.sum(p, axis=-1, keepdims=True)
        vj = jnp.where((lane >= j * V_DIM) & (lane < (j + 1) * V_DIM), v, jnp.zeros_like(v))
        return acc + jnp.dot(p.astype(BF16), vj, preferred_element_type=F32) / l

    acc = lax.fori_loop(0, 2, head, jnp.zeros(o_ref.shape[1:], F32))
    o_ref[0] = acc.astype(BF16)


def _attn(q, k, v, tq):
    B, _, L, _ = q.shape
    return pl.pallas_call(
        _attn_kernel,
        grid=(B, HEAD_PAIRS, L // tq),
        in_specs=[pl.BlockSpec((1, 2, tq, QK_DIM), lambda b, p, i: (b, p, i, 0)),
                  pl.BlockSpec((1, 2, L, QK_DIM), lambda b, p, i: (b, p, 0, 0)),
                  pl.BlockSpec((1, 1, L, 2 * V_DIM), lambda b, p, i: (b, p, 0, 0))],
        out_specs=pl.BlockSpec((1, tq, 2 * V_DIM), lambda b, p, i: (b, i, p)),
        out_shape=jax.ShapeDtypeStruct((B, L, ATTN_WIDTH), BF16),
        compiler_params=_cparams(("parallel", "parallel", "parallel")),
        name="attn",
    )(q, k, v)


def _final_kernel(x_ref, yt_ref, at_ref, zat_ref, gate_ref, bg_ref, why_ref, wat_ref, wout_ref, o_ref):
    yt = yt_ref[0]
    y = yt.T.astype(BF16)
    u_hy = jnp.dot(y, why_ref[...], preferred_element_type=F32)
    z = zat_ref[0].astype(F32)
    a = (at_ref[0].astype(F32) * (z * jax.nn.sigmoid(z))).astype(BF16)
    u_at = jnp.dot(a, wat_ref[...], preferred_element_type=F32)
    gates = jax.nn.sigmoid(gate_ref[0].astype(F32) + bg_ref[...])
    merged = gates[:, :D_MODEL] * u_hy + gates[:, D_MODEL:] * u_at
    o_ref[0] = x_ref[0] + jnp.dot(merged.astype(BF16), wout_ref[...], preferred_element_type=F32)


def _final(x, yt, at, zat, gate, bg, why, wat, wout, tl):
    B, L, _ = x.shape
    tok = lambda w: pl.BlockSpec((1, tl, w), lambda b, i: (b, i, 0))
    return pl.pallas_call(
        _final_kernel,
        grid=(B, L // tl),
        in_specs=[tok(D_MODEL), pl.BlockSpec((1, HY_WIDTH, tl), lambda b, i: (b, 0, i)),
                  tok(ATTN_WIDTH), tok(ATTN_WIDTH), tok(2 * D_MODEL),
                  _resident(bg.shape), _resident(why.shape), _resident(wat.shape), _resident(wout.shape)],
        out_specs=tok(D_MODEL),
        out_shape=jax.ShapeDtypeStruct((B, L, D_MODEL), F32),
        compiler_params=_cparams(("parallel", "parallel")),
        name="final",
    )(x, yt, at, zat, gate, bg, why, wat, wout)


def _layer(x, g_norm, w_in, b_gate, w_short, b_short, w_f1, b_f1, freq_1, w_f2, b_f2, freq_2, w_f3,
           hy_bias, w_hy_out, g_cq, w_uq, g_ckv, w_ukv, g_qn, g_kn, w_attn_out, w_out):
    B, L, _ = x.shape
    n = 2 * L
    n1 = n // DFT_LANE
    assert B % 2 == 0 and n1 * DFT_LANE == n and n1 >= 2 and (n1 & (n1 - 1)) == 0
    tl = min(512, L)
    ct = 64
    group = min(8, n1)

    o_z, o_cq, o_ckv, o_kr, o_za = 3 * HY_WIDTH, 4 * HY_WIDTH, 4 * HY_WIDTH + Q_LORA, 4 * HY_WIDTH + Q_LORA + KV_LORA, \
        4 * HY_WIDTH + Q_LORA + KV_LORA + QK_ROPE
    del o_z
    wt_hy = w_in[:, :4 * HY_WIDTH].T.astype(BF16)
    w_cq = w_in[:, o_cq:o_ckv].astype(BF16)
    w_ckv = w_in[:, o_ckv:o_kr].astype(BF16)
    w_rope = w_in[:, o_kr:o_za]
    half = QK_ROPE // 2
    w_rope_sw = jnp.concatenate([w_rope[:, half:], w_rope[:, :half]], axis=1)
    zpad = lambda w: jnp.zeros((D_MODEL, w), F32)
    w_kr = jnp.concatenate([zpad(QK_NOPE), w_rope, zpad(LANES - QK_DIM),
                            zpad(QK_NOPE), w_rope_sw, zpad(LANES - QK_DIM)], axis=1).astype(BF16)
    w_rest = w_in[:, o_za:].astype(BF16)

    perm = np.concatenate([np.arange(QK_NOPE), QK_NOPE + half + np.arange(half), QK_NOPE + np.arange(half)])
    wq = w_uq.reshape(Q_LORA, MLA_HEADS, QK_DIM).transpose(1, 0, 2)
    wqs = wq[:, :, perm].astype(BF16)
    wq = wq.astype(BF16)
    wkv = w_ukv.reshape(KV_LORA, MLA_HEADS, QK_NOPE + V_DIM).transpose(1, 0, 2)
    wk = jnp.concatenate([wkv[:, :, :QK_NOPE], jnp.zeros((MLA_HEADS, KV_LORA, QK_ROPE), F32)], axis=2).astype(BF16)
    wv = wkv[:, :, QK_NOPE:].reshape(HEAD_PAIRS, 2, KV_LORA, V_DIM).transpose(0, 2, 1, 3)
    wv = wv.reshape(HEAD_PAIRS, KV_LORA, 2 * V_DIM).astype(BF16)
    gq, gk = g_qn.reshape(1, QK_DIM), g_kn.reshape(1, QK_DIM)
    gqs, gks = gq[:, perm], gk[:, perm]

    hyt, cq, ckv, kr, zat, gate = _in_proj(x, g_norm.reshape(1, D_MODEL), wt_hy, w_cq, w_ckv, w_kr, w_rest,
                                           g_cq.reshape(1, Q_LORA), g_ckv.reshape(1, KV_LORA), tl)

    zfeat, t = _filter_features(L)
    w1 = jnp.concatenate([w_f1, jnp.zeros((LANES - FILTER_EMB, FILTER_HIDDEN), F32)], axis=0)
    w3 = w_f3.reshape(FILTER_HIDDEN, 2, 2, HY_WIDTH)
    w3f = w3[:, :, 0].reshape(FILTER_HIDDEN, 2 * HY_WIDTH).T
    w3b = w3[:, :, 1].reshape(FILTER_HIDDEN, 2 * HY_WIDTH).T
    deltas = np.abs(np.linspace(MIN_DECAY, MAX_DECAY, HY_WIDTH))
    dl = np.tile(deltas, 2).reshape(2 * HY_WIDTH, 1).astype(np.float32)
    row = lambda v: v.reshape(1, FILTER_HIDDEN)
    kt = _filter(jnp.asarray(zfeat), jnp.asarray(zfeat[::-1].copy()), jnp.asarray(t.reshape(1, L)),
                 jnp.asarray(t[::-1].copy().reshape(1, L)), w1, row(b_f1), row(freq_1), w_f2, row(b_f2), row(freq_2),
                 w3f, w3b, jnp.asarray(dl), L, 128)
    fcat32 = jnp.asarray(_dft_cat())
    tw = jnp.asarray(_twiddle(n1))
    kspec = _fspec(kt, fcat32, tw, n1, ct, group)

    wsh = w_short.reshape(3, 3, HY_WIDTH).transpose(1, 0, 2)[..., None]
    bsh = b_short.reshape(3, HY_WIDTH, 1)
    yt = _hyena(hyt.reshape(B, 4, HY_WIDTH, L), wsh, bsh, kspec, hy_bias.reshape(2, HY_WIDTH, 1),
                fcat32.astype(BF16), tw, n1, ct, group)

    ctab, stab = _rope_tables(L)
    q, k, v = _qkv(cq, ckv, kr, jnp.asarray(ctab), jnp.asarray(stab), wq, wqs, wk, wv, gq, gqs, gk, gks, tl)
    at = _attn(q, k, v, min(512, L))

    return _final(x, yt, at, zat, gate, b_gate.reshape(1, 2 * D_MODEL), w_hy_out.astype(BF16),
                  w_attn_out.astype(BF16), w_out.astype(BF16), tl)


def kernel(x, g_norm, w_in, b_gate, w_short, b_short, w_f1, b_f1, freq_1, w_f2, b_f2, freq_2, w_f3, hy_bias, w_hy_out, g_cq, w_uq, g_ckv, w_ukv, g_qn, g_kn, w_attn_out, w_out):
    depth = g_norm.shape[0]
    for l in range(depth):
        x = _layer(x, g_norm[l], w_in[l], b_gate[l], w_short[l], b_short[l], w_f1[l], b_f1[l], freq_1[l], w_f2[l],
                   b_f2[l], freq_2[l], w_f3[l], hy_bias[l], w_hy_out[l], g_cq[l], w_uq[l], g_ckv[l], w_ukv[l],
                   g_qn[l], g_kn[l], w_attn_out[l], w_out[l])
    return x
```

```python
import functools
import math

import numpy as np
import jax
import jax.numpy as jnp
from jax import lax
from jax.experimental import pallas as pl
from jax.experimental.pallas import tpu as pltpu

D_MODEL = 1024
HY_WIDTH = 512
FILTER_EMB = 33
FILTER_HIDDEN = 64
DECAY_TARGET = 1e-2
FAST_DECAY = 0.3
SLOW_DECAY = 1.5
MIN_DECAY = math.log(DECAY_TARGET) / SLOW_DECAY
MAX_DECAY = math.log(DECAY_TARGET) / FAST_DECAY
MLA_HEADS = 8
HEAD_PAIRS = MLA_HEADS // 2
QK_NOPE = 64
QK_ROPE = 32
QK_DIM = QK_NOPE + QK_ROPE
V_DIM = 64
Q_LORA = 384
KV_LORA = 256
ATTN_WIDTH = MLA_HEADS * V_DIM
ROPE_THETA = 10000.0
EPS = 1e-6

LANES = 128
SUBLANES = 8
DFT_LANE = 256
KR_ROWS = 2 * QK_DIM
KV_CHUNK = 512
VMEM_LIMIT = 56 * 1024 * 1024

F32 = jnp.float32
BF16 = jnp.bfloat16
_NT = (((1,), (1,)), ((), ()))
_NN = (((1,), (0,)), ((), ()))


def _cparams(sem):
    return pltpu.CompilerParams(dimension_semantics=sem, vmem_limit_bytes=VMEM_LIMIT)


def _resident(shape):
    nd = len(shape)
    return pl.BlockSpec(shape, lambda *_: (0,) * nd, pipeline_mode=pl.Buffered(1))


def _add(a, b):
    if a is None:
        return b
    if b is None:
        return a
    return a + b


def _sub(a, b):
    if b is None:
        return a
    if a is None:
        return -b
    return a - b


def _cadd(x, y):
    return (_add(x[0], y[0]), _add(x[1], y[1]))


def _csub(x, y):
    return (_sub(x[0], y[0]), _sub(x[1], y[1]))


def _scale(a, s):
    if a is None or s == 0.0:
        return None
    if s == 1.0:
        return a
    if s == -1.0:
        return -a
    return a * s


def _cmulc(x, w):
    wr, wi = float(np.real(w)), float(np.imag(w))
    if abs(wr) < 1e-15:
        wr = 0.0
    if abs(wi) < 1e-15:
        wi = 0.0
    xr, xi = x
    re = _sub(_scale(xr, wr), _scale(xi, wi))
    im = _add(_scale(xr, wi), _scale(xi, wr))
    return (re, im)


def _cmul(x, y):
    xr, xi = x
    yr, yi = y
    re = _sub(None if xr is None else xr * yr, None if xi is None else xi * yi)
    im = _add(None if xr is None else xr * yi, None if xi is None else xi * yr)
    return (re, im)


def _fft_dif(xs, sign):
    n = len(xs)
    if n == 1:
        return xs
    half = n // 2
    a = [_cadd(xs[i], xs[i + half]) for i in range(half)]
    b = [_cmulc(_csub(xs[i], xs[i + half]), np.exp(sign * 2j * np.pi * i / n)) for i in range(half)]
    ev = _fft_dif(a, sign)
    od = _fft_dif(b, sign)
    out = [None] * n
    out[0::2] = ev
    out[1::2] = od
    return out


def _zero_like_none(p, ref):
    return jnp.zeros_like(ref) if p is None else p


def _dft_cat():
    n = np.arange(DFT_LANE)
    ang = 2.0 * np.pi * np.outer(n, n) / DFT_LANE
    return np.concatenate([np.cos(ang), -np.sin(ang)], axis=1).astype(np.float32)


def _twiddle(n1):
    n = n1 * DFT_LANE
    ang = 2.0 * np.pi * np.outer(np.arange(n1), np.arange(DFT_LANE)) / n
    tw = np.stack([np.cos(ang), -np.sin(ang)], axis=1)
    return np.ascontiguousarray(np.broadcast_to(tw[:, :, None, :], (n1, 2, SUBLANES, DFT_LANE))).astype(np.float32)


def _filter_features(L):
    t = np.linspace(0.0, 1.0, L)[:, None]
    bands = (FILTER_EMB - 1) // 2
    f = np.linspace(1e-4, bands - 1, bands)
    ang = (2.0 * np.pi / L) * np.arange(L)[:, None] * f[None, :]
    z = np.concatenate([t, np.cos(ang), -np.sin(ang)], axis=-1)
    zp = np.zeros((L, LANES), np.float32)
    zp[:, :FILTER_EMB] = z
    return zp, t[:, 0].astype(np.float32)


def _rope_tables(L):
    pos = np.arange(L, dtype=np.float64)
    inv_freq = ROPE_THETA ** (-np.arange(0, QK_ROPE, 2, dtype=np.float64) / QK_ROPE)
    ang = pos[:, None] * inv_freq[None, :]
    c, s = np.cos(ang), np.sin(ang)
    ctab = np.concatenate([np.ones((L, QK_NOPE)), c, c], axis=1)
    stab = np.concatenate([np.zeros((L, QK_NOPE)), -s, s], axis=1)
    return ctab.astype(np.float32), stab.astype(np.float32)


def _in_proj_kernel(x_ref, gn_ref, wt_hy_ref, w_cq_ref, w_ckv_ref, w_kr_ref, w_rest_ref, gcq_ref, gckv_ref,
                    hyt_ref, cq_ref, ckv_ref, kr_ref, zat_ref, gate_ref):
    x = x_ref[0]
    ms = jnp.mean(x * x, axis=-1, keepdims=True)
    h = (x * lax.rsqrt(ms + EPS) * gn_ref[...]).astype(BF16)
    hyt_ref[0] = lax.dot_general(wt_hy_ref[...], h, _NT, preferred_element_type=F32).astype(BF16)

    def latent(w_ref, g_ref):
        c = jnp.dot(h, w_ref[...], preferred_element_type=F32)
        r = lax.rsqrt(jnp.mean(c * c, axis=-1, keepdims=True) + EPS)
        return (c * r * g_ref[...]).astype(BF16)

    cq_ref[0] = latent(w_cq_ref, gcq_ref)
    ckv_ref[0] = latent(w_ckv_ref, gckv_ref)
    kr_ref[0] = lax.dot_general(w_kr_ref[...], h, _NT, preferred_element_type=F32)
    rest = jnp.dot(h, w_rest_ref[...], preferred_element_type=F32)
    zat_ref[0] = rest[:, :ATTN_WIDTH].astype(BF16)
    gate_ref[0] = rest[:, ATTN_WIDTH:].astype(BF16)


def _in_proj(x, gn, wt_hy, w_cq, w_ckv, w_kr, w_rest, gcq, gckv, tl):
    B, L, _ = x.shape
    n_rest = w_rest.shape[1]
    return pl.pallas_call(
        _in_proj_kernel,
        grid=(B, L // tl),
        in_specs=[
            pl.BlockSpec((1, tl, D_MODEL), lambda b, i: (b, i, 0)),
            _resident(gn.shape), _resident(wt_hy.shape), _resident(w_cq.shape), _resident(w_ckv.shape),
            _resident(w_kr.shape), _resident(w_rest.shape), _resident(gcq.shape), _resident(gckv.shape),
        ],
        out_specs=[
            pl.BlockSpec((1, 4 * HY_WIDTH, tl), lambda b, i: (b, 0, i)),
            pl.BlockSpec((1, tl, Q_LORA), lambda b, i: (b, i, 0)),
            pl.BlockSpec((1, tl, KV_LORA), lambda b, i: (b, i, 0)),
            pl.BlockSpec((1, KR_ROWS, tl), lambda b, i: (b, 0, i)),
            pl.BlockSpec((1, tl, ATTN_WIDTH), lambda b, i: (b, i, 0)),
            pl.BlockSpec((1, tl, 2 * D_MODEL), lambda b, i: (b, i, 0)),
        ],
        out_shape=[
            jax.ShapeDtypeStruct((B, 4 * HY_WIDTH, L), BF16),
            jax.ShapeDtypeStruct((B, L, Q_LORA), BF16),
            jax.ShapeDtypeStruct((B, L, KV_LORA), BF16),
            jax.ShapeDtypeStruct((B, KR_ROWS, L), F32),
            jax.ShapeDtypeStruct((B, L, ATTN_WIDTH), BF16),
            jax.ShapeDtypeStruct((B, L, n_rest - ATTN_WIDTH), BF16),
        ],
        compiler_params=_cparams(("parallel", "parallel")),
        name="in_proj",
    )(x, gn, wt_hy, w_cq, w_ckv, w_kr, w_rest, gcq, gckv)


def _fmlp_kernel(zt_ref, w1t_ref, b1_ref, f1_ref, w2t_ref, b2_ref, f2_ref, out_ref):
    hp = lax.Precision.HIGHEST
    h = jnp.sin(f1_ref[...] * (jnp.dot(w1t_ref[...], zt_ref[0], precision=hp, preferred_element_type=F32) + b1_ref[...]))
    out_ref[0] = jnp.sin(f2_ref[...] * (jnp.dot(w2t_ref[...], h, precision=hp, preferred_element_type=F32) + b2_ref[...]))


def _fmlp(zt, w1t, b1, f1, w2t, b2, f2):
    _, _, L = zt.shape
    return pl.pallas_call(
        _fmlp_kernel,
        grid=(2,),
        in_specs=[pl.BlockSpec((1, LANES, L), lambda d: (d, 0, 0)), _resident(w1t.shape), _resident(b1.shape),
                  _resident(f1.shape), _resident(w2t.shape), _resident(b2.shape), _resident(f2.shape)],
        out_specs=pl.BlockSpec((1, FILTER_HIDDEN, L), lambda d: (d, 0, 0)),
        out_shape=jax.ShapeDtypeStruct((2, FILTER_HIDDEN, L), F32),
        compiler_params=_cparams(("parallel",)),
        name="fmlp",
    )(zt, w1t, b1, f1, w2t, b2, f2)


def _filter_kernel(h_ref, t_ref, w3f_ref, w3b_ref, dl_ref, out_ref, *, L):
    hp = lax.Precision.HIGHEST
    dl = dl_ref[...]
    for d, w_ref in enumerate((w3f_ref, w3b_ref)):
        k = jnp.dot(w_ref[...], h_ref[d], precision=hp, preferred_element_type=F32)
        out_ref[:, d * L:(d + 1) * L] = k * jnp.exp(-(dl * t_ref[d]))


def _filter(h2t, t2, w3f, w3b, dl, L, rows):
    n_rows = w3f.shape[0]
    row_spec = lambda shape: pl.BlockSpec(shape, lambda i: (i, 0))
    return pl.pallas_call(
        functools.partial(_filter_kernel, L=L),
        grid=(n_rows // rows,),
        in_specs=[_resident(h2t.shape), _resident(t2.shape),
                  row_spec((rows, FILTER_HIDDEN)), row_spec((rows, FILTER_HIDDEN)), row_spec((rows, 1))],
        out_specs=pl.BlockSpec((rows, 2 * L), lambda i: (i, 0)),
        out_shape=jax.ShapeDtypeStruct((n_rows, 2 * L), F32),
        compiler_params=_cparams(("parallel",)),
        name="filt",
    )(h2t, t2, w3f, w3b, dl)


def _mxu_dft(lhs_f32, fcat, groups, rows, inverse, precision=None):
    lhs = lhs_f32.reshape(groups * 2 * rows, DFT_LANE)
    if precision is None:
        lhs = lhs.astype(BF16)
    res = jnp.dot(lhs, fcat, precision=precision, preferred_element_type=F32)
    res = res.reshape(groups, 2, rows, 2 * DFT_LANE)
    a = res[:, 0, :, :DFT_LANE]
    b = res[:, 0, :, DFT_LANE:]
    c = res[:, 1, :, :DFT_LANE]
    d = res[:, 1, :, DFT_LANE:]
    if inverse:
        return a + d, c - b
    return a - d, b + c


def _stage_a_forward(load, g_scr, tw_ref, n1, rows_total):
    def body(r, carry):
        rs = pl.ds(pl.multiple_of(r * SUBLANES, SUBLANES), SUBLANES)
        for half in range(DFT_LANE // LANES):
            ls = slice(half * LANES, (half + 1) * LANES)
            xs = [load(i, rs, ls) for i in range(n1)]
            xs = [(None, None) if x is None else x for x in xs]
            X = _fft_dif(xs, -1.0)
            for k1 in range(n1):
                tw = (tw_ref[k1, 0, :, ls], tw_ref[k1, 1, :, ls])
                yr, yi = _cmul(X[k1], tw)
                g_scr[k1, 0, rs, ls] = yr
                g_scr[k1, 1, rs, ls] = yi
        return carry

    lax.fori_loop(0, rows_total // SUBLANES, body, 0)


def _fspec_kernel(k_ref, fcat_ref, tw_ref, out_ref, g_scr, *, n1, rows, group):
    def load(i, rs, ls):
        return (k_ref[rs, i * DFT_LANE + ls.start:i * DFT_LANE + ls.stop], None)

    _stage_a_forward(load, g_scr, tw_ref, n1, rows)
    inv_n = 1.0 / (n1 * DFT_LANE)
    for g0 in range(0, n1, group):
        hr, hi = _mxu_dft(g_scr[g0:g0 + group], fcat_ref[...], group, rows, False, precision=lax.Precision.HIGHEST)
        out_ref[0, g0:g0 + group, 0] = hr * inv_n
        out_ref[0, g0:g0 + group, 1] = hi * inv_n


def _fspec(kt, fcat, tw, n1, rows, group):
    n_rows, n = kt.shape
    tiles_per_order = HY_WIDTH // rows
    return pl.pallas_call(
        functools.partial(_fspec_kernel, n1=n1, rows=rows, group=group),
        grid=(n_rows // rows,),
        in_specs=[pl.BlockSpec((rows, n), lambda i: (i, 0)), _resident(fcat.shape), _resident(tw.shape)],
        out_specs=pl.BlockSpec((1, n1, 2, rows, DFT_LANE),
                               lambda i: (i // tiles_per_order, 0, 0, i % tiles_per_order, 0)),
        out_shape=jax.ShapeDtypeStruct((n_rows // HY_WIDTH, n1, 2, HY_WIDTH, DFT_LANE), F32),
        scratch_shapes=[pltpu.VMEM((n1, 2, rows, DFT_LANE), F32)],
        compiler_params=_cparams(("parallel",)),
        name="fspec",
    )(kt, fcat, tw)


def _hyena_kernel(hy_ref, wsh_ref, bsh_ref, ksp_ref, bias_ref, fcat_ref, tw_ref, out_ref,
                  u_scr, x1_scr, x2_scr, g_scr, *, L, n1, ct, group):
    lane = lax.broadcasted_iota(jnp.int32, (ct, L), 1)
    first, last = lane == 0, lane == L - 1
    for bi in range(2):
        for g, dst in enumerate((u_scr, x1_scr, x2_scr)):
            p = hy_ref[bi, g].astype(F32)
            left = jnp.where(first, 0.0, pltpu.roll(p, 1, 1))
            right = jnp.where(last, 0.0, pltpu.roll(p, L - 1, 1))
            conv = wsh_ref[g, 0] * left + wsh_ref[g, 1] * p + wsh_ref[g, 2] * right + bsh_ref[g]
            if g == 2:
                z = hy_ref[bi, 3].astype(F32)
                conv = conv * (z * jax.nn.sigmoid(z))
            dst[bi] = conv

    half_blocks = n1 // 2
    fcat = fcat_ref[...]

    for order in range(2):
        def load(i, rs, ls):
            if i >= half_blocks:
                return None
            cs = slice(i * DFT_LANE + ls.start, i * DFT_LANE + ls.stop)
            return (u_scr[0, rs, cs], u_scr[1, rs, cs])

        _stage_a_forward(load, g_scr, tw_ref, n1, ct)

        for g0 in range(0, n1, group):
            hr, hi = _mxu_dft(g_scr[g0:g0 + group], fcat, group, ct, False)
            kr = ksp_ref[order, g0:g0 + group, 0]
            ki = ksp_ref[order, g0:g0 + group, 1]
            g_scr[g0:g0 + group, 0] = hr * kr - hi * ki
            g_scr[g0:g0 + group, 1] = hr * ki + hi * kr
            qr, qi = _mxu_dft(g_scr[g0:g0 + group], fcat, group, ct, True)
            twr = tw_ref[g0:g0 + group, 0, 0:1, :]
            twi = tw_ref[g0:g0 + group, 1, 0:1, :]
            g_scr[g0:g0 + group, 0] = qr * twr + qi * twi
            g_scr[g0:g0 + group, 1] = qi * twr - qr * twi

        def body(r, carry, order=order):
            rs = pl.ds(pl.multiple_of(r * SUBLANES, SUBLANES), SUBLANES)
            bias = bias_ref[order, rs, :]
            for half in range(DFT_LANE // LANES):
                ls = slice(half * LANES, (half + 1) * LANES)
                q = [(g_scr[k1, 0, rs, ls], g_scr[k1, 1, rs, ls]) for k1 in range(n1)]
                ev = _fft_dif(q[0::2], 1.0)
                od = _fft_dif(q[1::2], 1.0)
                for i in range(half_blocks):
                    y = _cadd(ev[i], _cmulc(od[i], np.exp(2j * np.pi * i / n1)))
                    cs = slice(i * DFT_LANE + ls.start, i * DFT_LANE + ls.stop)
                    for bi in range(2):
                        u = u_scr[bi, rs, cs]
                        conv = y[bi] + bias * u
                        if order == 0:
                            u_scr[bi, rs, cs] = x1_scr[bi, rs, cs] * conv
                        else:
                            out_ref[bi, rs, cs] = x2_scr[bi, rs, cs] * conv
            return carry

        lax.fori_loop(0, ct // SUBLANES, body, 0)


def _hyena(hyt4, wsh, bsh, kspec, bias, fcat, tw, n1, ct, group):
    B, _, C, L = hyt4.shape
    return pl.pallas_call(
        functools.partial(_hyena_kernel, L=L, n1=n1, ct=ct, group=group),
        grid=(C // ct, B // 2),
        in_specs=[
            pl.BlockSpec((2, 4, ct, L), lambda c, p: (p, 0, c, 0)),
            pl.BlockSpec((3, 3, ct, 1), lambda c, p: (0, 0, c, 0)),
            pl.BlockSpec((3, ct, 1), lambda c, p: (0, c, 0)),
            pl.BlockSpec((2, n1, 2, ct, DFT_LANE), lambda c, p: (0, 0, 0, c, 0), pipeline_mode=pl.Buffered(1)),
            pl.BlockSpec((2, ct, 1), lambda c, p: (0, c, 0)),
            _resident(fcat.shape), _resident(tw.shape),
        ],
        out_specs=pl.BlockSpec((2, ct, L), lambda c, p: (p, c, 0)),
        out_shape=jax.ShapeDtypeStruct((B, C, L), F32),
        scratch_shapes=[pltpu.VMEM((2, ct, L), F32), pltpu.VMEM((2, ct, L), F32), pltpu.VMEM((2, ct, L), F32),
                        pltpu.VMEM((n1, 2, ct, DFT_LANE), F32)],
        compiler_params=_cparams(("parallel", "parallel")),
        name="hyena",
    )(hyt4, wsh, bsh, kspec, bias, fcat, tw)


def _qkv_kernel(cq_ref, ckv_ref, kr_ref, ct_ref, st_ref, ctt_ref, stt_ref, wq_ref, wqs_ref, wkt_ref, wv_ref,
                gq_ref, gqs_ref, gkc_ref, gksc_ref, ones_ref, q_ref, kt_ref, v_ref):
    cq = cq_ref[0]
    ckv = ckv_ref[0]
    kr_a = kr_ref[0, :QK_DIM, :]
    kr_s = kr_ref[0, QK_DIM:, :]
    scale = math.log2(math.e) / math.sqrt(QK_DIM)
    qc, qs = gq_ref[...] * ct_ref[...], gqs_ref[...] * st_ref[...]
    kc, ks = gkc_ref[...] * ctt_ref[...], gksc_ref[...] * stt_ref[...]
    for h in range(MLA_HEADS):
        qa = jnp.dot(cq, wq_ref[h], preferred_element_type=F32)
        qw = jnp.dot(cq, wqs_ref[h], preferred_element_type=F32)
        r = lax.rsqrt(jnp.mean(qa * qa, axis=-1, keepdims=True) + EPS) * scale
        q_ref[0, h] = ((qa * qc + qw * qs) * r).astype(BF16)
        ka = lax.dot_general(wkt_ref[h], ckv, _NT, preferred_element_type=F32) + kr_a
        rk = lax.rsqrt(jnp.mean(ka * ka, axis=0, keepdims=True) + EPS)
        kt_ref[0, h] = ((ka * kc + kr_s * ks) * rk).astype(BF16)
        v_ref[0, h] = (jnp.dot(ckv, wv_ref[h], preferred_element_type=F32) + ones_ref[h % 2]).astype(BF16)


def _qkv(cq, ckv, kr, ctab, stab, ctab_t, stab_t, wq, wqs, wkt, wv, gq, gqs, gkc, gksc, tl):
    ones = np.zeros((2, 1, 2 * V_DIM), np.float32)
    ones[0, 0, V_DIM] = 1.0
    ones[1, 0, 0] = 1.0
    ones = jnp.asarray(ones)
    B, L, _ = cq.shape
    tok = lambda w: pl.BlockSpec((1, tl, w), lambda b, i: (b, i, 0))
    tab = pl.BlockSpec((tl, QK_DIM), lambda b, i: (i, 0))
    tab_t = pl.BlockSpec((QK_DIM, tl), lambda b, i: (0, i))
    return pl.pallas_call(
        _qkv_kernel,
        grid=(B, L // tl),
        in_specs=[tok(Q_LORA), tok(KV_LORA), pl.BlockSpec((1, KR_ROWS, tl), lambda b, i: (b, 0, i)),
                  tab, tab, tab_t, tab_t,
                  _resident(wq.shape), _resident(wqs.shape), _resident(wkt.shape), _resident(wv.shape),
                  _resident(gq.shape), _resident(gqs.shape), _resident(gkc.shape), _resident(gksc.shape),
                  _resident(ones.shape)],
        out_specs=[pl.BlockSpec((1, MLA_HEADS, tl, QK_DIM), lambda b, i: (b, 0, i, 0)),
                   pl.BlockSpec((1, MLA_HEADS, QK_DIM, tl), lambda b, i: (b, 0, 0, i)),
                   pl.BlockSpec((1, MLA_HEADS, tl, 2 * V_DIM), lambda b, i: (b, 0, i, 0))],
        out_shape=[jax.ShapeDtypeStruct((B, MLA_HEADS, L, QK_DIM), BF16),
                   jax.ShapeDtypeStruct((B, MLA_HEADS, QK_DIM, L), BF16),
                   jax.ShapeDtypeStruct((B, MLA_HEADS, L, 2 * V_DIM), BF16)],
        compiler_params=_cparams(("parallel", "parallel")),
        name="qkv",
    )(cq, ckv, kr, ctab, stab, ctab_t, stab_t, wq, wqs, wkt, wv, gq, gqs, gkc, gksc, ones)


def _attn_kernel(q_ref, kt_ref, v_ref, o_ref, *, L):
    outs = []
    for j in range(2):
        q = q_ref[0, j]
        m = acc = None
        for c0 in range(0, L, KV_CHUNK):
            s = jnp.dot(q, kt_ref[0, j, :, c0:c0 + KV_CHUNK], preferred_element_type=F32)
            mc = jnp.max(s, axis=-1, keepdims=True)
            m_new = mc if m is None else jnp.maximum(m, mc)
            p = jnp.exp2(s - m_new).astype(BF16)
            pv = jnp.dot(p, v_ref[0, j, c0:c0 + KV_CHUNK, :], preferred_element_type=F32)
            acc = pv if m is None else jnp.exp2(m - m_new) * acc + pv
            m = m_new
        ones_lane = (1 - j) * V_DIM
        outs.append(acc / acc[:, ones_lane:ones_lane + 1])
    lane = lax.broadcasted_iota(jnp.int32, outs[0].shape, 1)
    o_ref[0] = jnp.where(lane < V_DIM, outs[0], outs[1]).astype(BF16)


def _attn(q, kt, v, tq):
    B, _, L, _ = q.shape
    return pl.pallas_call(
        functools.partial(_attn_kernel, L=L),
        grid=(B, HEAD_PAIRS, L // tq),
        in_specs=[pl.BlockSpec((1, 2, tq, QK_DIM), lambda b, p, i: (b, p, i, 0)),
                  pl.BlockSpec((1, 2, QK_DIM, L), lambda b, p, i: (b, p, 0, 0)),
                  pl.BlockSpec((1, 2, L, 2 * V_DIM), lambda b, p, i: (b, p, 0, 0))],
        out_specs=pl.BlockSpec((1, tq, 2 * V_DIM), lambda b, p, i: (b, i, p)),
        out_shape=jax.ShapeDtypeStruct((B, L, ATTN_WIDTH), BF16),
        compiler_params=_cparams(("parallel", "parallel", "parallel")),
        name="attn",
    )(q, kt, v)


def _final_kernel(x_ref, yt_ref, at_ref, zat_ref, gate_ref, bg_ref, why_ref, wat_ref, wout_ref, o_ref):
    yt = yt_ref[0]
    y = yt.T.astype(BF16)
    u_hy = jnp.dot(y, why_ref[...], preferred_element_type=F32)
    z = zat_ref[0].astype(F32)
    a = (at_ref[0].astype(F32) * (z * jax.nn.sigmoid(z))).astype(BF16)
    u_at = jnp.dot(a, wat_ref[...], preferred_element_type=F32)
    gates = jax.nn.sigmoid(gate_ref[0].astype(F32) + bg_ref[...])
    merged = gates[:, :D_MODEL] * u_hy + gates[:, D_MODEL:] * u_at
    o_ref[0] = x_ref[0] + jnp.dot(merged.astype(BF16), wout_ref[...], preferred_element_type=F32)


def _final(x, yt, at, zat, gate, bg, why, wat, wout, tl):
    B, L, _ = x.shape
    tok = lambda w: pl.BlockSpec((1, tl, w), lambda b, i: (b, i, 0))
    return pl.pallas_call(
        _final_kernel,
        grid=(B, L // tl),
        in_specs=[tok(D_MODEL), pl.BlockSpec((1, HY_WIDTH, tl), lambda b, i: (b, 0, i)),
                  tok(ATTN_WIDTH), tok(ATTN_WIDTH), tok(2 * D_MODEL),
                  _resident(bg.shape), _resident(why.shape), _resident(wat.shape), _resident(wout.shape)],
        out_specs=tok(D_MODEL),
        out_shape=jax.ShapeDtypeStruct((B, L, D_MODEL), F32),
        compiler_params=_cparams(("parallel", "parallel")),
        name="final",
    )(x, yt, at, zat, gate, bg, why, wat, wout)


def _layer(x, g_norm, w_in, b_gate, w_short, b_short, w_f1, b_f1, freq_1, w_f2, b_f2, freq_2, w_f3,
           hy_bias, w_hy_out, g_cq, w_uq, g_ckv, w_ukv, g_qn, g_kn, w_attn_out, w_out):
    B, L, _ = x.shape
    n = 2 * L
    n1 = n // DFT_LANE
    assert B % 2 == 0 and n1 * DFT_LANE == n and n1 >= 2 and (n1 & (n1 - 1)) == 0
    tl = min(512, L)
    ct = 64
    group = min(8, n1)

    o_z, o_cq, o_ckv, o_kr, o_za = 3 * HY_WIDTH, 4 * HY_WIDTH, 4 * HY_WIDTH + Q_LORA, 4 * HY_WIDTH + Q_LORA + KV_LORA, \
        4 * HY_WIDTH + Q_LORA + KV_LORA + QK_ROPE
    del o_z
    wt_hy = w_in[:, :4 * HY_WIDTH].T.astype(BF16)
    w_cq = w_in[:, o_cq:o_ckv].astype(BF16)
    w_ckv = w_in[:, o_ckv:o_kr].astype(BF16)
    w_rope = w_in[:, o_kr:o_za]
    half = QK_ROPE // 2
    w_rope_sw = jnp.concatenate([w_rope[:, half:], w_rope[:, :half]], axis=1)
    zpad = jnp.zeros((D_MODEL, QK_NOPE), F32)
    wt_kr = jnp.concatenate([zpad, w_rope, zpad, w_rope_sw], axis=1).T.astype(BF16)
    w_rest = w_in[:, o_za:].astype(BF16)

    perm = np.concatenate([np.arange(QK_NOPE), QK_NOPE + half + np.arange(half), QK_NOPE + np.arange(half)])
    wq = w_uq.reshape(Q_LORA, MLA_HEADS, QK_DIM).transpose(1, 0, 2)
    wqs = wq[:, :, perm].astype(BF16)
    wq = wq.astype(BF16)
    wkv = w_ukv.reshape(KV_LORA, MLA_HEADS, QK_NOPE + V_DIM).transpose(1, 0, 2)
    wk = jnp.concatenate([wkv[:, :, :QK_NOPE], jnp.zeros((MLA_HEADS, KV_LORA, QK_ROPE), F32)], axis=2)
    wkt = wk.transpose(0, 2, 1).astype(BF16)
    wv_h = wkv[:, :, QK_NOPE:]
    wv_z = jnp.zeros_like(wv_h)
    odd = (np.arange(MLA_HEADS) % 2 == 1).reshape(MLA_HEADS, 1, 1)
    wv = jnp.where(odd, jnp.concatenate([wv_z, wv_h], axis=2), jnp.concatenate([wv_h, wv_z], axis=2)).astype(BF16)
    gq = g_qn.reshape(1, QK_DIM)
    gqs = gq[:, perm]
    gkc = g_kn.reshape(QK_DIM, 1)
    gksc = gkc[perm]

    hyt, cq, ckv, kr, zat, gate = _in_proj(x, g_norm.reshape(1, D_MODEL), wt_hy, w_cq, w_ckv, wt_kr, w_rest,
                                           g_cq.reshape(1, Q_LORA), g_ckv.reshape(1, KV_LORA), tl)

    zfeat, t = _filter_features(L)
    w1 = jnp.concatenate([w_f1, jnp.zeros((LANES - FILTER_EMB, FILTER_HIDDEN), F32)], axis=0)
    w3 = w_f3.reshape(FILTER_HIDDEN, 2, 2, HY_WIDTH)
    w3f = w3[:, :, 0].reshape(FILTER_HIDDEN, 2 * HY_WIDTH).T
    w3b = w3[:, :, 1].reshape(FILTER_HIDDEN, 2 * HY_WIDTH).T
    deltas = np.abs(np.linspace(MIN_DECAY, MAX_DECAY, HY_WIDTH))
    dl = np.tile(deltas, 2).reshape(2 * HY_WIDTH, 1).astype(np.float32)
    col = lambda v: v.reshape(FILTER_HIDDEN, 1)
    zt = np.ascontiguousarray(np.stack([zfeat.T, zfeat[::-1].T]))
    t2 = np.ascontiguousarray(np.stack([t, t[::-1]]).reshape(2, 1, L))
    h2t = _fmlp(jnp.asarray(zt), w1.T, col(b_f1), col(freq_1), w_f2.T, col(b_f2), col(freq_2))
    kt = _filter(h2t, jnp.asarray(t2), w3f, w3b, jnp.asarray(dl), L, 128)
    fcat32 = jnp.asarray(_dft_cat())
    tw = jnp.asarray(_twiddle(n1))
    kspec = _fspec(kt, fcat32, tw, n1, ct, group)

    wsh = w_short.reshape(3, 3, HY_WIDTH).transpose(1, 0, 2)[..., None]
    bsh = b_short.reshape(3, HY_WIDTH, 1)
    yt = _hyena(hyt.reshape(B, 4, HY_WIDTH, L), wsh, bsh, kspec, hy_bias.reshape(2, HY_WIDTH, 1),
                fcat32.astype(BF16), tw, n1, ct, group)

    ctab, stab = _rope_tables(L)
    q, kt_, v = _qkv(cq, ckv, kr, jnp.asarray(ctab), jnp.asarray(stab), jnp.asarray(np.ascontiguousarray(ctab.T)),
                     jnp.asarray(np.ascontiguousarray(stab.T)), wq, wqs, wkt, wv, gq, gqs, gkc, gksc, tl)
    at = _attn(q, kt_, v, min(512, L))

    return _final(x, yt, at, zat, gate, b_gate.reshape(1, 2 * D_MODEL), w_hy_out.astype(BF16),
                  w_attn_out.astype(BF16), w_out.astype(BF16), tl)


def kernel(x, g_norm, w_in, b_gate, w_short, b_short, w_f1, b_f1, freq_1, w_f2, b_f2, freq_2, w_f3, hy_bias, w_hy_out, g_cq, w_uq, g_ckv, w_ukv, g_qn, g_kn, w_attn_out, w_out):
    depth = g_norm.shape[0]
    for l in range(depth):
        x = _layer(x, g_norm[l], w_in[l], b_gate[l], w_short[l], b_short[l], w_f1[l], b_f1[l], freq_1[l], w_f2[l],
                   b_f2[l], freq_2[l], w_f3[l], hy_bias[l], w_hy_out[l], g_cq[l], w_uq[l], g_ckv[l], w_ukv[l],
                   g_qn[l], g_kn[l], w_attn_out[l], w_out[l])
    return x
```

```python
import functools
import math

import numpy as np
import jax
import jax.numpy as jnp
from jax import lax
from jax.experimental import pallas as pl
from jax.experimental.pallas import tpu as pltpu

D_MODEL = 1024
HY_WIDTH = 512
FILTER_EMB = 33
FILTER_HIDDEN = 64
DECAY_TARGET = 1e-2
FAST_DECAY = 0.3
SLOW_DECAY = 1.5
MIN_DECAY = math.log(DECAY_TARGET) / SLOW_DECAY
MAX_DECAY = math.log(DECAY_TARGET) / FAST_DECAY
MLA_HEADS = 8
HEAD_PAIRS = MLA_HEADS // 2
QK_NOPE = 64
QK_ROPE = 32
QK_DIM = QK_NOPE + QK_ROPE
V_DIM = 64
Q_LORA = 384
KV_LORA = 256
ATTN_WIDTH = MLA_HEADS * V_DIM
ROPE_THETA = 10000.0
EPS = 1e-6

LANES = 128
SUBLANES = 8
DFT_LANE = 256
KR_ROWS = 2 * QK_DIM
KV_CHUNK = 512
VMEM_LIMIT = 56 * 1024 * 1024

F32 = jnp.float32
BF16 = jnp.bfloat16
_NT = (((1,), (1,)), ((), ()))
_NN = (((1,), (0,)), ((), ()))


def _cparams(sem):
    return pltpu.CompilerParams(dimension_semantics=sem, vmem_limit_bytes=VMEM_LIMIT)


def _resident(shape):
    nd = len(shape)
    return pl.BlockSpec(shape, lambda *_: (0,) * nd, pipeline_mode=pl.Buffered(1))


def _add(a, b):
    if a is None:
        return b
    if b is None:
        return a
    return a + b


def _sub(a, b):
    if b is None:
        return a
    if a is None:
        return -b
    return a - b


def _cadd(x, y):
    return (_add(x[0], y[0]), _add(x[1], y[1]))


def _csub(x, y):
    return (_sub(x[0], y[0]), _sub(x[1], y[1]))


def _scale(a, s):
    if a is None or s == 0.0:
        return None
    if s == 1.0:
        return a
    if s == -1.0:
        return -a
    return a * s


def _cmulc(x, w):
    wr, wi = float(np.real(w)), float(np.imag(w))
    if abs(wr) < 1e-15:
        wr = 0.0
    if abs(wi) < 1e-15:
        wi = 0.0
    xr, xi = x
    re = _sub(_scale(xr, wr), _scale(xi, wi))
    im = _add(_scale(xr, wi), _scale(xi, wr))
    return (re, im)


def _cmul(x, y):
    xr, xi = x
    yr, yi = y
    re = _sub(None if xr is None else xr * yr, None if xi is None else xi * yi)
    im = _add(None if xr is None else xr * yi, None if xi is None else xi * yr)
    return (re, im)


def _fft_dif(xs, sign):
    n = len(xs)
    if n == 1:
        return xs
    half = n // 2
    a = [_cadd(xs[i], xs[i + half]) for i in range(half)]
    b = [_cmulc(_csub(xs[i], xs[i + half]), np.exp(sign * 2j * np.pi * i / n)) for i in range(half)]
    ev = _fft_dif(a, sign)
    od = _fft_dif(b, sign)
    out = [None] * n
    out[0::2] = ev
    out[1::2] = od
    return out


def _zero_like_none(p, ref):
    return jnp.zeros_like(ref) if p is None else p


def _dft_cat():
    n = np.arange(DFT_LANE)
    ang = 2.0 * np.pi * np.outer(n, n) / DFT_LANE
    return np.concatenate([np.cos(ang), -np.sin(ang)], axis=1).astype(np.float32)


def _twiddle(n1):
    n = n1 * DFT_LANE
    ang = 2.0 * np.pi * np.outer(np.arange(n1), np.arange(DFT_LANE)) / n
    tw = np.stack([np.cos(ang), -np.sin(ang)], axis=1)
    return np.ascontiguousarray(np.broadcast_to(tw[:, :, None, :], (n1, 2, SUBLANES, DFT_LANE))).astype(np.float32)


def _filter_features(L):
    t = np.linspace(0.0, 1.0, L)[:, None]
    bands = (FILTER_EMB - 1) // 2
    f = np.linspace(1e-4, bands - 1, bands)
    ang = (2.0 * np.pi / L) * np.arange(L)[:, None] * f[None, :]
    z = np.concatenate([t, np.cos(ang), -np.sin(ang)], axis=-1)
    zp = np.zeros((L, LANES), np.float32)
    zp[:, :FILTER_EMB] = z
    return zp, t[:, 0].astype(np.float32)


def _rope_tables(L):
    pos = np.arange(L, dtype=np.float64)
    inv_freq = ROPE_THETA ** (-np.arange(0, QK_ROPE, 2, dtype=np.float64) / QK_ROPE)
    ang = pos[:, None] * inv_freq[None, :]
    c, s = np.cos(ang), np.sin(ang)
    ctab = np.concatenate([np.ones((L, QK_NOPE)), c, c], axis=1)
    stab = np.concatenate([np.zeros((L, QK_NOPE)), -s, s], axis=1)
    return ctab.astype(np.float32), stab.astype(np.float32)


def _in_proj_kernel(x_ref, gn_ref, wt_hy_ref, w_cq_ref, w_ckv_ref, w_kr_ref, w_rest_ref, gcq_ref, gckv_ref,
                    hyt_ref, cq_ref, ckv_ref, kr_ref, zat_ref, gate_ref):
    x = x_ref[0]
    ms = jnp.mean(x * x, axis=-1, keepdims=True)
    h = (x * lax.rsqrt(ms + EPS) * gn_ref[...]).astype(BF16)
    hyt_ref[0] = lax.dot_general(wt_hy_ref[...], h, _NT, preferred_element_type=F32).astype(BF16)

    def latent(w_ref, g_ref):
        c = jnp.dot(h, w_ref[...], preferred_element_type=F32)
        r = lax.rsqrt(jnp.mean(c * c, axis=-1, keepdims=True) + EPS)
        return (c * r * g_ref[...]).astype(BF16)

    cq_ref[0] = latent(w_cq_ref, gcq_ref)
    ckv_ref[0] = latent(w_ckv_ref, gckv_ref)
    kr_ref[0] = lax.dot_general(w_kr_ref[...], h, _NT, preferred_element_type=F32)
    rest = jnp.dot(h, w_rest_ref[...], preferred_element_type=F32)
    zat_ref[0] = rest[:, :ATTN_WIDTH].astype(BF16)
    gate_ref[0] = rest[:, ATTN_WIDTH:].astype(BF16)


def _in_proj(x, gn, wt_hy, w_cq, w_ckv, w_kr, w_rest, gcq, gckv, tl):
    B, L, _ = x.shape
    n_rest = w_rest.shape[1]
    return pl.pallas_call(
        _in_proj_kernel,
        grid=(B, L // tl),
        in_specs=[
            pl.BlockSpec((1, tl, D_MODEL), lambda b, i: (b, i, 0)),
            _resident(gn.shape), _resident(wt_hy.shape), _resident(w_cq.shape), _resident(w_ckv.shape),
            _resident(w_kr.shape), _resident(w_rest.shape), _resident(gcq.shape), _resident(gckv.shape),
        ],
        out_specs=[
            pl.BlockSpec((1, 4 * HY_WIDTH, tl), lambda b, i: (b, 0, i)),
            pl.BlockSpec((1, tl, Q_LORA), lambda b, i: (b, i, 0)),
            pl.BlockSpec((1, tl, KV_LORA), lambda b, i: (b, i, 0)),
            pl.BlockSpec((1, KR_ROWS, tl), lambda b, i: (b, 0, i)),
            pl.BlockSpec((1, tl, ATTN_WIDTH), lambda b, i: (b, i, 0)),
            pl.BlockSpec((1, tl, 2 * D_MODEL), lambda b, i: (b, i, 0)),
        ],
        out_shape=[
            jax.ShapeDtypeStruct((B, 4 * HY_WIDTH, L), BF16),
            jax.ShapeDtypeStruct((B, L, Q_LORA), BF16),
            jax.ShapeDtypeStruct((B, L, KV_LORA), BF16),
            jax.ShapeDtypeStruct((B, KR_ROWS, L), F32),
            jax.ShapeDtypeStruct((B, L, ATTN_WIDTH), BF16),
            jax.ShapeDtypeStruct((B, L, n_rest - ATTN_WIDTH), BF16),
        ],
        compiler_params=_cparams(("parallel", "parallel")),
        name="in_proj",
    )(x, gn, wt_hy, w_cq, w_ckv, w_kr, w_rest, gcq, gckv)


def _fmlp_kernel(zt_ref, w1t_ref, b1_ref, f1_ref, w2t_ref, b2_ref, f2_ref, out_ref):
    hp = lax.Precision.HIGHEST
    h = jnp.sin(f1_ref[...] * (jnp.dot(w1t_ref[...], zt_ref[0], precision=hp, preferred_element_type=F32) + b1_ref[...]))
    out_ref[0] = jnp.sin(f2_ref[...] * (jnp.dot(w2t_ref[...], h, precision=hp, preferred_element_type=F32) + b2_ref[...]))


def _fmlp(zt, w1t, b1, f1, w2t, b2, f2):
    _, _, L = zt.shape
    return pl.pallas_call(
        _fmlp_kernel,
        grid=(2,),
        in_specs=[pl.BlockSpec((1, LANES, L), lambda d: (d, 0, 0)), _resident(w1t.shape), _resident(b1.shape),
                  _resident(f1.shape), _resident(w2t.shape), _resident(b2.shape), _resident(f2.shape)],
        out_specs=pl.BlockSpec((1, FILTER_HIDDEN, L), lambda d: (d, 0, 0)),
        out_shape=jax.ShapeDtypeStruct((2, FILTER_HIDDEN, L), F32),
        compiler_params=_cparams(("parallel",)),
        name="fmlp",
    )(zt, w1t, b1, f1, w2t, b2, f2)


def _filter_kernel(h_ref, t_ref, w3f_ref, w3b_ref, dl_ref, out_ref, *, L):
    hp = lax.Precision.HIGHEST
    dl = dl_ref[...]
    for d, w_ref in enumerate((w3f_ref, w3b_ref)):
        k = jnp.dot(w_ref[...], h_ref[d], precision=hp, preferred_element_type=F32)
        out_ref[:, d * L:(d + 1) * L] = k * jnp.exp(-(dl * t_ref[d]))


def _filter(h2t, t2, w3f, w3b, dl, L, rows):
    n_rows = w3f.shape[0]
    row_spec = lambda shape: pl.BlockSpec(shape, lambda i: (i, 0))
    return pl.pallas_call(
        functools.partial(_filter_kernel, L=L),
        grid=(n_rows // rows,),
        in_specs=[_resident(h2t.shape), _resident(t2.shape),
                  row_spec((rows, FILTER_HIDDEN)), row_spec((rows, FILTER_HIDDEN)), row_spec((rows, 1))],
        out_specs=pl.BlockSpec((rows, 2 * L), lambda i: (i, 0)),
        out_shape=jax.ShapeDtypeStruct((n_rows, 2 * L), F32),
        compiler_params=_cparams(("parallel",)),
        name="filt",
    )(h2t, t2, w3f, w3b, dl)


def _mxu_dft(lhs_f32, fcat, groups, rows, inverse, precision=None):
    lhs = lhs_f32.reshape(groups * 2 * rows, DFT_LANE)
    if precision is None:
        lhs = lhs.astype(BF16)
    res = jnp.dot(lhs, fcat, precision=precision, preferred_element_type=F32)
    res = res.reshape(groups, 2, rows, 2 * DFT_LANE)
    a = res[:, 0, :, :DFT_LANE]
    b = res[:, 0, :, DFT_LANE:]
    c = res[:, 1, :, :DFT_LANE]
    d = res[:, 1, :, DFT_LANE:]
    if inverse:
        return a + d, c - b
    return a - d, b + c


def _stage_a_forward(load, g_scr, tw_ref, n1, rows_total):
    def body(r, carry):
        rs = pl.ds(pl.multiple_of(r * SUBLANES, SUBLANES), SUBLANES)
        for half in range(DFT_LANE // LANES):
            ls = slice(half * LANES, (half + 1) * LANES)
            xs = [load(i, rs, ls) for i in range(n1)]
            xs = [(None, None) if x is None else x for x in xs]
            X = _fft_dif(xs, -1.0)
            for k1 in range(n1):
                tw = (tw_ref[k1, 0, :, ls], tw_ref[k1, 1, :, ls])
                yr, yi = _cmul(X[k1], tw)
                g_scr[k1, 0, rs, ls] = yr
                g_scr[k1, 1, rs, ls] = yi
        return carry

    lax.fori_loop(0, rows_total // SUBLANES, body, 0)


def _fspec_kernel(k_ref, fcat_ref, tw_ref, out_ref, g_scr, *, n1, rows, group):
    def load(i, rs, ls):
        return (k_ref[rs, i * DFT_LANE + ls.start:i * DFT_LANE + ls.stop], None)

    _stage_a_forward(load, g_scr, tw_ref, n1, rows)
    inv_n = 1.0 / (n1 * DFT_LANE)
    for g0 in range(0, n1, group):
        hr, hi = _mxu_dft(g_scr[g0:g0 + group], fcat_ref[...], group, rows, False, precision=lax.Precision.HIGHEST)
        out_ref[0, g0:g0 + group, 0] = hr * inv_n
        out_ref[0, g0:g0 + group, 1] = hi * inv_n


def _fspec(kt, fcat, tw, n1, rows, group):
    n_rows, n = kt.shape
    tiles_per_order = HY_WIDTH // rows
    return pl.pallas_call(
        functools.partial(_fspec_kernel, n1=n1, rows=rows, group=group),
        grid=(n_rows // rows,),
        in_specs=[pl.BlockSpec((rows, n), lambda i: (i, 0)), _resident(fcat.shape), _resident(tw.shape)],
        out_specs=pl.BlockSpec((1, n1, 2, rows, DFT_LANE),
                               lambda i: (i // tiles_per_order, 0, 0, i % tiles_per_order, 0)),
        out_shape=jax.ShapeDtypeStruct((n_rows // HY_WIDTH, n1, 2, HY_WIDTH, DFT_LANE), F32),
        scratch_shapes=[pltpu.VMEM((n1, 2, rows, DFT_LANE), F32)],
        compiler_params=_cparams(("parallel",)),
        name="fspec",
    )(kt, fcat, tw)


def _hyena_kernel(hy_ref, wsh_ref, bsh_ref, ksp_ref, bias_ref, fcat_ref, tw_ref, out_ref,
                  u_scr, x1_scr, x2_scr, g_scr, *, L, n1, ct, group):
    lane = lax.broadcasted_iota(jnp.int32, (ct, L), 1)
    first, last = lane == 0, lane == L - 1
    for bi in range(2):
        for g, dst in enumerate((u_scr, x1_scr, x2_scr)):
            p = hy_ref[bi, g].astype(F32)
            left = jnp.where(first, 0.0, pltpu.roll(p, 1, 1))
            right = jnp.where(last, 0.0, pltpu.roll(p, L - 1, 1))
            conv = wsh_ref[g, 0] * left + wsh_ref[g, 1] * p + wsh_ref[g, 2] * right + bsh_ref[g]
            if g == 2:
                z = hy_ref[bi, 3].astype(F32)
                conv = conv * (z * jax.nn.sigmoid(z))
            dst[bi] = conv

    half_blocks = n1 // 2
    fcat = fcat_ref[...]

    for order in range(2):
        def load(i, rs, ls):
            if i >= half_blocks:
                return None
            cs = slice(i * DFT_LANE + ls.start, i * DFT_LANE + ls.stop)
            return (u_scr[0, rs, cs], u_scr[1, rs, cs])

        _stage_a_forward(load, g_scr, tw_ref, n1, ct)

        for g0 in range(0, n1, group):
            hr, hi = _mxu_dft(g_scr[g0:g0 + group], fcat, group, ct, False)
            kr = ksp_ref[order, g0:g0 + group, 0]
            ki = ksp_ref[order, g0:g0 + group, 1]
            g_scr[g0:g0 + group, 0] = hr * kr - hi * ki
            g_scr[g0:g0 + group, 1] = hr * ki + hi * kr
            qr, qi = _mxu_dft(g_scr[g0:g0 + group], fcat, group, ct, True)
            twr = tw_ref[g0:g0 + group, 0, 0:1, :]
            twi = tw_ref[g0:g0 + group, 1, 0:1, :]
            g_scr[g0:g0 + group, 0] = qr * twr + qi * twi
            g_scr[g0:g0 + group, 1] = qi * twr - qr * twi

        def body(r, carry, order=order):
            rs = pl.ds(pl.multiple_of(r * SUBLANES, SUBLANES), SUBLANES)
            bias = bias_ref[order, rs, :]
            for half in range(DFT_LANE // LANES):
                ls = slice(half * LANES, (half + 1) * LANES)
                q = [(g_scr[k1, 0, rs, ls], g_scr[k1, 1, rs, ls]) for k1 in range(n1)]
                ev = _fft_dif(q[0::2], 1.0)
                od = _fft_dif(q[1::2], 1.0)
                for i in range(half_blocks):
                    y = _cadd(ev[i], _cmulc(od[i], np.exp(2j * np.pi * i / n1)))
                    cs = slice(i * DFT_LANE + ls.start, i * DFT_LANE + ls.stop)
                    for bi in range(2):
                        u = u_scr[bi, rs, cs]
                        conv = y[bi] + bias * u
                        if order == 0:
                            u_scr[bi, rs, cs] = x1_scr[bi, rs, cs] * conv
                        else:
                            out_ref[bi, rs, cs] = x2_scr[bi, rs, cs] * conv
            return carry

        lax.fori_loop(0, ct // SUBLANES, body, 0)


def _hyena(hyt4, wsh, bsh, kspec, bias, fcat, tw, n1, ct, group):
    B, _, C, L = hyt4.shape
    return pl.pallas_call(
        functools.partial(_hyena_kernel, L=L, n1=n1, ct=ct, group=group),
        grid=(C // ct, B // 2),
        in_specs=[
            pl.BlockSpec((2, 4, ct, L), lambda c, p: (p, 0, c, 0)),
            pl.BlockSpec((3, 3, ct, 1), lambda c, p: (0, 0, c, 0)),
            pl.BlockSpec((3, ct, 1), lambda c, p: (0, c, 0)),
            pl.BlockSpec((2, n1, 2, ct, DFT_LANE), lambda c, p: (0, 0, 0, c, 0), pipeline_mode=pl.Buffered(1)),
            pl.BlockSpec((2, ct, 1), lambda c, p: (0, c, 0)),
            _resident(fcat.shape), _resident(tw.shape),
        ],
        out_specs=pl.BlockSpec((2, ct, L), lambda c, p: (p, c, 0)),
        out_shape=jax.ShapeDtypeStruct((B, C, L), F32),
        scratch_shapes=[pltpu.VMEM((2, ct, L), F32), pltpu.VMEM((2, ct, L), F32), pltpu.VMEM((2, ct, L), F32),
                        pltpu.VMEM((n1, 2, ct, DFT_LANE), F32)],
        compiler_params=_cparams(("parallel", "parallel")),
        name="hyena",
    )(hyt4, wsh, bsh, kspec, bias, fcat, tw)


def _qkv_kernel(cq_ref, ckv_ref, kr_ref, ct_ref, st_ref, ctt_ref, stt_ref, wq_ref, wqs_ref, wkt_ref, wv_ref,
                gq_ref, gqs_ref, gkc_ref, gksc_ref, ones_ref, q_ref, kt_ref, v_ref):
    cq = cq_ref[0]
    ckv = ckv_ref[0]
    kr_a = kr_ref[0, :QK_DIM, :]
    kr_s = kr_ref[0, QK_DIM:, :]
    scale = math.log2(math.e) / math.sqrt(QK_DIM)
    qc, qs = gq_ref[...] * ct_ref[...], gqs_ref[...] * st_ref[...]
    kc, ks = gkc_ref[...] * ctt_ref[...], gksc_ref[...] * stt_ref[...]
    for h in range(MLA_HEADS):
        qa = jnp.dot(cq, wq_ref[h], preferred_element_type=F32)
        qw = jnp.dot(cq, wqs_ref[h], preferred_element_type=F32)
        r = lax.rsqrt(jnp.mean(qa * qa, axis=-1, keepdims=True) + EPS) * scale
        q_ref[0, h] = ((qa * qc + qw * qs) * r).astype(BF16)
        ka = lax.dot_general(wkt_ref[h], ckv, _NT, preferred_element_type=F32) + kr_a
        rk = lax.rsqrt(jnp.mean(ka * ka, axis=0, keepdims=True) + EPS)
        kt_ref[0, h] = ((ka * kc + kr_s * ks) * rk).astype(BF16)
        v_ref[0, h] = (jnp.dot(ckv, wv_ref[h], preferred_element_type=F32) + ones_ref[h % 2]).astype(BF16)


def _qkv(cq, ckv, kr, ctab, stab, ctab_t, stab_t, wq, wqs, wkt, wv, gq, gqs, gkc, gksc, tl):
    ones = np.zeros((2, 1, 2 * V_DIM), np.float32)
    ones[0, 0, V_DIM] = 1.0
    ones[1, 0, 0] = 1.0
    ones = jnp.asarray(ones)
    B, L, _ = cq.shape
    tok = lambda w: pl.BlockSpec((1, tl, w), lambda b, i: (b, i, 0))
    tab = pl.BlockSpec((tl, QK_DIM), lambda b, i: (i, 0))
    tab_t = pl.BlockSpec((QK_DIM, tl), lambda b, i: (0, i))
    return pl.pallas_call(
        _qkv_kernel,
        grid=(B, L // tl),
        in_specs=[tok(Q_LORA), tok(KV_LORA), pl.BlockSpec((1, KR_ROWS, tl), lambda b, i: (b, 0, i)),
                  tab, tab, tab_t, tab_t,
                  _resident(wq.shape), _resident(wqs.shape), _resident(wkt.shape), _resident(wv.shape),
                  _resident(gq.shape), _resident(gqs.shape), _resident(gkc.shape), _resident(gksc.shape),
                  _resident(ones.shape)],
        out_specs=[pl.BlockSpec((1, MLA_HEADS, tl, QK_DIM), lambda b, i: (b, 0, i, 0)),
                   pl.BlockSpec((1, MLA_HEADS, QK_DIM, tl), lambda b, i: (b, 0, 0, i)),
                   pl.BlockSpec((1, MLA_HEADS, tl, 2 * V_DIM), lambda b, i: (b, 0, i, 0))],
        out_shape=[jax.ShapeDtypeStruct((B, MLA_HEADS, L, QK_DIM), BF16),
                   jax.ShapeDtypeStruct((B, MLA_HEADS, QK_DIM, L), BF16),
                   jax.ShapeDtypeStruct((B, MLA_HEADS, L, 2 * V_DIM), BF16)],
        compiler_params=_cparams(("parallel", "parallel")),
        name="qkv",
    )(cq, ckv, kr, ctab, stab, ctab_t, stab_t, wq, wqs, wkt, wv, gq, gqs, gkc, gksc, ones)


def _attn_kernel(q_ref, kt_ref, v_ref, o_ref, *, L):
    def scores(j, c0):
        return jnp.dot(q_ref[0, j], kt_ref[0, j, :, c0:c0 + KV_CHUNK], preferred_element_type=F32)

    m = [None, None]
    acc = [None, None]
    nxt = [scores(j, 0) for j in range(2)]
    for c0 in range(0, L, KV_CHUNK):
        cur = nxt
        if c0 + KV_CHUNK < L:
            nxt = [scores(j, c0 + KV_CHUNK) for j in range(2)]
        for j in range(2):
            s = cur[j]
            mc = jnp.max(s, axis=-1, keepdims=True)
            m_new = mc if m[j] is None else jnp.maximum(m[j], mc)
            p = jnp.exp2(s - m_new).astype(BF16)
            pv = jnp.dot(p, v_ref[0, j, c0:c0 + KV_CHUNK, :], preferred_element_type=F32)
            acc[j] = pv if m[j] is None else jnp.exp2(m[j] - m_new) * acc[j] + pv
            m[j] = m_new
    outs = []
    for j in range(2):
        ones_lane = (1 - j) * V_DIM
        outs.append(acc[j] / acc[j][:, ones_lane:ones_lane + 1])
    lane = lax.broadcasted_iota(jnp.int32, outs[0].shape, 1)
    o_ref[0] = jnp.where(lane < V_DIM, outs[0], outs[1]).astype(BF16)


def _attn(q, kt, v, tq):
    B, _, L, _ = q.shape
    return pl.pallas_call(
        functools.partial(_attn_kernel, L=L),
        grid=(B, HEAD_PAIRS, L // tq),
        in_specs=[pl.BlockSpec((1, 2, tq, QK_DIM), lambda b, p, i: (b, p, i, 0)),
                  pl.BlockSpec((1, 2, QK_DIM, L), lambda b, p, i: (b, p, 0, 0)),
                  pl.BlockSpec((1, 2, L, 2 * V_DIM), lambda b, p, i: (b, p, 0, 0))],
        out_specs=pl.BlockSpec((1, tq, 2 * V_DIM), lambda b, p, i: (b, i, p)),
        out_shape=jax.ShapeDtypeStruct((B, L, ATTN_WIDTH), BF16),
        compiler_params=_cparams(("parallel", "parallel", "parallel")),
        name="attn",
    )(q, kt, v)


def _final_kernel(x_ref, yt_ref, at_ref, zat_ref, gate_ref, bg_ref, why_ref, wat_ref, wout_ref, o_ref):
    yt = yt_ref[0]
    y = yt.T.astype(BF16)
    u_hy = jnp.dot(y, why_ref[...], preferred_element_type=F32)
    z = zat_ref[0].astype(F32)
    a = (at_ref[0].astype(F32) * (z * jax.nn.sigmoid(z))).astype(BF16)
    u_at = jnp.dot(a, wat_ref[...], preferred_element_type=F32)
    gates = jax.nn.sigmoid(gate_ref[0].astype(F32) + bg_ref[...])
    merged = gates[:, :D_MODEL] * u_hy + gates[:, D_MODEL:] * u_at
    o_ref[0] = x_ref[0] + jnp.dot(merged.astype(BF16), wout_ref[...], preferred_element_type=F32)


def _final(x, yt, at, zat, gate, bg, why, wat, wout, tl):
    B, L, _ = x.shape
    tok = lambda w: pl.BlockSpec((1, tl, w), lambda b, i: (b, i, 0))
    return pl.pallas_call(
        _final_kernel,
        grid=(B, L // tl),
        in_specs=[tok(D_MODEL), pl.BlockSpec((1, HY_WIDTH, tl), lambda b, i: (b, 0, i)),
                  tok(ATTN_WIDTH), tok(ATTN_WIDTH), tok(2 * D_MODEL),
                  _resident(bg.shape), _resident(why.shape), _resident(wat.shape), _resident(wout.shape)],
        out_specs=tok(D_MODEL),
        out_shape=jax.ShapeDtypeStruct((B, L, D_MODEL), F32),
        compiler_params=_cparams(("parallel", "parallel")),
        name="final",
    )(x, yt, at, zat, gate, bg, why, wat, wout)


def _layer(x, g_norm, w_in, b_gate, w_short, b_short, w_f1, b_f1, freq_1, w_f2, b_f2, freq_2, w_f3,
           hy_bias, w_hy_out, g_cq, w_uq, g_ckv, w_ukv, g_qn, g_kn, w_attn_out, w_out):
    B, L, _ = x.shape
    n = 2 * L
    n1 = n // DFT_LANE
    assert B % 2 == 0 and n1 * DFT_LANE == n and n1 >= 2 and (n1 & (n1 - 1)) == 0
    tl = min(512, L)
    ct = 64
    group = min(8, n1)

    o_z, o_cq, o_ckv, o_kr, o_za = 3 * HY_WIDTH, 4 * HY_WIDTH, 4 * HY_WIDTH + Q_LORA, 4 * HY_WIDTH + Q_LORA + KV_LORA, \
        4 * HY_WIDTH + Q_LORA + KV_LORA + QK_ROPE
    del o_z
    wt_hy = w_in[:, :4 * HY_WIDTH].T.astype(BF16)
    w_cq = w_in[:, o_cq:o_ckv].astype(BF16)
    w_ckv = w_in[:, o_ckv:o_kr].astype(BF16)
    w_rope = w_in[:, o_kr:o_za]
    half = QK_ROPE // 2
    w_rope_sw = jnp.concatenate([w_rope[:, half:], w_rope[:, :half]], axis=1)
    zpad = jnp.zeros((D_MODEL, QK_NOPE), F32)
    wt_kr = jnp.concatenate([zpad, w_rope, zpad, w_rope_sw], axis=1).T.astype(BF16)
    w_rest = w_in[:, o_za:].astype(BF16)

    perm = np.concatenate([np.arange(QK_NOPE), QK_NOPE + half + np.arange(half), QK_NOPE + np.arange(half)])
    wq = w_uq.reshape(Q_LORA, MLA_HEADS, QK_DIM).transpose(1, 0, 2)
    wqs = wq[:, :, perm].astype(BF16)
    wq = wq.astype(BF16)
    wkv = w_ukv.reshape(KV_LORA, MLA_HEADS, QK_NOPE + V_DIM).transpose(1, 0, 2)
    wk = jnp.concatenate([wkv[:, :, :QK_NOPE], jnp.zeros((MLA_HEADS, KV_LORA, QK_ROPE), F32)], axis=2)
    wkt = wk.transpose(0, 2, 1).astype(BF16)
    wv_h = wkv[:, :, QK_NOPE:]
    wv_z = jnp.zeros_like(wv_h)
    odd = (np.arange(MLA_HEADS) % 2 == 1).reshape(MLA_HEADS, 1, 1)
    wv = jnp.where(odd, jnp.concatenate([wv_z, wv_h], axis=2), jnp.concatenate([wv_h, wv_z], axis=2)).astype(BF16)
    gq = g_qn.reshape(1, QK_DIM)
    gqs = gq[:, perm]
    gkc = g_kn.reshape(QK_DIM, 1)
    gksc = gkc[perm]

    hyt, cq, ckv, kr, zat, gate = _in_proj(x, g_norm.reshape(1, D_MODEL), wt_hy, w_cq, w_ckv, wt_kr, w_rest,
                                           g_cq.reshape(1, Q_LORA), g_ckv.reshape(1, KV_LORA), tl)

    zfeat, t = _filter_features(L)
    w1 = jnp.concatenate([w_f1, jnp.zeros((LANES - FILTER_EMB, FILTER_HIDDEN), F32)], axis=0)
    w3 = w_f3.reshape(FILTER_HIDDEN, 2, 2, HY_WIDTH)
    w3f = w3[:, :, 0].reshape(FILTER_HIDDEN, 2 * HY_WIDTH).T
    w3b = w3[:, :, 1].reshape(FILTER_HIDDEN, 2 * HY_WIDTH).T
    deltas = np.abs(np.linspace(MIN_DECAY, MAX_DECAY, HY_WIDTH))
    dl = np.tile(deltas, 2).reshape(2 * HY_WIDTH, 1).astype(np.float32)
    col = lambda v: v.reshape(FILTER_HIDDEN, 1)
    zt = np.ascontiguousarray(np.stack([zfeat.T, zfeat[::-1].T]))
    t2 = np.ascontiguousarray(np.stack([t, t[::-1]]).reshape(2, 1, L))
    h2t = _fmlp(jnp.asarray(zt), w1.T, col(b_f1), col(freq_1), w_f2.T, col(b_f2), col(freq_2))
    kt = _filter(h2t, jnp.asarray(t2), w3f, w3b, jnp.asarray(dl), L, 128)
    fcat32 = jnp.asarray(_dft_cat())
    tw = jnp.asarray(_twiddle(n1))
    kspec = _fspec(kt, fcat32, tw, n1, ct, group)

    wsh = w_short.reshape(3, 3, HY_WIDTH).transpose(1, 0, 2)[..., None]
    bsh = b_short.reshape(3, HY_WIDTH, 1)
    yt = _hyena(hyt.reshape(B, 4, HY_WIDTH, L), wsh, bsh, kspec, hy_bias.reshape(2, HY_WIDTH, 1),
                fcat32.astype(BF16), tw, n1, ct, group)

    ctab, stab = _rope_tables(L)
    q, kt_, v = _qkv(cq, ckv, kr, jnp.asarray(ctab), jnp.asarray(stab), jnp.asarray(np.ascontiguousarray(ctab.T)),
                     jnp.asarray(np.ascontiguousarray(stab.T)), wq, wqs, wkt, wv, gq, gqs, gkc, gksc, tl)
    at = _attn(q, kt_, v, min(512, L))

    return _final(x, yt, at, zat, gate, b_gate.reshape(1, 2 * D_MODEL), w_hy_out.astype(BF16),
                  w_attn_out.astype(BF16), w_out.astype(BF16), tl)


def kernel(x, g_norm, w_in, b_gate, w_short, b_short, w_f1, b_f1, freq_1, w_f2, b_f2, freq_2, w_f3, hy_bias, w_hy_out, g_cq, w_uq, g_ckv, w_ukv, g_qn, g_kn, w_attn_out, w_out):
    depth = g_norm.shape[0]
    for l in range(depth):
        x = _layer(x, g_norm[l], w_in[l], b_gate[l], w_short[l], b_short[l], w_f1[l], b_f1[l], freq_1[l], w_f2[l],
                   b_f2[l], freq_2[l], w_f3[l], hy_bias[l], w_hy_out[l], g_cq[l], w_uq[l], g_ckv[l], w_ukv[l],
                   g_qn[l], g_kn[l], w_attn_out[l], w_out[l])
    return x
```

```python
import functools
import math

import numpy as np
import jax
import jax.numpy as jnp
from jax import lax
from jax.experimental import pallas as pl
from jax.experimental.pallas import tpu as pltpu

D_MODEL = 1024
HY_WIDTH = 512
FILTER_EMB = 33
FILTER_HIDDEN = 64
DECAY_TARGET = 1e-2
FAST_DECAY = 0.3
SLOW_DECAY = 1.5
MIN_DECAY = math.log(DECAY_TARGET) / SLOW_DECAY
MAX_DECAY = math.log(DECAY_TARGET) / FAST_DECAY
MLA_HEADS = 8
HEAD_PAIRS = MLA_HEADS // 2
QK_NOPE = 64
QK_ROPE = 32
QK_DIM = QK_NOPE + QK_ROPE
V_DIM = 64
Q_LORA = 384
KV_LORA = 256
ATTN_WIDTH = MLA_HEADS * V_DIM
ROPE_THETA = 10000.0
EPS = 1e-6

LANES = 128
SUBLANES = 8
DFT_LANE = 256
KR_ROWS = 2 * QK_DIM
KV_CHUNK = 1024
VMEM_LIMIT = 56 * 1024 * 1024

F32 = jnp.float32
BF16 = jnp.bfloat16
_NT = (((1,), (1,)), ((), ()))
_NN = (((1,), (0,)), ((), ()))


def _cparams(sem):
    return pltpu.CompilerParams(dimension_semantics=sem, vmem_limit_bytes=VMEM_LIMIT)


def _resident(shape):
    nd = len(shape)
    return pl.BlockSpec(shape, lambda *_: (0,) * nd, pipeline_mode=pl.Buffered(1))


def _add(a, b):
    if a is None:
        return b
    if b is None:
        return a
    return a + b


def _sub(a, b):
    if b is None:
        return a
    if a is None:
        return -b
    return a - b


def _cadd(x, y):
    return (_add(x[0], y[0]), _add(x[1], y[1]))


def _csub(x, y):
    return (_sub(x[0], y[0]), _sub(x[1], y[1]))


def _scale(a, s):
    if a is None or s == 0.0:
        return None
    if s == 1.0:
        return a
    if s == -1.0:
        return -a
    return a * s


def _cmulc(x, w):
    wr, wi = float(np.real(w)), float(np.imag(w))
    if abs(wr) < 1e-15:
        wr = 0.0
    if abs(wi) < 1e-15:
        wi = 0.0
    xr, xi = x
    if wr != 0.0 and abs(abs(wr) - abs(wi)) < 1e-12:
        sr, si, c = np.sign(wr), np.sign(wi), abs(wr)
        return (_scale(_sub(_scale(xr, sr), _scale(xi, si)), c), _scale(_add(_scale(xr, si), _scale(xi, sr)), c))
    re = _sub(_scale(xr, wr), _scale(xi, wi))
    im = _add(_scale(xr, wi), _scale(xi, wr))
    return (re, im)


def _cmul(x, y):
    xr, xi = x
    yr, yi = y
    re = _sub(None if xr is None else xr * yr, None if xi is None else xi * yi)
    im = _add(None if xr is None else xr * yi, None if xi is None else xi * yr)
    return (re, im)


def _fft_dif(xs, sign):
    n = len(xs)
    if n == 1:
        return xs
    half = n // 2
    a = [_cadd(xs[i], xs[i + half]) for i in range(half)]
    b = [_cmulc(_csub(xs[i], xs[i + half]), np.exp(sign * 2j * np.pi * i / n)) for i in range(half)]
    ev = _fft_dif(a, sign)
    od = _fft_dif(b, sign)
    out = [None] * n
    out[0::2] = ev
    out[1::2] = od
    return out


def _dft_mats():
    n = np.arange(DFT_LANE)
    ang = 2.0 * np.pi * np.outer(n, n) / DFT_LANE
    fr, fi = np.cos(ang), -np.sin(ang)
    fwd = np.block([[fr, fi], [-fi, fr]]).astype(np.float32)
    inv = np.block([[fr, -fi], [fi, fr]]).astype(np.float32)
    return fwd, inv


def _twiddle(n1):
    n = n1 * DFT_LANE
    ang = 2.0 * np.pi * np.outer(np.arange(n1), np.arange(DFT_LANE)) / n
    tw = np.stack([np.cos(ang), -np.sin(ang)], axis=1)
    return np.ascontiguousarray(np.broadcast_to(tw[:, :, None, :], (n1, 2, SUBLANES, DFT_LANE))).astype(np.float32)


def _filter_features(L):
    t = np.linspace(0.0, 1.0, L)[:, None]
    bands = (FILTER_EMB - 1) // 2
    f = np.linspace(1e-4, bands - 1, bands)
    ang = (2.0 * np.pi / L) * np.arange(L)[:, None] * f[None, :]
    z = np.concatenate([t, np.cos(ang), -np.sin(ang)], axis=-1)
    zp = np.zeros((L, LANES), np.float32)
    zp[:, :FILTER_EMB] = z
    return zp, t[:, 0].astype(np.float32)


def _rope_tables(L):
    pos = np.arange(L, dtype=np.float64)
    inv_freq = ROPE_THETA ** (-np.arange(0, QK_ROPE, 2, dtype=np.float64) / QK_ROPE)
    ang = pos[:, None] * inv_freq[None, :]
    c, s = np.cos(ang), np.sin(ang)
    ctab = np.concatenate([np.ones((L, QK_NOPE)), c, c], axis=1)
    stab = np.concatenate([np.zeros((L, QK_NOPE)), -s, s], axis=1)
    return ctab.astype(np.float32), stab.astype(np.float32)


def _in_proj_kernel(x_ref, gn_ref, wt_hy_ref, w_cq_ref, w_ckv_ref, w_kr_ref, w_rest_ref, gcq_ref, gckv_ref,
                    hyt_ref, cq_ref, ckv_ref, kr_ref, zat_ref, gate_ref):
    x = x_ref[0]
    ms = jnp.mean(x * x, axis=-1, keepdims=True)
    h = (x * lax.rsqrt(ms + EPS) * gn_ref[...]).astype(BF16)
    hyt_ref[0] = lax.dot_general(wt_hy_ref[...], h, _NT, preferred_element_type=F32).astype(BF16)

    def latent(w_ref, g_ref):
        c = jnp.dot(h, w_ref[...], preferred_element_type=F32)
        r = lax.rsqrt(jnp.mean(c * c, axis=-1, keepdims=True) + EPS)
        return (c * r * g_ref[...]).astype(BF16)

    cq_ref[0] = latent(w_cq_ref, gcq_ref)
    ckv_ref[0] = latent(w_ckv_ref, gckv_ref)
    kr_ref[0] = lax.dot_general(w_kr_ref[...], h, _NT, preferred_element_type=F32)
    rest = jnp.dot(h, w_rest_ref[...], preferred_element_type=F32)
    zat_ref[0] = rest[:, :ATTN_WIDTH].astype(BF16)
    gate_ref[0] = rest[:, ATTN_WIDTH:].astype(BF16)


def _in_proj(x, gn, wt_hy, w_cq, w_ckv, w_kr, w_rest, gcq, gckv, tl):
    B, L, _ = x.shape
    n_rest = w_rest.shape[1]
    return pl.pallas_call(
        _in_proj_kernel,
        grid=(B, L // tl),
        in_specs=[
            pl.BlockSpec((1, tl, D_MODEL), lambda b, i: (b, i, 0)),
            _resident(gn.shape), _resident(wt_hy.shape), _resident(w_cq.shape), _resident(w_ckv.shape),
            _resident(w_kr.shape), _resident(w_rest.shape), _resident(gcq.shape), _resident(gckv.shape),
        ],
        out_specs=[
            pl.BlockSpec((1, 4 * HY_WIDTH, tl), lambda b, i: (b, 0, i)),
            pl.BlockSpec((1, tl, Q_LORA), lambda b, i: (b, i, 0)),
            pl.BlockSpec((1, tl, KV_LORA), lambda b, i: (b, i, 0)),
            pl.BlockSpec((1, KR_ROWS, tl), lambda b, i: (b, 0, i)),
            pl.BlockSpec((1, tl, ATTN_WIDTH), lambda b, i: (b, i, 0)),
            pl.BlockSpec((1, tl, 2 * D_MODEL), lambda b, i: (b, i, 0)),
        ],
        out_shape=[
            jax.ShapeDtypeStruct((B, 4 * HY_WIDTH, L), BF16),
            jax.ShapeDtypeStruct((B, L, Q_LORA), BF16),
            jax.ShapeDtypeStruct((B, L, KV_LORA), BF16),
            jax.ShapeDtypeStruct((B, KR_ROWS, L), F32),
            jax.ShapeDtypeStruct((B, L, ATTN_WIDTH), BF16),
            jax.ShapeDtypeStruct((B, L, n_rest - ATTN_WIDTH), BF16),
        ],
        compiler_params=_cparams(("parallel", "parallel")),
        name="in_proj",
    )(x, gn, wt_hy, w_cq, w_ckv, w_kr, w_rest, gcq, gckv)


def _fmlp_kernel(zt_ref, w1t_ref, b1_ref, f1_ref, w2t_ref, b2_ref, f2_ref, out_ref):
    hp = lax.Precision.HIGHEST
    h = jnp.sin(f1_ref[...] * (jnp.dot(w1t_ref[...], zt_ref[0], precision=hp, preferred_element_type=F32) + b1_ref[...]))
    out_ref[0] = jnp.sin(f2_ref[...] * (jnp.dot(w2t_ref[...], h, precision=hp, preferred_element_type=F32) + b2_ref[...]))


def _fmlp(zt, w1t, b1, f1, w2t, b2, f2):
    _, _, L = zt.shape
    return pl.pallas_call(
        _fmlp_kernel,
        grid=(2,),
        in_specs=[pl.BlockSpec((1, LANES, L), lambda d: (d, 0, 0)), _resident(w1t.shape), _resident(b1.shape),
                  _resident(f1.shape), _resident(w2t.shape), _resident(b2.shape), _resident(f2.shape)],
        out_specs=pl.BlockSpec((1, FILTER_HIDDEN, L), lambda d: (d, 0, 0)),
        out_shape=jax.ShapeDtypeStruct((2, FILTER_HIDDEN, L), F32),
        compiler_params=_cparams(("parallel",)),
        name="fmlp",
    )(zt, w1t, b1, f1, w2t, b2, f2)


def _filter_kernel(h_ref, t_ref, w3f_ref, w3b_ref, dl_ref, out_ref, *, L):
    hp = lax.Precision.HIGHEST
    dl = dl_ref[...]
    for d, w_ref in enumerate((w3f_ref, w3b_ref)):
        k = jnp.dot(w_ref[...], h_ref[d], precision=hp, preferred_element_type=F32)
        out_ref[:, d * L:(d + 1) * L] = k * jnp.exp(-(dl * t_ref[d]))


def _filter(h2t, t2, w3f, w3b, dl, L, rows):
    n_rows = w3f.shape[0]
    row_spec = lambda shape: pl.BlockSpec(shape, lambda i: (i, 0))
    return pl.pallas_call(
        functools.partial(_filter_kernel, L=L),
        grid=(n_rows // rows,),
        in_specs=[_resident(h2t.shape), _resident(t2.shape),
                  row_spec((rows, FILTER_HIDDEN)), row_spec((rows, FILTER_HIDDEN)), row_spec((rows, 1))],
        out_specs=pl.BlockSpec((rows, 2 * L), lambda i: (i, 0)),
        out_shape=jax.ShapeDtypeStruct((n_rows, 2 * L), F32),
        compiler_params=_cparams(("parallel",)),
        name="filt",
    )(h2t, t2, w3f, w3b, dl)


def _stage_a_forward(load, g_scr, tw_ref, n1, rows_total):
    def body(r, carry):
        rs = pl.ds(pl.multiple_of(r * SUBLANES, SUBLANES), SUBLANES)
        for half in range(DFT_LANE // LANES):
            ls = slice(half * LANES, (half + 1) * LANES)
            li = slice(DFT_LANE + half * LANES, DFT_LANE + (half + 1) * LANES)
            xs = [load(i, rs, ls) for i in range(n1)]
            xs = [(None, None) if x is None else x for x in xs]
            X = _fft_dif(xs, -1.0)
            for k1 in range(n1):
                if tw_ref is None:
                    yr, yi = X[k1]
                else:
                    yr, yi = _cmul(X[k1], (tw_ref[k1, 0, :, ls], tw_ref[k1, 1, :, ls]))
                g_scr[k1, rs, ls] = yr
                g_scr[k1, rs, li] = yi
        return carry

    lax.fori_loop(0, rows_total // SUBLANES, body, 0)


def _fspec_kernel(k_ref, wf_ref, tw_ref, out_ref, g_scr, *, n1, rows, group):
    def load(i, rs, ls):
        return (k_ref[rs, i * DFT_LANE + ls.start:i * DFT_LANE + ls.stop], None)

    _stage_a_forward(load, g_scr, tw_ref, n1, rows)
    inv_n = 1.0 / (n1 * DFT_LANE)
    for g0 in range(0, n1, group):
        lhs = g_scr[g0:g0 + group].reshape(group * rows, 2 * DFT_LANE)
        res = jnp.dot(lhs, wf_ref[...], precision=lax.Precision.HIGHEST, preferred_element_type=F32)
        out_ref[0, g0:g0 + group] = res.reshape(group, rows, 2 * DFT_LANE) * inv_n


def _fspec(kt, wf, tw, n1, rows, group):
    n_rows, n = kt.shape
    tiles_per_order = HY_WIDTH // rows
    return pl.pallas_call(
        functools.partial(_fspec_kernel, n1=n1, rows=rows, group=group),
        grid=(n_rows // rows,),
        in_specs=[pl.BlockSpec((rows, n), lambda i: (i, 0)), _resident(wf.shape), _resident(tw.shape)],
        out_specs=pl.BlockSpec((1, n1, rows, 2 * DFT_LANE),
                               lambda i: (i // tiles_per_order, 0, i % tiles_per_order, 0)),
        out_shape=jax.ShapeDtypeStruct((n_rows // HY_WIDTH, n1, HY_WIDTH, 2 * DFT_LANE), F32),
        scratch_shapes=[pltpu.VMEM((n1, rows, 2 * DFT_LANE), F32)],
        compiler_params=_cparams(("parallel",)),
        name="fspec",
    )(kt, wf, tw)


def _hyena_kernel(hy_ref, wsh_ref, bsh_ref, ksp_ref, bias_ref, wf_ref, wi_ref, tw_ref, out_ref,
                  u_scr, x1_scr, x2_scr, g_scr, *, L, n1, ct, group):
    lane = lax.broadcasted_iota(jnp.int32, (ct, LANES), 1)

    def short_conv(g, dst):
        for bi in range(2):
            p = hy_ref[bi, g].astype(F32)
            left = pltpu.roll(p, 1, 1)
            left = jnp.concatenate([jnp.where(lane == 0, 0.0, left[:, :LANES]), left[:, LANES:]], axis=1)
            right = pltpu.roll(p, L - 1, 1)
            right = jnp.concatenate([right[:, :L - LANES], jnp.where(lane == LANES - 1, 0.0, right[:, L - LANES:])],
                                    axis=1)
            conv = wsh_ref[g, 0] * left + wsh_ref[g, 1] * p + wsh_ref[g, 2] * right + bsh_ref[g]
            if g == 2:
                z = hy_ref[bi, 3].astype(F32)
                conv = conv * (z * jax.nn.sigmoid(z))
            dst[bi] = conv

    short_conv(0, u_scr)
    gate_scr = (x1_scr, x2_scr)

    half_blocks = n1 // 2
    wf = wf_ref[...]
    wi = wi_ref[...]
    re, im = slice(0, DFT_LANE), slice(DFT_LANE, 2 * DFT_LANE)

    def mxu_dft(x, w):
        res = jnp.dot(x.reshape(group * ct, 2 * DFT_LANE).astype(BF16), w, preferred_element_type=F32)
        res = res.reshape(group, ct, 2 * DFT_LANE)
        return res[..., re], res[..., im]

    for order in range(2):
        def load(i, rs, ls):
            if i >= half_blocks:
                return None
            cs = slice(i * DFT_LANE + ls.start, i * DFT_LANE + ls.stop)
            return (u_scr[0, rs, cs], u_scr[1, rs, cs])

        _stage_a_forward(load, g_scr, None, n1, ct)

        def twiddled(gs):
            g = g_scr[gs]
            gr, gi = g[..., re], g[..., im]
            twr = tw_ref[gs, 0, 0:1, :]
            twi = tw_ref[gs, 1, 0:1, :]
            return jnp.concatenate([gr * twr - gi * twi, gr * twi + gi * twr], axis=-1)

        def untwiddle(gs, q):
            qr, qi = q
            twr = tw_ref[gs, 0, 0:1, :]
            twi = tw_ref[gs, 1, 0:1, :]
            g_scr[gs, :, re] = qr * twr + qi * twi
            g_scr[gs, :, im] = qi * twr - qr * twi

        groups = [slice(g0, g0 + group) for g0 in range(0, n1, group)]
        spectra = [mxu_dft(twiddled(gs), wf) for gs in groups]
        short_conv(order + 1, gate_scr[order])
        prev = None
        for gs, (hr, hi) in zip(groups, spectra):
            kr, ki = ksp_ref[order, gs, :, re], ksp_ref[order, gs, :, im]
            q = mxu_dft(jnp.concatenate([hr * kr - hi * ki, hr * ki + hi * kr], axis=-1), wi)
            if prev is not None:
                untwiddle(*prev)
            prev = (gs, q)
        untwiddle(*prev)

        def body(r, carry, order=order):
            rs = pl.ds(pl.multiple_of(r * SUBLANES, SUBLANES), SUBLANES)
            bias = bias_ref[order, rs, :]
            for half in range(DFT_LANE // LANES):
                ls = slice(half * LANES, (half + 1) * LANES)
                li = slice(DFT_LANE + half * LANES, DFT_LANE + (half + 1) * LANES)
                q = [(g_scr[k1, rs, ls], g_scr[k1, rs, li]) for k1 in range(n1)]
                ev = _fft_dif(q[0::2], 1.0)
                od = _fft_dif(q[1::2], 1.0)
                for i in range(half_blocks):
                    y = _cadd(ev[i], _cmulc(od[i], np.exp(2j * np.pi * i / n1)))
                    cs = slice(i * DFT_LANE + ls.start, i * DFT_LANE + ls.stop)
                    for bi in range(2):
                        u = u_scr[bi, rs, cs]
                        conv = y[bi] + bias * u
                        if order == 0:
                            u_scr[bi, rs, cs] = x1_scr[bi, rs, cs] * conv
                        else:
                            out_ref[bi, rs, cs] = x2_scr[bi, rs, cs] * conv
            return carry

        lax.fori_loop(0, ct // SUBLANES, body, 0)


def _hyena(hyt4, wsh, bsh, kspec, bias, wf, wi, tw, n1, ct, group):
    B, _, C, L = hyt4.shape
    return pl.pallas_call(
        functools.partial(_hyena_kernel, L=L, n1=n1, ct=ct, group=group),
        grid=(C // ct, B // 2),
        in_specs=[
            pl.BlockSpec((2, 4, ct, L), lambda c, p: (p, 0, c, 0)),
            pl.BlockSpec((3, 3, ct, 1), lambda c, p: (0, 0, c, 0)),
            pl.BlockSpec((3, ct, 1), lambda c, p: (0, c, 0)),
            pl.BlockSpec((2, n1, ct, 2 * DFT_LANE), lambda c, p: (0, 0, c, 0), pipeline_mode=pl.Buffered(1)),
            pl.BlockSpec((2, ct, 1), lambda c, p: (0, c, 0)),
            _resident(wf.shape), _resident(wi.shape), _resident(tw.shape),
        ],
        out_specs=pl.BlockSpec((2, ct, L), lambda c, p: (p, c, 0)),
        out_shape=jax.ShapeDtypeStruct((B, C, L), F32),
        scratch_shapes=[pltpu.VMEM((2, ct, L), F32), pltpu.VMEM((2, ct, L), F32), pltpu.VMEM((2, ct, L), F32),
                        pltpu.VMEM((n1, ct, 2 * DFT_LANE), F32)],
        compiler_params=_cparams(("parallel", "parallel")),
        name="hyena",
    )(hyt4, wsh, bsh, kspec, bias, wf, wi, tw)


def _qkv_kernel(cq_ref, ckv_ref, kr_ref, ct_ref, st_ref, ctt_ref, stt_ref, wq_ref, wqs_ref, wkt_ref, wv_ref,
                gq_ref, gqs_ref, gkc_ref, gksc_ref, ones_ref, q_ref, kt_ref, v_ref):
    cq = cq_ref[0]
    ckv = ckv_ref[0]
    kr_a = kr_ref[0, :QK_DIM, :]
    kr_s = kr_ref[0, QK_DIM:, :]
    scale = math.log2(math.e) / math.sqrt(QK_DIM)
    qc, qs = gq_ref[...] * ct_ref[...], gqs_ref[...] * st_ref[...]
    kc, ks = gkc_ref[...] * ctt_ref[...], gksc_ref[...] * stt_ref[...]
    qa_all = jnp.dot(cq, wq_ref[...], preferred_element_type=F32)
    qw_all = jnp.dot(cq, wqs_ref[...], preferred_element_type=F32)
    ka_all = lax.dot_general(wkt_ref[...], ckv, _NT, preferred_element_type=F32)
    v_all = jnp.dot(ckv, wv_ref[...], preferred_element_type=F32)
    for h in range(MLA_HEADS):
        hs = slice(h * LANES, (h + 1) * LANES)
        qa, qw = qa_all[:, hs], qw_all[:, hs]
        r = lax.rsqrt(jnp.sum(qa * qa, axis=-1, keepdims=True) * (1.0 / QK_DIM) + EPS) * scale
        q_ref[0, h] = ((qa * qc + qw * qs) * r)[:, :QK_DIM].astype(BF16)
        ka = ka_all[h * QK_DIM:(h + 1) * QK_DIM] + kr_a
        rk = lax.rsqrt(jnp.mean(ka * ka, axis=0, keepdims=True) + EPS)
        kt_ref[0, h] = ((ka * kc + kr_s * ks) * rk).astype(BF16)
        v_ref[0, h] = (v_all[:, hs] + ones_ref[h % 2]).astype(BF16)


def _qkv(cq, ckv, kr, ctab, stab, ctab_t, stab_t, wq, wqs, wkt, wv, gq, gqs, gkc, gksc, tl):
    ones = np.zeros((2, 1, 2 * V_DIM), np.float32)
    ones[0, 0, V_DIM] = 1.0
    ones[1, 0, 0] = 1.0
    ones = jnp.asarray(ones)
    B, L, _ = cq.shape
    tok = lambda w: pl.BlockSpec((1, tl, w), lambda b, i: (b, i, 0))
    tab = pl.BlockSpec((tl, LANES), lambda b, i: (i, 0))
    tab_t = pl.BlockSpec((QK_DIM, tl), lambda b, i: (0, i))
    return pl.pallas_call(
        _qkv_kernel,
        grid=(B, L // tl),
        in_specs=[tok(Q_LORA), tok(KV_LORA), pl.BlockSpec((1, KR_ROWS, tl), lambda b, i: (b, 0, i)),
                  tab, tab, tab_t, tab_t,
                  _resident(wq.shape), _resident(wqs.shape), _resident(wkt.shape), _resident(wv.shape),
                  _resident(gq.shape), _resident(gqs.shape), _resident(gkc.shape), _resident(gksc.shape),
                  _resident(ones.shape)],
        out_specs=[pl.BlockSpec((1, MLA_HEADS, tl, QK_DIM), lambda b, i: (b, 0, i, 0)),
                   pl.BlockSpec((1, MLA_HEADS, QK_DIM, tl), lambda b, i: (b, 0, 0, i)),
                   pl.BlockSpec((1, MLA_HEADS, tl, 2 * V_DIM), lambda b, i: (b, 0, i, 0))],
        out_shape=[jax.ShapeDtypeStruct((B, MLA_HEADS, L, QK_DIM), BF16),
                   jax.ShapeDtypeStruct((B, MLA_HEADS, QK_DIM, L), BF16),
                   jax.ShapeDtypeStruct((B, MLA_HEADS, L, 2 * V_DIM), BF16)],
        compiler_params=_cparams(("parallel", "parallel")),
        name="qkv",
    )(cq, ckv, kr, ctab, stab, ctab_t, stab_t, wq, wqs, wkt, wv, gq, gqs, gkc, gksc, ones)


def _attn_kernel(q_ref, kt_ref, v_ref, o_ref, *, L):
    def scores(j, c0):
        return jnp.dot(q_ref[0, j], kt_ref[0, j, :, c0:c0 + KV_CHUNK], preferred_element_type=F32)

    m = [None, None]
    acc = [None, None]
    nxt = [scores(j, 0) for j in range(2)]
    for c0 in range(0, L, KV_CHUNK):
        cur = nxt
        if c0 + KV_CHUNK < L:
            nxt = [scores(j, c0 + KV_CHUNK) for j in range(2)]
        for j in range(2):
            s = cur[j]
            mc = jnp.max(s, axis=-1, keepdims=True)
            m_new = mc if m[j] is None else jnp.maximum(m[j], mc)
            p = jnp.exp2(s - m_new).astype(BF16)
            pv = jnp.dot(p, v_ref[0, j, c0:c0 + KV_CHUNK, :], preferred_element_type=F32)
            acc[j] = pv if m[j] is None else jnp.exp2(m[j] - m_new) * acc[j] + pv
            m[j] = m_new
    outs = []
    for j in range(2):
        ones_lane = (1 - j) * V_DIM
        outs.append(acc[j] / acc[j][:, ones_lane:ones_lane + 1])
    lane = lax.broadcasted_iota(jnp.int32, outs[0].shape, 1)
    o_ref[0] = jnp.where(lane < V_DIM, outs[0], outs[1]).astype(BF16)


def _attn(q, kt, v, tq):
    B, _, L, _ = q.shape
    return pl.pallas_call(
        functools.partial(_attn_kernel, L=L),
        grid=(B, HEAD_PAIRS, L // tq),
        in_specs=[pl.BlockSpec((1, 2, tq, QK_DIM), lambda b, p, i: (b, p, i, 0)),
                  pl.BlockSpec((1, 2, QK_DIM, L), lambda b, p, i: (b, p, 0, 0)),
                  pl.BlockSpec((1, 2, L, 2 * V_DIM), lambda b, p, i: (b, p, 0, 0))],
        out_specs=pl.BlockSpec((1, tq, 2 * V_DIM), lambda b, p, i: (b, i, p)),
        out_shape=jax.ShapeDtypeStruct((B, L, ATTN_WIDTH), BF16),
        compiler_params=_cparams(("parallel", "parallel", "parallel")),
        name="attn",
    )(q, kt, v)


def _final_kernel(x_ref, yt_ref, at_ref, zat_ref, gate_ref, bg_ref, why_ref, wat_ref, wout_ref, o_ref):
    yt = yt_ref[0]
    y = yt.T.astype(BF16)
    u_hy = jnp.dot(y, why_ref[...], preferred_element_type=F32)
    z = zat_ref[0].astype(F32)
    a = (at_ref[0].astype(F32) * (z * jax.nn.sigmoid(z))).astype(BF16)
    u_at = jnp.dot(a, wat_ref[...], preferred_element_type=F32)
    gates = jax.nn.sigmoid(gate_ref[0].astype(F32) + bg_ref[...])
    merged = gates[:, :D_MODEL] * u_hy + gates[:, D_MODEL:] * u_at
    o_ref[0] = x_ref[0] + jnp.dot(merged.astype(BF16), wout_ref[...], preferred_element_type=F32)


def _final(x, yt, at, zat, gate, bg, why, wat, wout, tl):
    B, L, _ = x.shape
    tok = lambda w: pl.BlockSpec((1, tl, w), lambda b, i: (b, i, 0))
    return pl.pallas_call(
        _final_kernel,
        grid=(B, L // tl),
        in_specs=[tok(D_MODEL), pl.BlockSpec((1, HY_WIDTH, tl), lambda b, i: (b, 0, i)),
                  tok(ATTN_WIDTH), tok(ATTN_WIDTH), tok(2 * D_MODEL),
                  _resident(bg.shape), _resident(why.shape), _resident(wat.shape), _resident(wout.shape)],
        out_specs=tok(D_MODEL),
        out_shape=jax.ShapeDtypeStruct((B, L, D_MODEL), F32),
        compiler_params=_cparams(("parallel", "parallel")),
        name="final",
    )(x, yt, at, zat, gate, bg, why, wat, wout)


def _layer(x, g_norm, w_in, b_gate, w_short, b_short, w_f1, b_f1, freq_1, w_f2, b_f2, freq_2, w_f3,
           hy_bias, w_hy_out, g_cq, w_uq, g_ckv, w_ukv, g_qn, g_kn, w_attn_out, w_out):
    B, L, _ = x.shape
    n = 2 * L
    n1 = n // DFT_LANE
    assert B % 2 == 0 and n1 * DFT_LANE == n and n1 >= 2 and (n1 & (n1 - 1)) == 0
    tl = min(512, L)
    ct = 64
    group = min(8, n1)

    o_z, o_cq, o_ckv, o_kr, o_za = 3 * HY_WIDTH, 4 * HY_WIDTH, 4 * HY_WIDTH + Q_LORA, 4 * HY_WIDTH + Q_LORA + KV_LORA, \
        4 * HY_WIDTH + Q_LORA + KV_LORA + QK_ROPE
    del o_z
    wt_hy = w_in[:, :4 * HY_WIDTH].T.astype(BF16)
    w_cq = w_in[:, o_cq:o_ckv].astype(BF16)
    w_ckv = w_in[:, o_ckv:o_kr].astype(BF16)
    w_rope = w_in[:, o_kr:o_za]
    half = QK_ROPE // 2
    w_rope_sw = jnp.concatenate([w_rope[:, half:], w_rope[:, :half]], axis=1)
    zpad = jnp.zeros((D_MODEL, QK_NOPE), F32)
    wt_kr = jnp.concatenate([zpad, w_rope, zpad, w_rope_sw], axis=1).T.astype(BF16)
    w_rest = w_in[:, o_za:].astype(BF16)

    perm = np.concatenate([np.arange(QK_NOPE), QK_NOPE + half + np.arange(half), QK_NOPE + np.arange(half)])
    wq3 = w_uq.reshape(Q_LORA, MLA_HEADS, QK_DIM)
    pad_heads = lambda w: jnp.pad(w, ((0, 0), (0, 0), (0, LANES - QK_DIM))).reshape(Q_LORA, MLA_HEADS * LANES)
    wq = pad_heads(wq3).astype(BF16)
    wqs = pad_heads(wq3[:, :, perm]).astype(BF16)
    wkv = w_ukv.reshape(KV_LORA, MLA_HEADS, QK_NOPE + V_DIM).transpose(1, 0, 2)
    wk = jnp.concatenate([wkv[:, :, :QK_NOPE], jnp.zeros((MLA_HEADS, KV_LORA, QK_ROPE), F32)], axis=2)
    wkt = wk.transpose(0, 2, 1).reshape(MLA_HEADS * QK_DIM, KV_LORA).astype(BF16)
    wv_h = wkv[:, :, QK_NOPE:]
    wv_z = jnp.zeros_like(wv_h)
    odd = (np.arange(MLA_HEADS) % 2 == 1).reshape(MLA_HEADS, 1, 1)
    wv = jnp.where(odd, jnp.concatenate([wv_z, wv_h], axis=2), jnp.concatenate([wv_h, wv_z], axis=2))
    wv = wv.transpose(1, 0, 2).reshape(KV_LORA, MLA_HEADS * LANES).astype(BF16)
    pad_lanes = lambda g: jnp.pad(g, ((0, 0), (0, LANES - QK_DIM)))
    gq = g_qn.reshape(1, QK_DIM)
    gqs = pad_lanes(gq[:, perm])
    gq = pad_lanes(gq)
    gkc = g_kn.reshape(QK_DIM, 1)
    gksc = gkc[perm]

    hyt, cq, ckv, kr, zat, gate = _in_proj(x, g_norm.reshape(1, D_MODEL), wt_hy, w_cq, w_ckv, wt_kr, w_rest,
                                           g_cq.reshape(1, Q_LORA), g_ckv.reshape(1, KV_LORA), tl)

    zfeat, t = _filter_features(L)
    w1 = jnp.concatenate([w_f1, jnp.zeros((LANES - FILTER_EMB, FILTER_HIDDEN), F32)], axis=0)
    w3 = w_f3.reshape(FILTER_HIDDEN, 2, 2, HY_WIDTH)
    w3f = w3[:, :, 0].reshape(FILTER_HIDDEN, 2 * HY_WIDTH).T
    w3b = w3[:, :, 1].reshape(FILTER_HIDDEN, 2 * HY_WIDTH).T
    deltas = np.abs(np.linspace(MIN_DECAY, MAX_DECAY, HY_WIDTH))
    dl = np.tile(deltas, 2).reshape(2 * HY_WIDTH, 1).astype(np.float32)
    col = lambda v: v.reshape(FILTER_HIDDEN, 1)
    zt = np.ascontiguousarray(np.stack([zfeat.T, zfeat[::-1].T]))
    t2 = np.ascontiguousarray(np.stack([t, t[::-1]]).reshape(2, 1, L))
    h2t = _fmlp(jnp.asarray(zt), w1.T, col(b_f1), col(freq_1), w_f2.T, col(b_f2), col(freq_2))
    kt = _filter(h2t, jnp.asarray(t2), w3f, w3b, jnp.asarray(dl), L, 128)
    wf32, wi32 = (jnp.asarray(w) for w in _dft_mats())
    tw = jnp.asarray(_twiddle(n1))
    kspec = _fspec(kt, wf32, tw, n1, ct, group)

    wsh = w_short.reshape(3, 3, HY_WIDTH).transpose(1, 0, 2)[..., None]
    bsh = b_short.reshape(3, HY_WIDTH, 1)
    yt = _hyena(hyt.reshape(B, 4, HY_WIDTH, L), wsh, bsh, kspec, hy_bias.reshape(2, HY_WIDTH, 1),
                wf32.astype(BF16), wi32.astype(BF16), tw, n1, ct, group)

    ctab, stab = _rope_tables(L)
    pad_tab = lambda tab: np.pad(tab, ((0, 0), (0, LANES - QK_DIM)))
    q, kt_, v = _qkv(cq, ckv, kr, jnp.asarray(pad_tab(ctab)), jnp.asarray(pad_tab(stab)),
                     jnp.asarray(np.ascontiguousarray(ctab.T)),
                     jnp.asarray(np.ascontiguousarray(stab.T)), wq, wqs, wkt, wv, gq, gqs, gkc, gksc, tl)
    at = _attn(q, kt_, v, min(512, L))

    return _final(x, yt, at, zat, gate, b_gate.reshape(1, 2 * D_MODEL), w_hy_out.astype(BF16),
                  w_attn_out.astype(BF16), w_out.astype(BF16), tl)


def kernel(x, g_norm, w_in, b_gate, w_short, b_short, w_f1, b_f1, freq_1, w_f2, b_f2, freq_2, w_f3, hy_bias, w_hy_out, g_cq, w_uq, g_ckv, w_ukv, g_qn, g_kn, w_attn_out, w_out):
    depth = g_norm.shape[0]
    for l in range(depth):
        x = _layer(x, g_norm[l], w_in[l], b_gate[l], w_short[l], b_short[l], w_f1[l], b_f1[l], freq_1[l], w_f2[l],
                   b_f2[l], freq_2[l], w_f3[l], hy_bias[l], w_hy_out[l], g_cq[l], w_uq[l], g_ckv[l], w_ukv[l],
                   g_qn[l], g_kn[l], w_attn_out[l], w_out[l])
    return x
```

```python
import functools
import math

import numpy as np
import jax
import jax.numpy as jnp
from jax import lax
from jax.experimental import pallas as pl
from jax.experimental.pallas import tpu as pltpu

D_MODEL = 1024
HY_WIDTH = 512
FILTER_EMB = 33
FILTER_HIDDEN = 64
DECAY_TARGET = 1e-2
FAST_DECAY = 0.3
SLOW_DECAY = 1.5
MIN_DECAY = math.log(DECAY_TARGET) / SLOW_DECAY
MAX_DECAY = math.log(DECAY_TARGET) / FAST_DECAY
MLA_HEADS = 8
HEAD_PAIRS = MLA_HEADS // 2
QK_NOPE = 64
QK_ROPE = 32
QK_DIM = QK_NOPE + QK_ROPE
V_DIM = 64
Q_LORA = 384
KV_LORA = 256
ATTN_WIDTH = MLA_HEADS * V_DIM
ROPE_THETA = 10000.0
EPS = 1e-6

LANES = 128
SUBLANES = 8
DFT_LANE = 256
KR_ROWS = 2 * QK_DIM
KV_CHUNK = 1024
VMEM_LIMIT = 56 * 1024 * 1024

F32 = jnp.float32
BF16 = jnp.bfloat16
_NT = (((1,), (1,)), ((), ()))
_NN = (((1,), (0,)), ((), ()))


def _cparams(sem):
    return pltpu.CompilerParams(dimension_semantics=sem, vmem_limit_bytes=VMEM_LIMIT)


def _resident(shape):
    nd = len(shape)
    return pl.BlockSpec(shape, lambda *_: (0,) * nd, pipeline_mode=pl.Buffered(1))


def _add(a, b):
    if a is None:
        return b
    if b is None:
        return a
    return a + b


def _sub(a, b):
    if b is None:
        return a
    if a is None:
        return -b
    return a - b


def _cadd(x, y):
    return (_add(x[0], y[0]), _add(x[1], y[1]))


def _csub(x, y):
    return (_sub(x[0], y[0]), _sub(x[1], y[1]))


def _scale(a, s):
    if a is None or s == 0.0:
        return None
    if s == 1.0:
        return a
    if s == -1.0:
        return -a
    return a * s


def _cmulc(x, w):
    wr, wi = float(np.real(w)), float(np.imag(w))
    if abs(wr) < 1e-15:
        wr = 0.0
    if abs(wi) < 1e-15:
        wi = 0.0
    xr, xi = x
    if wr != 0.0 and abs(abs(wr) - abs(wi)) < 1e-12:
        sr, si, c = np.sign(wr), np.sign(wi), abs(wr)
        return (_scale(_sub(_scale(xr, sr), _scale(xi, si)), c), _scale(_add(_scale(xr, si), _scale(xi, sr)), c))
    re = _sub(_scale(xr, wr), _scale(xi, wi))
    im = _add(_scale(xr, wi), _scale(xi, wr))
    return (re, im)


def _cmul(x, y):
    xr, xi = x
    yr, yi = y
    re = _sub(None if xr is None else xr * yr, None if xi is None else xi * yi)
    im = _add(None if xr is None else xr * yi, None if xi is None else xi * yr)
    return (re, im)


def _fft_dif(xs, sign):
    n = len(xs)
    if n == 1:
        return xs
    half = n // 2
    a = [_cadd(xs[i], xs[i + half]) for i in range(half)]
    b = [_cmulc(_csub(xs[i], xs[i + half]), np.exp(sign * 2j * np.pi * i / n)) for i in range(half)]
    ev = _fft_dif(a, sign)
    od = _fft_dif(b, sign)
    out = [None] * n
    out[0::2] = ev
    out[1::2] = od
    return out


def _dft_mats():
    n = np.arange(DFT_LANE)
    ang = 2.0 * np.pi * np.outer(n, n) / DFT_LANE
    fr, fi = np.cos(ang), -np.sin(ang)
    fwd = np.block([[fr, fi], [-fi, fr]]).astype(np.float32)
    inv = np.block([[fr, -fi], [fi, fr]]).astype(np.float32)
    return fwd, inv


def _twiddle(n1):
    n = n1 * DFT_LANE
    ang = 2.0 * np.pi * np.outer(np.arange(n1), np.arange(DFT_LANE)) / n
    tw = np.stack([np.cos(ang), -np.sin(ang)], axis=1)
    return np.ascontiguousarray(np.broadcast_to(tw[:, :, None, :], (n1, 2, SUBLANES, DFT_LANE))).astype(np.float32)


def _filter_features(L):
    t = np.linspace(0.0, 1.0, L)[:, None]
    bands = (FILTER_EMB - 1) // 2
    f = np.linspace(1e-4, bands - 1, bands)
    ang = (2.0 * np.pi / L) * np.arange(L)[:, None] * f[None, :]
    z = np.concatenate([t, np.cos(ang), -np.sin(ang)], axis=-1)
    zp = np.zeros((L, LANES), np.float32)
    zp[:, :FILTER_EMB] = z
    return zp, t[:, 0].astype(np.float32)


def _rope_tables(L):
    pos = np.arange(L, dtype=np.float64)
    inv_freq = ROPE_THETA ** (-np.arange(0, QK_ROPE, 2, dtype=np.float64) / QK_ROPE)
    ang = pos[:, None] * inv_freq[None, :]
    c, s = np.cos(ang), np.sin(ang)
    ctab = np.concatenate([np.ones((L, QK_NOPE)), c, c], axis=1)
    stab = np.concatenate([np.zeros((L, QK_NOPE)), -s, s], axis=1)
    return ctab.astype(np.float32), stab.astype(np.float32)


def _in_proj_kernel(x_ref, gn_ref, wt_hy_ref, w_cq_ref, w_ckv_ref, w_kr_ref, w_rest_ref, gcq_ref, gckv_ref,
                    ct_ref, st_ref, ctt_ref, stt_ref, wq_ref, wqs_ref, wkt_ref, wv_ref,
                    gq_ref, gqs_ref, gkc_ref, gksc_ref, ones_ref,
                    hyt_ref, zat_ref, gate_ref, q_ref, kt_ref, v_ref):
    x = x_ref[0]
    ms = jnp.mean(x * x, axis=-1, keepdims=True)
    h = (x * lax.rsqrt(ms + EPS) * gn_ref[...]).astype(BF16)
    hyt_ref[0] = lax.dot_general(wt_hy_ref[...], h, _NT, preferred_element_type=F32).astype(BF16)

    def latent(w_ref, g_ref):
        c = jnp.dot(h, w_ref[...], preferred_element_type=F32)
        r = lax.rsqrt(jnp.mean(c * c, axis=-1, keepdims=True) + EPS)
        return (c * r * g_ref[...]).astype(BF16)

    cq = latent(w_cq_ref, gcq_ref)
    ckv = latent(w_ckv_ref, gckv_ref)
    kr = lax.dot_general(w_kr_ref[...], h, _NT, preferred_element_type=F32)
    rest = jnp.dot(h, w_rest_ref[...], preferred_element_type=F32)
    zat_ref[0] = rest[:, :ATTN_WIDTH].astype(BF16)
    gate_ref[0] = rest[:, ATTN_WIDTH:].astype(BF16)

    kr_a = kr[:QK_DIM]
    kr_s = kr[QK_DIM:]
    scale = math.log2(math.e) / math.sqrt(QK_DIM)
    qc, qs = gq_ref[...] * ct_ref[...], gqs_ref[...] * st_ref[...]
    kc, ks = gkc_ref[...] * ctt_ref[...], gksc_ref[...] * stt_ref[...]
    qa_all = jnp.dot(cq, wq_ref[...], preferred_element_type=F32)
    qw_all = jnp.dot(cq, wqs_ref[...], preferred_element_type=F32)
    ka_all = lax.dot_general(wkt_ref[...], ckv, _NT, preferred_element_type=F32)
    v_all = jnp.dot(ckv, wv_ref[...], preferred_element_type=F32)
    for hd in range(MLA_HEADS):
        hs = slice(hd * LANES, (hd + 1) * LANES)
        qa, qw = qa_all[:, hs], qw_all[:, hs]
        r = lax.rsqrt(jnp.sum(qa * qa, axis=-1, keepdims=True) * (1.0 / QK_DIM) + EPS) * scale
        q_ref[0, hd] = ((qa * qc + qw * qs) * r)[:, :QK_DIM].astype(BF16)
        ka = ka_all[hd * QK_DIM:(hd + 1) * QK_DIM] + kr_a
        rk = lax.rsqrt(jnp.mean(ka * ka, axis=0, keepdims=True) + EPS)
        kt_ref[0, hd] = ((ka * kc + kr_s * ks) * rk).astype(BF16)
        v_ref[0, hd] = (v_all[:, hs] + ones_ref[hd % 2]).astype(BF16)


def _in_proj(x, gn, wt_hy, w_cq, w_ckv, w_kr, w_rest, gcq, gckv, ctab, stab, ctab_t, stab_t,
             wq, wqs, wkt, wv, gq, gqs, gkc, gksc, tl):
    ones = np.zeros((2, 1, 2 * V_DIM), np.float32)
    ones[0, 0, V_DIM] = 1.0
    ones[1, 0, 0] = 1.0
    ones = jnp.asarray(ones)
    B, L, _ = x.shape
    n_rest = w_rest.shape[1]
    tab = pl.BlockSpec((tl, LANES), lambda b, i: (i, 0))
    tab_t = pl.BlockSpec((QK_DIM, tl), lambda b, i: (0, i))
    consts = (gn, wt_hy, w_cq, w_ckv, w_kr, w_rest, gcq, gckv)
    qkv_consts = (wq, wqs, wkt, wv, gq, gqs, gkc, gksc, ones)
    return pl.pallas_call(
        _in_proj_kernel,
        grid=(B, L // tl),
        in_specs=[pl.BlockSpec((1, tl, D_MODEL), lambda b, i: (b, i, 0))]
        + [_resident(a.shape) for a in consts] + [tab, tab, tab_t, tab_t] + [_resident(a.shape) for a in qkv_consts],
        out_specs=[
            pl.BlockSpec((1, 4 * HY_WIDTH, tl), lambda b, i: (b, 0, i)),
            pl.BlockSpec((1, tl, ATTN_WIDTH), lambda b, i: (b, i, 0)),
            pl.BlockSpec((1, tl, 2 * D_MODEL), lambda b, i: (b, i, 0)),
            pl.BlockSpec((1, MLA_HEADS, tl, QK_DIM), lambda b, i: (b, 0, i, 0)),
            pl.BlockSpec((1, MLA_HEADS, QK_DIM, tl), lambda b, i: (b, 0, 0, i)),
            pl.BlockSpec((1, MLA_HEADS, tl, 2 * V_DIM), lambda b, i: (b, 0, i, 0)),
        ],
        out_shape=[
            jax.ShapeDtypeStruct((B, 4 * HY_WIDTH, L), BF16),
            jax.ShapeDtypeStruct((B, L, ATTN_WIDTH), BF16),
            jax.ShapeDtypeStruct((B, L, n_rest - ATTN_WIDTH), BF16),
            jax.ShapeDtypeStruct((B, MLA_HEADS, L, QK_DIM), BF16),
            jax.ShapeDtypeStruct((B, MLA_HEADS, QK_DIM, L), BF16),
            jax.ShapeDtypeStruct((B, MLA_HEADS, L, 2 * V_DIM), BF16),
        ],
        compiler_params=_cparams(("parallel", "parallel")),
        name="in_proj",
    )(x, *consts, ctab, stab, ctab_t, stab_t, *qkv_consts)


def _fmlp_kernel(zt_ref, w1t_ref, b1_ref, f1_ref, w2t_ref, b2_ref, f2_ref, out_ref):
    hp = lax.Precision.HIGHEST
    h = jnp.sin(f1_ref[...] * (jnp.dot(w1t_ref[...], zt_ref[0], precision=hp, preferred_element_type=F32) + b1_ref[...]))
    out_ref[0] = jnp.sin(f2_ref[...] * (jnp.dot(w2t_ref[...], h, precision=hp, preferred_element_type=F32) + b2_ref[...]))


def _fmlp(zt, w1t, b1, f1, w2t, b2, f2):
    _, _, L = zt.shape
    return pl.pallas_call(
        _fmlp_kernel,
        grid=(2,),
        in_specs=[pl.BlockSpec((1, LANES, L), lambda d: (d, 0, 0)), _resident(w1t.shape), _resident(b1.shape),
                  _resident(f1.shape), _resident(w2t.shape), _resident(b2.shape), _resident(f2.shape)],
        out_specs=pl.BlockSpec((1, FILTER_HIDDEN, L), lambda d: (d, 0, 0)),
        out_shape=jax.ShapeDtypeStruct((2, FILTER_HIDDEN, L), F32),
        compiler_params=_cparams(("parallel",)),
        name="fmlp",
    )(zt, w1t, b1, f1, w2t, b2, f2)


def _filter_kernel(h_ref, t_ref, w3f_ref, w3b_ref, dl_ref, out_ref, *, L):
    hp = lax.Precision.HIGHEST
    dl = dl_ref[...]
    for d, w_ref in enumerate((w3f_ref, w3b_ref)):
        k = jnp.dot(w_ref[...], h_ref[d], precision=hp, preferred_element_type=F32)
        out_ref[:, d * L:(d + 1) * L] = k * jnp.exp(-(dl * t_ref[d]))


def _filter(h2t, t2, w3f, w3b, dl, L, rows):
    n_rows = w3f.shape[0]
    row_spec = lambda shape: pl.BlockSpec(shape, lambda i: (i, 0))
    return pl.pallas_call(
        functools.partial(_filter_kernel, L=L),
        grid=(n_rows // rows,),
        in_specs=[_resident(h2t.shape), _resident(t2.shape),
                  row_spec((rows, FILTER_HIDDEN)), row_spec((rows, FILTER_HIDDEN)), row_spec((rows, 1))],
        out_specs=pl.BlockSpec((rows, 2 * L), lambda i: (i, 0)),
        out_shape=jax.ShapeDtypeStruct((n_rows, 2 * L), F32),
        compiler_params=_cparams(("parallel",)),
        name="filt",
    )(h2t, t2, w3f, w3b, dl)


def _stage_a_forward(load, g_scr, tw_ref, n1, rows_total):
    def body(r, carry):
        rs = pl.ds(pl.multiple_of(r * SUBLANES, SUBLANES), SUBLANES)
        for half in range(DFT_LANE // LANES):
            ls = slice(half * LANES, (half + 1) * LANES)
            li = slice(DFT_LANE + half * LANES, DFT_LANE + (half + 1) * LANES)
            xs = [load(i, rs, ls) for i in range(n1)]
            xs = [(None, None) if x is None else x for x in xs]
            X = _fft_dif(xs, -1.0)
            for k1 in range(n1):
                if tw_ref is None:
                    yr, yi = X[k1]
                else:
                    yr, yi = _cmul(X[k1], (tw_ref[k1, 0, :, ls], tw_ref[k1, 1, :, ls]))
                g_scr[k1, rs, ls] = yr
                g_scr[k1, rs, li] = yi
        return carry

    lax.fori_loop(0, rows_total // SUBLANES, body, 0)


def _fspec_kernel(k_ref, wf_ref, tw_ref, out_ref, g_scr, *, n1, rows, group):
    def load(i, rs, ls):
        return (k_ref[rs, i * DFT_LANE + ls.start:i * DFT_LANE + ls.stop], None)

    _stage_a_forward(load, g_scr, tw_ref, n1, rows)
    inv_n = 1.0 / (n1 * DFT_LANE)
    w = wf_ref[...]
    w_hi = w.astype(BF16)
    w_lo = (w - w_hi.astype(F32)).astype(BF16)
    for g0 in range(0, n1, group):
        lhs = g_scr[g0:g0 + group].reshape(group * rows, 2 * DFT_LANE)
        l_hi = lhs.astype(BF16)
        l_lo = (lhs - l_hi.astype(F32)).astype(BF16)
        res = (jnp.dot(l_hi, w_hi, preferred_element_type=F32) + jnp.dot(l_hi, w_lo, preferred_element_type=F32)
               + jnp.dot(l_lo, w_hi, preferred_element_type=F32))
        out_ref[0, g0:g0 + group] = res.reshape(group, rows, 2 * DFT_LANE) * inv_n


def _fspec(kt, wf, tw, n1, rows, group):
    n_rows, n = kt.shape
    tiles_per_order = HY_WIDTH // rows
    return pl.pallas_call(
        functools.partial(_fspec_kernel, n1=n1, rows=rows, group=group),
        grid=(n_rows // rows,),
        in_specs=[pl.BlockSpec((rows, n), lambda i: (i, 0)), _resident(wf.shape), _resident(tw.shape)],
        out_specs=pl.BlockSpec((1, n1, rows, 2 * DFT_LANE),
                               lambda i: (i // tiles_per_order, 0, i % tiles_per_order, 0)),
        out_shape=jax.ShapeDtypeStruct((n_rows // HY_WIDTH, n1, HY_WIDTH, 2 * DFT_LANE), F32),
        scratch_shapes=[pltpu.VMEM((n1, rows, 2 * DFT_LANE), F32)],
        compiler_params=_cparams(("parallel",)),
        name="fspec",
    )(kt, wf, tw)


def _hyena_kernel(hy_ref, wsh_ref, bsh_ref, ksp_ref, bias_ref, wf_ref, wi_ref, tw_ref, out_ref,
                  u_scr, x1_scr, x2_scr, g_scr, *, L, n1, ct, group):
    lane = lax.broadcasted_iota(jnp.int32, (ct, LANES), 1)

    def short_conv(g, dst):
        for bi in range(2):
            p = hy_ref[bi, g].astype(F32)
            left = pltpu.roll(p, 1, 1)
            left = jnp.concatenate([jnp.where(lane == 0, 0.0, left[:, :LANES]), left[:, LANES:]], axis=1)
            right = pltpu.roll(p, L - 1, 1)
            right = jnp.concatenate([right[:, :L - LANES], jnp.where(lane == LANES - 1, 0.0, right[:, L - LANES:])],
                                    axis=1)
            conv = wsh_ref[g, 0] * left + wsh_ref[g, 1] * p + wsh_ref[g, 2] * right + bsh_ref[g]
            if g == 2:
                z = hy_ref[bi, 3].astype(F32)
                conv = conv * (z * jax.nn.sigmoid(z))
            dst[bi] = conv

    short_conv(0, u_scr)
    gate_scr = (x1_scr, x2_scr)

    half_blocks = n1 // 2
    wf = wf_ref[...]
    wi = wi_ref[...]
    re, im = slice(0, DFT_LANE), slice(DFT_LANE, 2 * DFT_LANE)

    def mxu_dft(x, w):
        res = jnp.dot(x.reshape(group * ct, 2 * DFT_LANE).astype(BF16), w, preferred_element_type=F32)
        res = res.reshape(group, ct, 2 * DFT_LANE)
        return res[..., re], res[..., im]

    for order in range(2):
        def load(i, rs, ls):
            if i >= half_blocks:
                return None
            cs = slice(i * DFT_LANE + ls.start, i * DFT_LANE + ls.stop)
            return (u_scr[0, rs, cs], u_scr[1, rs, cs])

        _stage_a_forward(load, g_scr, None, n1, ct)

        def twiddled(gs):
            g = g_scr[gs]
            gr, gi = g[..., re], g[..., im]
            twr = tw_ref[gs, 0, 0:1, :]
            twi = tw_ref[gs, 1, 0:1, :]
            return jnp.concatenate([gr * twr - gi * twi, gr * twi + gi * twr], axis=-1)

        def untwiddle(gs, q):
            qr, qi = q
            twr = tw_ref[gs, 0, 0:1, :]
            twi = tw_ref[gs, 1, 0:1, :]
            g_scr[gs, :, re] = qr * twr + qi * twi
            g_scr[gs, :, im] = qi * twr - qr * twi

        groups = [slice(g0, g0 + group) for g0 in range(0, n1, group)]
        spectra = [mxu_dft(twiddled(gs), wf) for gs in groups]
        short_conv(order + 1, gate_scr[order])
        prev = None
        for gs, (hr, hi) in zip(groups, spectra):
            kr, ki = ksp_ref[order, gs, :, re], ksp_ref[order, gs, :, im]
            q = mxu_dft(jnp.concatenate([hr * kr - hi * ki, hr * ki + hi * kr], axis=-1), wi)
            if prev is not None:
                untwiddle(*prev)
            prev = (gs, q)
        untwiddle(*prev)

        def body(r, carry, order=order):
            rs = pl.ds(pl.multiple_of(r * SUBLANES, SUBLANES), SUBLANES)
            bias = bias_ref[order, rs, :]
            for half in range(DFT_LANE // LANES):
                ls = slice(half * LANES, (half + 1) * LANES)
                li = slice(DFT_LANE + half * LANES, DFT_LANE + (half + 1) * LANES)
                q = [(g_scr[k1, rs, ls], g_scr[k1, rs, li]) for k1 in range(n1)]
                ev = _fft_dif(q[0::2], 1.0)
                od = _fft_dif(q[1::2], 1.0)
                for i in range(half_blocks):
                    y = _cadd(ev[i], _cmulc(od[i], np.exp(2j * np.pi * i / n1)))
                    cs = slice(i * DFT_LANE + ls.start, i * DFT_LANE + ls.stop)
                    for bi in range(2):
                        u = u_scr[bi, rs, cs]
                        conv = y[bi] + bias * u
                        if order == 0:
                            u_scr[bi, rs, cs] = x1_scr[bi, rs, cs] * conv
                        else:
                            out_ref[bi, rs, cs] = x2_scr[bi, rs, cs] * conv
            return carry

        lax.fori_loop(0, ct // SUBLANES, body, 0)


def _hyena(hyt4, wsh, bsh, kspec, bias, wf, wi, tw, n1, ct, group):
    B, _, C, L = hyt4.shape
    return pl.pallas_call(
        functools.partial(_hyena_kernel, L=L, n1=n1, ct=ct, group=group),
        grid=(C // ct, B // 2),
        in_specs=[
            pl.BlockSpec((2, 4, ct, L), lambda c, p: (p, 0, c, 0)),
            pl.BlockSpec((3, 3, ct, 1), lambda c, p: (0, 0, c, 0)),
            pl.BlockSpec((3, ct, 1), lambda c, p: (0, c, 0)),
            pl.BlockSpec((2, n1, ct, 2 * DFT_LANE), lambda c, p: (0, 0, c, 0), pipeline_mode=pl.Buffered(1)),
            pl.BlockSpec((2, ct, 1), lambda c, p: (0, c, 0)),
            _resident(wf.shape), _resident(wi.shape), _resident(tw.shape),
        ],
        out_specs=pl.BlockSpec((2, ct, L), lambda c, p: (p, c, 0)),
        out_shape=jax.ShapeDtypeStruct((B, C, L), F32),
        scratch_shapes=[pltpu.VMEM((2, ct, L), F32), pltpu.VMEM((2, ct, L), F32), pltpu.VMEM((2, ct, L), F32),
                        pltpu.VMEM((n1, ct, 2 * DFT_LANE), F32)],
        compiler_params=_cparams(("parallel", "parallel")),
        name="hyena",
    )(hyt4, wsh, bsh, kspec, bias, wf, wi, tw)


def _attn_kernel(q_ref, kt_ref, v_ref, o_ref, *, L):
    def scores(j, c0):
        return jnp.dot(q_ref[0, j], kt_ref[0, j, :, c0:c0 + KV_CHUNK], preferred_element_type=F32)

    m = [None, None]
    acc = [None, None]
    nxt = [scores(j, 0) for j in range(2)]
    for c0 in range(0, L, KV_CHUNK):
        cur = nxt
        if c0 + KV_CHUNK < L:
            nxt = [scores(j, c0 + KV_CHUNK) for j in range(2)]
        for j in range(2):
            s = cur[j]
            mc = jnp.max(s, axis=-1, keepdims=True)
            m_new = mc if m[j] is None else jnp.maximum(m[j], mc)
            p = jnp.exp2(s - m_new).astype(BF16)
            pv = jnp.dot(p, v_ref[0, j, c0:c0 + KV_CHUNK, :], preferred_element_type=F32)
            acc[j] = pv if m[j] is None else jnp.exp2(m[j] - m_new) * acc[j] + pv
            m[j] = m_new
    outs = []
    for j in range(2):
        ones_lane = (1 - j) * V_DIM
        outs.append(acc[j] / acc[j][:, ones_lane:ones_lane + 1])
    lane = lax.broadcasted_iota(jnp.int32, outs[0].shape, 1)
    o_ref[0] = jnp.where(lane < V_DIM, outs[0], outs[1]).astype(BF16)


def _attn(q, kt, v, tq):
    B, _, L, _ = q.shape
    return pl.pallas_call(
        functools.partial(_attn_kernel, L=L),
        grid=(B, HEAD_PAIRS, L // tq),
        in_specs=[pl.BlockSpec((1, 2, tq, QK_DIM), lambda b, p, i: (b, p, i, 0)),
                  pl.BlockSpec((1, 2, QK_DIM, L), lambda b, p, i: (b, p, 0, 0)),
                  pl.BlockSpec((1, 2, L, 2 * V_DIM), lambda b, p, i: (b, p, 0, 0))],
        out_specs=pl.BlockSpec((1, tq, 2 * V_DIM), lambda b, p, i: (b, i, p)),
        out_shape=jax.ShapeDtypeStruct((B, L, ATTN_WIDTH), BF16),
        compiler_params=_cparams(("parallel", "parallel", "parallel")),
        name="attn",
    )(q, kt, v)


def _final_kernel(x_ref, yt_ref, at_ref, zat_ref, gate_ref, bg_ref, why_ref, wat_ref, wout_ref, o_ref):
    yt = yt_ref[0]
    y = yt.T.astype(BF16)
    u_hy = jnp.dot(y, why_ref[...], preferred_element_type=F32)
    z = zat_ref[0].astype(F32)
    a = (at_ref[0].astype(F32) * (z * jax.nn.sigmoid(z))).astype(BF16)
    u_at = jnp.dot(a, wat_ref[...], preferred_element_type=F32)
    gates = jax.nn.sigmoid(gate_ref[0].astype(F32) + bg_ref[...])
    merged = gates[:, :D_MODEL] * u_hy + gates[:, D_MODEL:] * u_at
    o_ref[0] = x_ref[0] + jnp.dot(merged.astype(BF16), wout_ref[...], preferred_element_type=F32)


def _final(x, yt, at, zat, gate, bg, why, wat, wout, tl):
    B, L, _ = x.shape
    tok = lambda w: pl.BlockSpec((1, tl, w), lambda b, i: (b, i, 0))
    return pl.pallas_call(
        _final_kernel,
        grid=(B, L // tl),
        in_specs=[tok(D_MODEL), pl.BlockSpec((1, HY_WIDTH, tl), lambda b, i: (b, 0, i)),
                  tok(ATTN_WIDTH), tok(ATTN_WIDTH), tok(2 * D_MODEL),
                  _resident(bg.shape), _resident(why.shape), _resident(wat.shape), _resident(wout.shape)],
        out_specs=tok(D_MODEL),
        out_shape=jax.ShapeDtypeStruct((B, L, D_MODEL), F32),
        compiler_params=_cparams(("parallel", "parallel")),
        name="final",
    )(x, yt, at, zat, gate, bg, why, wat, wout)


def _layer(x, g_norm, w_in, b_gate, w_short, b_short, w_f1, b_f1, freq_1, w_f2, b_f2, freq_2, w_f3,
           hy_bias, w_hy_out, g_cq, w_uq, g_ckv, w_ukv, g_qn, g_kn, w_attn_out, w_out):
    B, L, _ = x.shape
    n = 2 * L
    n1 = n // DFT_LANE
    assert B % 2 == 0 and n1 * DFT_LANE == n and n1 >= 2 and (n1 & (n1 - 1)) == 0
    tl = min(512, L)
    ct = 64
    group = min(8, n1)

    o_z, o_cq, o_ckv, o_kr, o_za = 3 * HY_WIDTH, 4 * HY_WIDTH, 4 * HY_WIDTH + Q_LORA, 4 * HY_WIDTH + Q_LORA + KV_LORA, \
        4 * HY_WIDTH + Q_LORA + KV_LORA + QK_ROPE
    del o_z
    wt_hy = w_in[:, :4 * HY_WIDTH].T.astype(BF16)
    w_cq = w_in[:, o_cq:o_ckv].astype(BF16)
    w_ckv = w_in[:, o_ckv:o_kr].astype(BF16)
    w_rope = w_in[:, o_kr:o_za]
    half = QK_ROPE // 2
    w_rope_sw = jnp.concatenate([w_rope[:, half:], w_rope[:, :half]], axis=1)
    zpad = jnp.zeros((D_MODEL, QK_NOPE), F32)
    wt_kr = jnp.concatenate([zpad, w_rope, zpad, w_rope_sw], axis=1).T.astype(BF16)
    w_rest = w_in[:, o_za:].astype(BF16)

    perm = np.concatenate([np.arange(QK_NOPE), QK_NOPE + half + np.arange(half), QK_NOPE + np.arange(half)])
    wq3 = w_uq.reshape(Q_LORA, MLA_HEADS, QK_DIM)
    pad_heads = lambda w: jnp.pad(w, ((0, 0), (0, 0), (0, LANES - QK_DIM))).reshape(Q_LORA, MLA_HEADS * LANES)
    wq = pad_heads(wq3).astype(BF16)
    wqs = pad_heads(wq3[:, :, perm]).astype(BF16)
    wkv = w_ukv.reshape(KV_LORA, MLA_HEADS, QK_NOPE + V_DIM).transpose(1, 0, 2)
    wk = jnp.concatenate([wkv[:, :, :QK_NOPE], jnp.zeros((MLA_HEADS, KV_LORA, QK_ROPE), F32)], axis=2)
    wkt = wk.transpose(0, 2, 1).reshape(MLA_HEADS * QK_DIM, KV_LORA).astype(BF16)
    wv_h = wkv[:, :, QK_NOPE:]
    wv_z = jnp.zeros_like(wv_h)
    odd = (np.arange(MLA_HEADS) % 2 == 1).reshape(MLA_HEADS, 1, 1)
    wv = jnp.where(odd, jnp.concatenate([wv_z, wv_h], axis=2), jnp.concatenate([wv_h, wv_z], axis=2))
    wv = wv.transpose(1, 0, 2).reshape(KV_LORA, MLA_HEADS * LANES).astype(BF16)
    pad_lanes = lambda g: jnp.pad(g, ((0, 0), (0, LANES - QK_DIM)))
    gq = g_qn.reshape(1, QK_DIM)
    gqs = pad_lanes(gq[:, perm])
    gq = pad_lanes(gq)
    gkc = g_kn.reshape(QK_DIM, 1)
    gksc = gkc[perm]

    ctab, stab = _rope_tables(L)
    pad_tab = lambda tab: np.pad(tab, ((0, 0), (0, LANES - QK_DIM)))
    hyt, zat, gate, q, kt_, v = _in_proj(
        x, g_norm.reshape(1, D_MODEL), wt_hy, w_cq, w_ckv, wt_kr, w_rest, g_cq.reshape(1, Q_LORA),
        g_ckv.reshape(1, KV_LORA), jnp.asarray(pad_tab(ctab)), jnp.asarray(pad_tab(stab)),
        jnp.asarray(np.ascontiguousarray(ctab.T)), jnp.asarray(np.ascontiguousarray(stab.T)),
        wq, wqs, wkt, wv, gq, gqs, gkc, gksc, tl)

    zfeat, t = _filter_features(L)
    w1 = jnp.concatenate([w_f1, jnp.zeros((LANES - FILTER_EMB, FILTER_HIDDEN), F32)], axis=0)
    w3 = w_f3.reshape(FILTER_HIDDEN, 2, 2, HY_WIDTH)
    w3f = w3[:, :, 0].reshape(FILTER_HIDDEN, 2 * HY_WIDTH).T
    w3b = w3[:, :, 1].reshape(FILTER_HIDDEN, 2 * HY_WIDTH).T
    deltas = np.abs(np.linspace(MIN_DECAY, MAX_DECAY, HY_WIDTH))
    dl = np.tile(deltas, 2).reshape(2 * HY_WIDTH, 1).astype(np.float32)
    col = lambda v: v.reshape(FILTER_HIDDEN, 1)
    zt = np.ascontiguousarray(np.stack([zfeat.T, zfeat[::-1].T]))
    t2 = np.ascontiguousarray(np.stack([t, t[::-1]]).reshape(2, 1, L))
    h2t = _fmlp(jnp.asarray(zt), w1.T, col(b_f1), col(freq_1), w_f2.T, col(b_f2), col(freq_2))
    kt = _filter(h2t, jnp.asarray(t2), w3f, w3b, jnp.asarray(dl), L, 128)
    wf32, wi32 = (jnp.asarray(w) for w in _dft_mats())
    tw = jnp.asarray(_twiddle(n1))
    kspec = _fspec(kt, wf32, tw, n1, ct, group)

    wsh = w_short.reshape(3, 3, HY_WIDTH).transpose(1, 0, 2)[..., None]
    bsh = b_short.reshape(3, HY_WIDTH, 1)
    yt = _hyena(hyt.reshape(B, 4, HY_WIDTH, L), wsh, bsh, kspec, hy_bias.reshape(2, HY_WIDTH, 1),
                wf32.astype(BF16), wi32.astype(BF16), tw, n1, ct, group)

    at = _attn(q, kt_, v, min(1024, L))

    return _final(x, yt, at, zat, gate, b_gate.reshape(1, 2 * D_MODEL), w_hy_out.astype(BF16),
                  w_attn_out.astype(BF16), w_out.astype(BF16), tl)


def kernel(x, g_norm, w_in, b_gate, w_short, b_short, w_f1, b_f1, freq_1, w_f2, b_f2, freq_2, w_f3, hy_bias, w_hy_out, g_cq, w_uq, g_ckv, w_ukv, g_qn, g_kn, w_attn_out, w_out):
    depth = g_norm.shape[0]
    for l in range(depth):
        x = _layer(x, g_norm[l], w_in[l], b_gate[l], w_short[l], b_short[l], w_f1[l], b_f1[l], freq_1[l], w_f2[l],
                   b_f2[l], freq_2[l], w_f3[l], hy_bias[l], w_hy_out[l], g_cq[l], w_uq[l], g_ckv[l], w_ukv[l],
                   g_qn[l], g_kn[l], w_attn_out[l], w_out[l])
    return x
```

```python
import functools
import math

import numpy as np
import jax
import jax.numpy as jnp
from jax import lax
from jax.experimental import pallas as pl
from jax.experimental.pallas import tpu as pltpu

D_MODEL = 1024
HY_WIDTH = 512
FILTER_EMB = 33
FILTER_HIDDEN = 64
DECAY_TARGET = 1e-2
FAST_DECAY = 0.3
SLOW_DECAY = 1.5
MIN_DECAY = math.log(DECAY_TARGET) / SLOW_DECAY
MAX_DECAY = math.log(DECAY_TARGET) / FAST_DECAY
MLA_HEADS = 8
HEAD_PAIRS = MLA_HEADS // 2
QK_NOPE = 64
QK_ROPE = 32
QK_DIM = QK_NOPE + QK_ROPE
V_DIM = 64
Q_LORA = 384
KV_LORA = 256
ATTN_WIDTH = MLA_HEADS * V_DIM
ROPE_THETA = 10000.0
EPS = 1e-6

LANES = 128
SUBLANES = 8
DFT_LANE = 256
KV_CHUNK = 1024
KV_AHEAD = 1
VMEM_LIMIT = 56 * 1024 * 1024

F32 = jnp.float32
BF16 = jnp.bfloat16
_NT = (((1,), (1,)), ((), ()))
_NN = (((1,), (0,)), ((), ()))


def _cparams(sem):
    return pltpu.CompilerParams(dimension_semantics=sem, vmem_limit_bytes=VMEM_LIMIT)


def _resident(shape):
    nd = len(shape)
    return pl.BlockSpec(shape, lambda *_: (0,) * nd, pipeline_mode=pl.Buffered(1))


def _add(a, b):
    if a is None:
        return b
    if b is None:
        return a
    return a + b


def _sub(a, b):
    if b is None:
        return a
    if a is None:
        return -b
    return a - b


def _cadd(x, y):
    return (_add(x[0], y[0]), _add(x[1], y[1]))


def _csub(x, y):
    return (_sub(x[0], y[0]), _sub(x[1], y[1]))


def _scale(a, s):
    if a is None or s == 0.0:
        return None
    if s == 1.0:
        return a
    if s == -1.0:
        return -a
    return a * s


def _cmulc(x, w):
    wr, wi = float(np.real(w)), float(np.imag(w))
    if abs(wr) < 1e-15:
        wr = 0.0
    if abs(wi) < 1e-15:
        wi = 0.0
    xr, xi = x
    if wr != 0.0 and abs(abs(wr) - abs(wi)) < 1e-12:
        sr, si, c = np.sign(wr), np.sign(wi), abs(wr)
        return (_scale(_sub(_scale(xr, sr), _scale(xi, si)), c), _scale(_add(_scale(xr, si), _scale(xi, sr)), c))
    re = _sub(_scale(xr, wr), _scale(xi, wi))
    im = _add(_scale(xr, wi), _scale(xi, wr))
    return (re, im)


def _cmul(x, y):
    xr, xi = x
    yr, yi = y
    re = _sub(None if xr is None else xr * yr, None if xi is None else xi * yi)
    im = _add(None if xr is None else xr * yi, None if xi is None else xi * yr)
    return (re, im)


def _fft_dif(xs, sign):
    n = len(xs)
    if n == 1:
        return xs
    half = n // 2
    a = [_cadd(xs[i], xs[i + half]) for i in range(half)]
    b = [_cmulc(_csub(xs[i], xs[i + half]), np.exp(sign * 2j * np.pi * i / n)) for i in range(half)]
    ev = _fft_dif(a, sign)
    od = _fft_dif(b, sign)
    out = [None] * n
    out[0::2] = ev
    out[1::2] = od
    return out


def _dft_mats():
    n = np.arange(DFT_LANE)
    ang = 2.0 * np.pi * np.outer(n, n) / DFT_LANE
    fr, fi = np.cos(ang), -np.sin(ang)
    fwd = np.block([[fr, fi], [-fi, fr]]).astype(np.float32)
    inv = np.block([[fr, -fi], [fi, fr]]).astype(np.float32)
    return fwd, inv


def _twiddle(n1):
    n = n1 * DFT_LANE
    ang = 2.0 * np.pi * np.outer(np.arange(n1), np.arange(DFT_LANE)) / n
    tw = np.stack([np.cos(ang), -np.sin(ang)], axis=1)
    return np.ascontiguousarray(np.broadcast_to(tw[:, :, None, :], (n1, 2, SUBLANES, DFT_LANE))).astype(np.float32)


def _filter_features(L):
    t = np.linspace(0.0, 1.0, L)[:, None]
    bands = (FILTER_EMB - 1) // 2
    f = np.linspace(1e-4, bands - 1, bands)
    ang = (2.0 * np.pi / L) * np.arange(L)[:, None] * f[None, :]
    z = np.concatenate([t, np.cos(ang), -np.sin(ang)], axis=-1)
    zp = np.zeros((L, LANES), np.float32)
    zp[:, :FILTER_EMB] = z
    return zp, t[:, 0].astype(np.float32)


def _rope_tables(L):
    pos = np.arange(L, dtype=np.float64)
    inv_freq = ROPE_THETA ** (-np.arange(0, QK_ROPE, 2, dtype=np.float64) / QK_ROPE)
    ang = pos[:, None] * inv_freq[None, :]
    c, s = np.cos(ang), np.sin(ang)
    ctab = np.concatenate([np.ones((L, QK_NOPE)), c, c], axis=1)
    stab = np.concatenate([np.zeros((L, QK_NOPE)), -s, s], axis=1)
    return ctab.astype(np.float32), stab.astype(np.float32)


def _in_proj_kernel(x_ref, gn_ref, wt_hy_ref, w_cq_ref, w_ckv_ref, w_kr_ref, w_rest_ref, gcq_ref, gckv_ref,
                    ct_ref, st_ref, ctt_ref, stt_ref, wq_ref, wqs_ref, wkt_ref, wv_ref,
                    gq_ref, gqs_ref, gkc_ref, gksc_ref, ones_ref,
                    hyt_ref, zat_ref, gate_ref, q_ref, kt_ref, v_ref):
    x = x_ref[0]
    ms = jnp.mean(x * x, axis=-1, keepdims=True)
    h = (x * lax.rsqrt(ms + EPS) * gn_ref[...]).astype(BF16)

    def latent(w_ref, g_ref):
        c = jnp.dot(h, w_ref[...], preferred_element_type=F32)
        r = lax.rsqrt(jnp.mean(c * c, axis=-1, keepdims=True) + EPS)
        return (c * r * g_ref[...]).astype(BF16)

    cq = latent(w_cq_ref, gcq_ref)
    ckv = latent(w_ckv_ref, gckv_ref)
    kr = lax.dot_general(w_kr_ref[...], h, _NT, preferred_element_type=F32)
    qa_all = jnp.dot(cq, wq_ref[...], preferred_element_type=F32)
    qw_all = jnp.dot(cq, wqs_ref[...], preferred_element_type=F32)
    ka_all = lax.dot_general(wkt_ref[...], ckv, _NT, preferred_element_type=F32)
    v_all = jnp.dot(ckv, wv_ref[...], preferred_element_type=F32)

    hyt_ref[0] = lax.dot_general(wt_hy_ref[...], h, _NT, preferred_element_type=F32).astype(BF16)
    rest = jnp.dot(h, w_rest_ref[...], preferred_element_type=F32)
    zat_ref[0] = rest[:, :ATTN_WIDTH].astype(BF16)
    gate_ref[0] = rest[:, ATTN_WIDTH:].astype(BF16)

    half = QK_ROPE // 2
    scale = math.log2(math.e) / math.sqrt(QK_DIM)
    qc, qs = gq_ref[...] * ct_ref[...], gqs_ref[...] * st_ref[...]
    kc, ks = gkc_ref[...] * ctt_ref[...], gksc_ref[...] * stt_ref[...]
    for hd in range(MLA_HEADS):
        hs = slice(hd * LANES, (hd + 1) * LANES)
        qa, qw = qa_all[:, hs], qw_all[:, hs]
        r = lax.rsqrt(jnp.sum(qa * qa, axis=-1, keepdims=True) * (1.0 / QK_DIM) + EPS) * scale
        q_ref[0, hd] = ((qa * qc + qw * qs) * r)[:, :QK_DIM].astype(BF16)
        ka = ka_all[hd * QK_DIM:(hd + 1) * QK_DIM] + kr
        kw = jnp.concatenate([ka[:QK_NOPE], ka[QK_NOPE + half:], ka[QK_NOPE:QK_NOPE + half]], axis=0)
        rk = lax.rsqrt(jnp.mean(ka * ka, axis=0, keepdims=True) + EPS)
        kt_ref[0, hd] = ((ka * kc + kw * ks) * rk).astype(BF16)
        v_ref[0, hd] = (v_all[:, hs] + ones_ref[hd % 2]).astype(BF16)


def _in_proj(x, gn, wt_hy, w_cq, w_ckv, w_kr, w_rest, gcq, gckv, ctab, stab, ctab_t, stab_t,
             wq, wqs, wkt, wv, gq, gqs, gkc, gksc, tl):
    ones = np.zeros((2, 1, 2 * V_DIM), np.float32)
    ones[0, 0, V_DIM] = 1.0
    ones[1, 0, 0] = 1.0
    ones = jnp.asarray(ones)
    B, L, _ = x.shape
    n_rest = w_rest.shape[1]
    tab = pl.BlockSpec((tl, LANES), lambda b, i: (i, 0))
    tab_t = pl.BlockSpec((QK_DIM, tl), lambda b, i: (0, i))
    consts = (gn, wt_hy, w_cq, w_ckv, w_kr, w_rest, gcq, gckv)
    qkv_consts = (wq, wqs, wkt, wv, gq, gqs, gkc, gksc, ones)
    return pl.pallas_call(
        _in_proj_kernel,
        grid=(B, L // tl),
        in_specs=[pl.BlockSpec((1, tl, D_MODEL), lambda b, i: (b, i, 0))]
        + [_resident(a.shape) for a in consts] + [tab, tab, tab_t, tab_t] + [_resident(a.shape) for a in qkv_consts],
        out_specs=[
            pl.BlockSpec((1, 4 * HY_WIDTH, tl), lambda b, i: (b, 0, i)),
            pl.BlockSpec((1, tl, ATTN_WIDTH), lambda b, i: (b, i, 0)),
            pl.BlockSpec((1, tl, 2 * D_MODEL), lambda b, i: (b, i, 0)),
            pl.BlockSpec((1, MLA_HEADS, tl, QK_DIM), lambda b, i: (b, 0, i, 0)),
            pl.BlockSpec((1, MLA_HEADS, QK_DIM, tl), lambda b, i: (b, 0, 0, i)),
            pl.BlockSpec((1, MLA_HEADS, tl, 2 * V_DIM), lambda b, i: (b, 0, i, 0)),
        ],
        out_shape=[
            jax.ShapeDtypeStruct((B, 4 * HY_WIDTH, L), BF16),
            jax.ShapeDtypeStruct((B, L, ATTN_WIDTH), BF16),
            jax.ShapeDtypeStruct((B, L, n_rest - ATTN_WIDTH), BF16),
            jax.ShapeDtypeStruct((B, MLA_HEADS, L, QK_DIM), BF16),
            jax.ShapeDtypeStruct((B, MLA_HEADS, QK_DIM, L), BF16),
            jax.ShapeDtypeStruct((B, MLA_HEADS, L, 2 * V_DIM), BF16),
        ],
        compiler_params=_cparams(("parallel", "parallel")),
        name="in_proj",
    )(x, *consts, ctab, stab, ctab_t, stab_t, *qkv_consts)


def _fmlp_kernel(zt_ref, w1t_ref, b1_ref, f1_ref, w2t_ref, b2_ref, f2_ref, out_ref):
    hp = lax.Precision.HIGHEST
    h = jnp.sin(f1_ref[...] * (jnp.dot(w1t_ref[...], zt_ref[0], precision=hp, preferred_element_type=F32) + b1_ref[...]))
    out_ref[0] = jnp.sin(f2_ref[...] * (jnp.dot(w2t_ref[...], h, precision=hp, preferred_element_type=F32) + b2_ref[...]))


def _fmlp(zt, w1t, b1, f1, w2t, b2, f2):
    _, _, L = zt.shape
    return pl.pallas_call(
        _fmlp_kernel,
        grid=(2,),
        in_specs=[pl.BlockSpec((1, LANES, L), lambda d: (d, 0, 0)), _resident(w1t.shape), _resident(b1.shape),
                  _resident(f1.shape), _resident(w2t.shape), _resident(b2.shape), _resident(f2.shape)],
        out_specs=pl.BlockSpec((1, FILTER_HIDDEN, L), lambda d: (d, 0, 0)),
        out_shape=jax.ShapeDtypeStruct((2, FILTER_HIDDEN, L), F32),
        compiler_params=_cparams(("parallel",)),
        name="fmlp",
    )(zt, w1t, b1, f1, w2t, b2, f2)


def _stage_a_forward(load, g_scr, tw_ref, n1, rows_total):
    def body(r, carry):
        rs = pl.ds(pl.multiple_of(r * SUBLANES, SUBLANES), SUBLANES)
        for half in range(DFT_LANE // LANES):
            ls = slice(half * LANES, (half + 1) * LANES)
            li = slice(DFT_LANE + half * LANES, DFT_LANE + (half + 1) * LANES)
            xs = [load(i, rs, ls) for i in range(n1)]
            xs = [(None, None) if x is None else x for x in xs]
            X = _fft_dif(xs, -1.0)
            for k1 in range(n1):
                if tw_ref is None:
                    yr, yi = X[k1]
                else:
                    yr, yi = _cmul(X[k1], (tw_ref[k1, 0, :, ls], tw_ref[k1, 1, :, ls]))
                g_scr[k1, rs, ls] = yr
                g_scr[k1, rs, li] = yi
        return carry

    lax.fori_loop(0, rows_total // SUBLANES, body, 0)


def _fspec_kernel(h_ref, t_ref, w3f_ref, w3b_ref, dl_ref, wf_ref, tw_ref, out_ref, k_scr, g_scr,
                  *, L, n1, rows, group):
    dl = dl_ref[...]
    for d, w_ref in enumerate((w3f_ref, w3b_ref)):
        k = jnp.dot(w_ref[...], h_ref[d], precision=lax.Precision.HIGHEST, preferred_element_type=F32)
        k_scr[:, d * L:(d + 1) * L] = k * jnp.exp(-(dl * t_ref[d]))

    def load(i, rs, ls):
        return (k_scr[rs, i * DFT_LANE + ls.start:i * DFT_LANE + ls.stop], None)

    _stage_a_forward(load, g_scr, tw_ref, n1, rows)
    inv_n = 1.0 / (n1 * DFT_LANE)
    w = wf_ref[...]
    w_hi = w.astype(BF16)
    w_lo = (w - w_hi.astype(F32)).astype(BF16)
    for g0 in range(0, n1, group):
        lhs = g_scr[g0:g0 + group].reshape(group * rows, 2 * DFT_LANE)
        l_hi = lhs.astype(BF16)
        l_lo = (lhs - l_hi.astype(F32)).astype(BF16)
        res = (jnp.dot(l_hi, w_hi, preferred_element_type=F32) + jnp.dot(l_hi, w_lo, preferred_element_type=F32)
               + jnp.dot(l_lo, w_hi, preferred_element_type=F32))
        out_ref[0, g0:g0 + group] = res.reshape(group, rows, 2 * DFT_LANE) * inv_n


def _fspec(h2t, t2, w3f, w3b, dl, wf, tw, L, n1, rows, group):
    n_rows = w3f.shape[0]
    tiles_per_order = HY_WIDTH // rows
    row_spec = lambda shape: pl.BlockSpec(shape, lambda i: (i, 0))
    return pl.pallas_call(
        functools.partial(_fspec_kernel, L=L, n1=n1, rows=rows, group=group),
        grid=(n_rows // rows,),
        in_specs=[_resident(h2t.shape), _resident(t2.shape),
                  row_spec((rows, FILTER_HIDDEN)), row_spec((rows, FILTER_HIDDEN)), row_spec((rows, 1)),
                  _resident(wf.shape), _resident(tw.shape)],
        out_specs=pl.BlockSpec((1, n1, rows, 2 * DFT_LANE),
                               lambda i: (i // tiles_per_order, 0, i % tiles_per_order, 0)),
        out_shape=jax.ShapeDtypeStruct((n_rows // HY_WIDTH, n1, HY_WIDTH, 2 * DFT_LANE), F32),
        scratch_shapes=[pltpu.VMEM((rows, 2 * L), F32), pltpu.VMEM((n1, rows, 2 * DFT_LANE), F32)],
        compiler_params=_cparams(("parallel",)),
        name="fspec",
    )(h2t, t2, w3f, w3b, dl, wf, tw)


def _hyena_kernel(hy_ref, wsh_ref, bsh_ref, ksp_ref, bias_ref, wf_ref, wi_ref, tw_ref, out_ref,
                  u_scr, x1_scr, x2_scr, g_scr, *, L, n1, ct, group):
    lane = lax.broadcasted_iota(jnp.int32, (ct, LANES), 1)

    def short_conv(g, dst):
        for bi in range(2):
            p = hy_ref[bi, g].astype(F32)
            left = pltpu.roll(p, 1, 1)
            left = jnp.concatenate([jnp.where(lane == 0, 0.0, left[:, :LANES]), left[:, LANES:]], axis=1)
            right = pltpu.roll(p, L - 1, 1)
            right = jnp.concatenate([right[:, :L - LANES], jnp.where(lane == LANES - 1, 0.0, right[:, L - LANES:])],
                                    axis=1)
            conv = wsh_ref[g, 0] * left + wsh_ref[g, 1] * p + wsh_ref[g, 2] * right + bsh_ref[g]
            if g == 2:
                z = hy_ref[bi, 3].astype(F32)
                conv = conv * (z * jax.nn.sigmoid(z))
            dst[bi] = conv

    short_conv(0, u_scr)
    gate_scr = (x1_scr, x2_scr)

    half_blocks = n1 // 2
    wf = wf_ref[...]
    wi = wi_ref[...]
    re, im = slice(0, DFT_LANE), slice(DFT_LANE, 2 * DFT_LANE)

    def mxu_dft(x, w):
        res = jnp.dot(x.reshape(group * ct, 2 * DFT_LANE).astype(BF16), w, preferred_element_type=F32)
        res = res.reshape(group, ct, 2 * DFT_LANE)
        return res[..., re], res[..., im]

    for order in range(2):
        def load(i, rs, ls):
            if i >= half_blocks:
                return None
            cs = slice(i * DFT_LANE + ls.start, i * DFT_LANE + ls.stop)
            return (u_scr[0, rs, cs], u_scr[1, rs, cs])

        _stage_a_forward(load, g_scr, None, n1, ct)

        def twiddled(gs):
            g = g_scr[gs]
            gr, gi = g[..., re], g[..., im]
            twr = tw_ref[gs, 0, 0:1, :]
            twi = tw_ref[gs, 1, 0:1, :]
            return jnp.concatenate([gr * twr - gi * twi, gr * twi + gi * twr], axis=-1)

        def untwiddle(gs, q):
            qr, qi = q
            twr = tw_ref[gs, 0, 0:1, :]
            twi = tw_ref[gs, 1, 0:1, :]
            g_scr[gs, :, re] = qr * twr + qi * twi
            g_scr[gs, :, im] = qi * twr - qr * twi

        groups = [slice(g0, g0 + group) for g0 in range(0, n1, group)]
        spectra = [mxu_dft(twiddled(gs), wf) for gs in groups]
        short_conv(order + 1, gate_scr[order])
        prev = None
        for gs, (hr, hi) in zip(groups, spectra):
            kr, ki = ksp_ref[order, gs, :, re], ksp_ref[order, gs, :, im]
            q = mxu_dft(jnp.concatenate([hr * kr - hi * ki, hr * ki + hi * kr], axis=-1), wi)
            if prev is not None:
                untwiddle(*prev)
            prev = (gs, q)
        untwiddle(*prev)

        def body(r, carry, order=order):
            rs = pl.ds(pl.multiple_of(r * SUBLANES, SUBLANES), SUBLANES)
            bias = bias_ref[order, rs, :]
            for half in range(DFT_LANE // LANES):
                ls = slice(half * LANES, (half + 1) * LANES)
                li = slice(DFT_LANE + half * LANES, DFT_LANE + (half + 1) * LANES)
                q = [(g_scr[k1, rs, ls], g_scr[k1, rs, li]) for k1 in range(n1)]
                ev = _fft_dif(q[0::2], 1.0)
                od = _fft_dif(q[1::2], 1.0)
                for i in range(half_blocks):
                    y = _cadd(ev[i], _cmulc(od[i], np.exp(2j * np.pi * i / n1)))
                    cs = slice(i * DFT_LANE + ls.start, i * DFT_LANE + ls.stop)
                    for bi in range(2):
                        u = u_scr[bi, rs, cs]
                        conv = y[bi] + bias * u
                        if order == 0:
                            u_scr[bi, rs, cs] = x1_scr[bi, rs, cs] * conv
                        else:
                            out_ref[bi, rs, cs] = x2_scr[bi, rs, cs] * conv
            return carry

        lax.fori_loop(0, ct // SUBLANES, body, 0)


def _hyena(hyt4, wsh, bsh, kspec, bias, wf, wi, tw, n1, ct, group):
    B, _, C, L = hyt4.shape
    return pl.pallas_call(
        functools.partial(_hyena_kernel, L=L, n1=n1, ct=ct, group=group),
        grid=(C // ct, B // 2),
        in_specs=[
            pl.BlockSpec((2, 4, ct, L), lambda c, p: (p, 0, c, 0)),
            pl.BlockSpec((3, 3, ct, 1), lambda c, p: (0, 0, c, 0)),
            pl.BlockSpec((3, ct, 1), lambda c, p: (0, c, 0)),
            pl.BlockSpec((2, n1, ct, 2 * DFT_LANE), lambda c, p: (0, 0, c, 0), pipeline_mode=pl.Buffered(1)),
            pl.BlockSpec((2, ct, 1), lambda c, p: (0, c, 0)),
            _resident(wf.shape), _resident(wi.shape), _resident(tw.shape),
        ],
        out_specs=pl.BlockSpec((2, ct, L), lambda c, p: (p, c, 0)),
        out_shape=jax.ShapeDtypeStruct((B, C, L), F32),
        scratch_shapes=[pltpu.VMEM((2, ct, L), F32), pltpu.VMEM((2, ct, L), F32), pltpu.VMEM((2, ct, L), F32),
                        pltpu.VMEM((n1, ct, 2 * DFT_LANE), F32)],
        compiler_params=_cparams(("parallel", "parallel")),
        name="hyena",
    )(hyt4, wsh, bsh, kspec, bias, wf, wi, tw)


def _attn_kernel(q_ref, kt_ref, v_ref, o_ref, *, L):
    def scores(j, c0):
        return jnp.dot(q_ref[0, j], kt_ref[0, j, :, c0:c0 + KV_CHUNK], preferred_element_type=F32)

    m = [None, None]
    acc = [None, None]
    starts = list(range(0, L, KV_CHUNK))
    ahead = {i: [scores(j, c0) for j in range(2)] for i, c0 in enumerate(starts[:KV_AHEAD])}
    for i, c0 in enumerate(starts):
        cur = ahead.pop(i)
        if i + KV_AHEAD < len(starts):
            ahead[i + KV_AHEAD] = [scores(j, starts[i + KV_AHEAD]) for j in range(2)]
        for j in range(2):
            s = cur[j]
            mc = jnp.max(s, axis=-1, keepdims=True)
            m_new = mc if m[j] is None else jnp.maximum(m[j], mc)
            p = jnp.exp2(s - m_new).astype(BF16)
            pv = jnp.dot(p, v_ref[0, j, c0:c0 + KV_CHUNK, :], preferred_element_type=F32)
            acc[j] = pv if m[j] is None else jnp.exp2(m[j] - m_new) * acc[j] + pv
            m[j] = m_new
    outs = []
    for j in range(2):
        ones_lane = (1 - j) * V_DIM
        outs.append(acc[j] / acc[j][:, ones_lane:ones_lane + 1])
    lane = lax.broadcasted_iota(jnp.int32, outs[0].shape, 1)
    o_ref[0] = jnp.where(lane < V_DIM, outs[0], outs[1]).astype(BF16)


def _attn(q, kt, v, tq):
    B, _, L, _ = q.shape
    return pl.pallas_call(
        functools.partial(_attn_kernel, L=L),
        grid=(B, HEAD_PAIRS, L // tq),
        in_specs=[pl.BlockSpec((1, 2, tq, QK_DIM), lambda b, p, i: (b, p, i, 0)),
                  pl.BlockSpec((1, 2, QK_DIM, L), lambda b, p, i: (b, p, 0, 0)),
                  pl.BlockSpec((1, 2, L, 2 * V_DIM), lambda b, p, i: (b, p, 0, 0))],
        out_specs=pl.BlockSpec((1, tq, 2 * V_DIM), lambda b, p, i: (b, i, p)),
        out_shape=jax.ShapeDtypeStruct((B, L, ATTN_WIDTH), BF16),
        compiler_params=_cparams(("parallel", "parallel", "parallel")),
        name="attn",
    )(q, kt, v)


def _final_kernel(x_ref, yt_ref, at_ref, zat_ref, gate_ref, bg_ref, why_ref, wat_ref, wout_ref, o_ref):
    yt = yt_ref[0]
    y = yt.T.astype(BF16)
    u_hy = jnp.dot(y, why_ref[...], preferred_element_type=F32)
    z = zat_ref[0].astype(F32)
    a = (at_ref[0].astype(F32) * (z * jax.nn.sigmoid(z))).astype(BF16)
    u_at = jnp.dot(a, wat_ref[...], preferred_element_type=F32)
    gates = jax.nn.sigmoid(gate_ref[0].astype(F32) + bg_ref[...])
    merged = gates[:, :D_MODEL] * u_hy + gates[:, D_MODEL:] * u_at
    o_ref[0] = x_ref[0] + jnp.dot(merged.astype(BF16), wout_ref[...], preferred_element_type=F32)


def _final(x, yt, at, zat, gate, bg, why, wat, wout, tl):
    B, L, _ = x.shape
    tok = lambda w: pl.BlockSpec((1, tl, w), lambda b, i: (b, i, 0))
    return pl.pallas_call(
        _final_kernel,
        grid=(B, L // tl),
        in_specs=[tok(D_MODEL), pl.BlockSpec((1, HY_WIDTH, tl), lambda b, i: (b, 0, i)),
                  tok(ATTN_WIDTH), tok(ATTN_WIDTH), tok(2 * D_MODEL),
                  _resident(bg.shape), _resident(why.shape), _resident(wat.shape), _resident(wout.shape)],
        out_specs=tok(D_MODEL),
        out_shape=jax.ShapeDtypeStruct((B, L, D_MODEL), F32),
        compiler_params=_cparams(("parallel", "parallel")),
        name="final",
    )(x, yt, at, zat, gate, bg, why, wat, wout)


def _layer(x, g_norm, w_in, b_gate, w_short, b_short, w_f1, b_f1, freq_1, w_f2, b_f2, freq_2, w_f3,
           hy_bias, w_hy_out, g_cq, w_uq, g_ckv, w_ukv, g_qn, g_kn, w_attn_out, w_out):
    B, L, _ = x.shape
    n = 2 * L
    n1 = n // DFT_LANE
    assert B % 2 == 0 and n1 * DFT_LANE == n and n1 >= 2 and (n1 & (n1 - 1)) == 0
    tl = min(512, L)
    ct = 64
    group = min(8, n1)

    o_z, o_cq, o_ckv, o_kr, o_za = 3 * HY_WIDTH, 4 * HY_WIDTH, 4 * HY_WIDTH + Q_LORA, 4 * HY_WIDTH + Q_LORA + KV_LORA, \
        4 * HY_WIDTH + Q_LORA + KV_LORA + QK_ROPE
    del o_z
    w_in = w_in.astype(BF16)
    wt_hy = w_in[:, :4 * HY_WIDTH].T
    w_cq = w_in[:, o_cq:o_ckv]
    w_ckv = w_in[:, o_ckv:o_kr]
    half = QK_ROPE // 2
    wt_kr = jnp.concatenate([jnp.zeros((D_MODEL, QK_NOPE), BF16), w_in[:, o_kr:o_za]], axis=1).T
    w_rest = w_in[:, o_za:]

    perm = np.concatenate([np.arange(QK_NOPE), QK_NOPE + half + np.arange(half), QK_NOPE + np.arange(half)])
    wq3 = w_uq.reshape(Q_LORA, MLA_HEADS, QK_DIM)
    pad_heads = lambda w: jnp.pad(w, ((0, 0), (0, 0), (0, LANES - QK_DIM))).reshape(Q_LORA, MLA_HEADS * LANES)
    wq = pad_heads(wq3).astype(BF16)
    wqs = pad_heads(wq3[:, :, perm]).astype(BF16)
    wkv = w_ukv.reshape(KV_LORA, MLA_HEADS, QK_NOPE + V_DIM).transpose(1, 0, 2)
    wk = jnp.concatenate([wkv[:, :, :QK_NOPE], jnp.zeros((MLA_HEADS, KV_LORA, QK_ROPE), F32)], axis=2)
    wkt = wk.transpose(0, 2, 1).reshape(MLA_HEADS * QK_DIM, KV_LORA).astype(BF16)
    wv_h = wkv[:, :, QK_NOPE:]
    wv_z = jnp.zeros_like(wv_h)
    odd = (np.arange(MLA_HEADS) % 2 == 1).reshape(MLA_HEADS, 1, 1)
    wv = jnp.where(odd, jnp.concatenate([wv_z, wv_h], axis=2), jnp.concatenate([wv_h, wv_z], axis=2))
    wv = wv.transpose(1, 0, 2).reshape(KV_LORA, MLA_HEADS * LANES).astype(BF16)
    pad_lanes = lambda g: jnp.pad(g, ((0, 0), (0, LANES - QK_DIM)))
    gq = g_qn.reshape(1, QK_DIM)
    gqs = pad_lanes(gq[:, perm])
    gq = pad_lanes(gq)
    gkc = g_kn.reshape(QK_DIM, 1)
    gksc = gkc[perm]

    ctab, stab = _rope_tables(L)
    pad_tab = lambda tab: np.pad(tab, ((0, 0), (0, LANES - QK_DIM)))
    hyt, zat, gate, q, kt_, v = _in_proj(
        x, g_norm.reshape(1, D_MODEL), wt_hy, w_cq, w_ckv, wt_kr, w_rest, g_cq.reshape(1, Q_LORA),
        g_ckv.reshape(1, KV_LORA), jnp.asarray(pad_tab(ctab)), jnp.asarray(pad_tab(stab)),
        jnp.asarray(np.ascontiguousarray(ctab.T)), jnp.asarray(np.ascontiguousarray(stab.T)),
        wq, wqs, wkt, wv, gq, gqs, gkc, gksc, tl)

    zfeat, t = _filter_features(L)
    w1 = jnp.concatenate([w_f1, jnp.zeros((LANES - FILTER_EMB, FILTER_HIDDEN), F32)], axis=0)
    w3 = w_f3.reshape(FILTER_HIDDEN, 2, 2, HY_WIDTH)
    w3f = w3[:, :, 0].reshape(FILTER_HIDDEN, 2 * HY_WIDTH).T
    w3b = w3[:, :, 1].reshape(FILTER_HIDDEN, 2 * HY_WIDTH).T
    deltas = np.abs(np.linspace(MIN_DECAY, MAX_DECAY, HY_WIDTH))
    dl = np.tile(deltas, 2).reshape(2 * HY_WIDTH, 1).astype(np.float32)
    col = lambda v: v.reshape(FILTER_HIDDEN, 1)
    zt = np.ascontiguousarray(np.stack([zfeat.T, zfeat[::-1].T]))
    t2 = np.ascontiguousarray(np.stack([t, t[::-1]]).reshape(2, 1, L))
    h2t = _fmlp(jnp.asarray(zt), w1.T, col(b_f1), col(freq_1), w_f2.T, col(b_f2), col(freq_2))
    wf32, wi32 = (jnp.asarray(w) for w in _dft_mats())
    tw = jnp.asarray(_twiddle(n1))
    kspec = _fspec(h2t, jnp.asarray(t2), w3f, w3b, jnp.asarray(dl), wf32, tw, L, n1, ct, group)

    wsh = w_short.reshape(3, 3, HY_WIDTH).transpose(1, 0, 2)[..., None]
    bsh = b_short.reshape(3, HY_WIDTH, 1)
    yt = _hyena(hyt.reshape(B, 4, HY_WIDTH, L), wsh, bsh, kspec, hy_bias.reshape(2, HY_WIDTH, 1),
                wf32.astype(BF16), wi32.astype(BF16), tw, n1, ct, group)

    at = _attn(q, kt_, v, min(1024, L))

    return _final(x, yt, at, zat, gate, b_gate.reshape(1, 2 * D_MODEL), w_hy_out.astype(BF16),
                  w_attn_out.astype(BF16), w_out.astype(BF16), tl)


def kernel(x, g_norm, w_in, b_gate, w_short, b_short, w_f1, b_f1, freq_1, w_f2, b_f2, freq_2, w_f3, hy_bias, w_hy_out, g_cq, w_uq, g_ckv, w_ukv, g_qn, g_kn, w_attn_out, w_out):
    depth = g_norm.shape[0]
    for l in range(depth):
        x = _layer(x, g_norm[l], w_in[l], b_gate[l], w_short[l], b_short[l], w_f1[l], b_f1[l], freq_1[l], w_f2[l],
                   b_f2[l], freq_2[l], w_f3[l], hy_bias[l], w_hy_out[l], g_cq[l], w_uq[l], g_ckv[l], w_ukv[l],
                   g_qn[l], g_kn[l], w_attn_out[l], w_out[l])
    return x
```

```python
import functools
import math

import numpy as np
import jax
import jax.numpy as jnp
from jax import lax
from jax.experimental import pallas as pl
from jax.experimental.pallas import tpu as pltpu

D_MODEL = 1024
HY_WIDTH = 512
FILTER_EMB = 33
FILTER_HIDDEN = 64
DECAY_TARGET = 1e-2
FAST_DECAY = 0.3
SLOW_DECAY = 1.5
MIN_DECAY = math.log(DECAY_TARGET) / SLOW_DECAY
MAX_DECAY = math.log(DECAY_TARGET) / FAST_DECAY
MLA_HEADS = 8
HEAD_PAIRS = MLA_HEADS // 2
QK_NOPE = 64
QK_ROPE = 32
QK_DIM = QK_NOPE + QK_ROPE
V_DIM = 64
Q_LORA = 384
KV_LORA = 256
ATTN_WIDTH = MLA_HEADS * V_DIM
ROPE_THETA = 10000.0
EPS = 1e-6

LANES = 128
SUBLANES = 8
DFT_LANE = 256
V_ROWS = 80
KV_CHUNK = 256
Q_TILE = 256
VMEM_LIMIT = 56 * 1024 * 1024

F32 = jnp.float32
BF16 = jnp.bfloat16
_NT = (((1,), (1,)), ((), ()))
_NN = (((1,), (0,)), ((), ()))


def _cparams(sem):
    return pltpu.CompilerParams(dimension_semantics=sem, vmem_limit_bytes=VMEM_LIMIT)


def _resident(shape):
    nd = len(shape)
    return pl.BlockSpec(shape, lambda *_: (0,) * nd, pipeline_mode=pl.Buffered(1))


def _add(a, b):
    if a is None:
        return b
    if b is None:
        return a
    return a + b


def _sub(a, b):
    if b is None:
        return a
    if a is None:
        return -b
    return a - b


def _cadd(x, y):
    return (_add(x[0], y[0]), _add(x[1], y[1]))


def _csub(x, y):
    return (_sub(x[0], y[0]), _sub(x[1], y[1]))


def _scale(a, s):
    if a is None or s == 0.0:
        return None
    if s == 1.0:
        return a
    if s == -1.0:
        return -a
    return a * s


def _cmulc(x, w):
    wr, wi = float(np.real(w)), float(np.imag(w))
    if abs(wr) < 1e-15:
        wr = 0.0
    if abs(wi) < 1e-15:
        wi = 0.0
    xr, xi = x
    if wr != 0.0 and abs(abs(wr) - abs(wi)) < 1e-12:
        sr, si, c = np.sign(wr), np.sign(wi), abs(wr)
        return (_scale(_sub(_scale(xr, sr), _scale(xi, si)), c), _scale(_add(_scale(xr, si), _scale(xi, sr)), c))
    re = _sub(_scale(xr, wr), _scale(xi, wi))
    im = _add(_scale(xr, wi), _scale(xi, wr))
    return (re, im)


def _cmul(x, y):
    xr, xi = x
    yr, yi = y
    re = _sub(None if xr is None else xr * yr, None if xi is None else xi * yi)
    im = _add(None if xr is None else xr * yi, None if xi is None else xi * yr)
    return (re, im)


def _fft_dif(xs, sign):
    n = len(xs)
    if n == 1:
        return xs
    half = n // 2
    a = [_cadd(xs[i], xs[i + half]) for i in range(half)]
    b = [_cmulc(_csub(xs[i], xs[i + half]), np.exp(sign * 2j * np.pi * i / n)) for i in range(half)]
    ev = _fft_dif(a, sign)
    od = _fft_dif(b, sign)
    out = [None] * n
    out[0::2] = ev
    out[1::2] = od
    return out


def _dft_mats():
    n = np.arange(DFT_LANE)
    ang = 2.0 * np.pi * np.outer(n, n) / DFT_LANE
    fr, fi = np.cos(ang), -np.sin(ang)
    fwd = np.block([[fr, fi], [-fi, fr]]).astype(np.float32)
    inv = np.block([[fr, -fi], [fi, fr]]).astype(np.float32)
    return fwd, inv


def _twiddle(n1):
    n = n1 * DFT_LANE
    ang = 2.0 * np.pi * np.outer(np.arange(n1), np.arange(DFT_LANE)) / n
    tw = np.stack([np.cos(ang), -np.sin(ang)], axis=1)
    return np.ascontiguousarray(np.broadcast_to(tw[:, :, None, :], (n1, 2, SUBLANES, DFT_LANE))).astype(np.float32)


def _filter_features(L):
    t = np.linspace(0.0, 1.0, L)[:, None]
    bands = (FILTER_EMB - 1) // 2
    f = np.linspace(1e-4, bands - 1, bands)
    ang = (2.0 * np.pi / L) * np.arange(L)[:, None] * f[None, :]
    z = np.concatenate([t, np.cos(ang), -np.sin(ang)], axis=-1)
    zp = np.zeros((L, LANES), np.float32)
    zp[:, :FILTER_EMB] = z
    return zp, t[:, 0].astype(np.float32)


def _rope_tables(L):
    pos = np.arange(L, dtype=np.float64)
    inv_freq = ROPE_THETA ** (-np.arange(0, QK_ROPE, 2, dtype=np.float64) / QK_ROPE)
    ang = pos[:, None] * inv_freq[None, :]
    c, s = np.cos(ang), np.sin(ang)
    ctab = np.concatenate([np.ones((L, QK_NOPE)), c, c], axis=1)
    stab = np.concatenate([np.zeros((L, QK_NOPE)), -s, s], axis=1)
    return ctab.astype(np.float32), stab.astype(np.float32)


def _in_proj_kernel(x_ref, gn_ref, wt_hy_ref, w_cq_ref, w_ckv_ref, w_kr_ref, w_rest_ref, gcq_ref, gckv_ref,
                    ct_ref, st_ref, ctt_ref, stt_ref, wqt_ref, wk_ref, wvt_ref,
                    gqc_ref, gqsc_ref, gk_ref, gks_ref, ones_ref,
                    hyt_ref, zat_ref, gate_ref, qt_ref, k_ref, vt_ref):
    x = x_ref[0]
    ms = jnp.mean(x * x, axis=-1, keepdims=True)
    h = (x * lax.rsqrt(ms + EPS) * gn_ref[...]).astype(BF16)

    def latent(w_ref, g_ref):
        c = jnp.dot(h, w_ref[...], preferred_element_type=F32)
        r = lax.rsqrt(jnp.mean(c * c, axis=-1, keepdims=True) + EPS)
        return (c * r * g_ref[...]).astype(BF16)

    cq = latent(w_cq_ref, gcq_ref)
    ckv = latent(w_ckv_ref, gckv_ref)
    kr = jnp.dot(h, w_kr_ref[...], preferred_element_type=F32)
    qt_all = lax.dot_general(wqt_ref[...], cq, _NT, preferred_element_type=F32)
    ka_all = jnp.dot(ckv, wk_ref[...], preferred_element_type=F32)
    vt_all = lax.dot_general(wvt_ref[...], ckv, _NT, preferred_element_type=F32) + ones_ref[...]

    hyt_ref[0] = lax.dot_general(wt_hy_ref[...], h, _NT, preferred_element_type=F32).astype(BF16)
    rest = jnp.dot(h, w_rest_ref[...], preferred_element_type=F32)
    zat_ref[0] = rest[:, :ATTN_WIDTH].astype(BF16)
    gate_ref[0] = rest[:, ATTN_WIDTH:].astype(BF16)

    half = QK_ROPE // 2
    scale = math.log2(math.e) / math.sqrt(QK_DIM)
    qc, qs = gqc_ref[...] * ctt_ref[...], gqsc_ref[...] * stt_ref[...]
    kc, ks = gk_ref[...] * ct_ref[...], gks_ref[...] * st_ref[...]
    kr_a, kr_w = kr[:, :LANES], kr[:, LANES:]
    for hd in range(MLA_HEADS):
        qa = qt_all[hd * QK_DIM:(hd + 1) * QK_DIM]
        qw = jnp.concatenate([qa[:QK_NOPE], qa[QK_NOPE + half:], qa[QK_NOPE:QK_NOPE + half]], axis=0)
        r = lax.rsqrt(jnp.mean(qa * qa, axis=0, keepdims=True) + EPS) * scale
        qt_ref[0, hd] = ((qa * qc + qw * qs) * r).astype(BF16)
        ka = ka_all[:, hd * LANES:(hd + 1) * LANES] + kr_a
        rk = lax.rsqrt(jnp.sum(ka * ka, axis=-1, keepdims=True) * (1.0 / QK_DIM) + EPS)
        k_ref[0, hd] = ((ka * kc + kr_w * ks) * rk)[:, :QK_DIM].astype(BF16)
        vt_ref[0, hd] = vt_all[hd * V_ROWS:(hd + 1) * V_ROWS].astype(BF16)


def _in_proj(x, gn, wt_hy, w_cq, w_ckv, w_kr, w_rest, gcq, gckv, ctab, stab, ctab_t, stab_t,
             wqt, wk, wvt, gqc, gqsc, gk, gks, tl):
    ones = np.zeros((MLA_HEADS, V_ROWS, 1), np.float32)
    ones[:, V_DIM, 0] = 1.0
    ones = jnp.asarray(ones.reshape(MLA_HEADS * V_ROWS, 1))
    B, L, _ = x.shape
    n_rest = w_rest.shape[1]
    tab = pl.BlockSpec((tl, LANES), lambda b, i: (i, 0))
    tab_t = pl.BlockSpec((QK_DIM, tl), lambda b, i: (0, i))
    consts = (gn, wt_hy, w_cq, w_ckv, w_kr, w_rest, gcq, gckv)
    qkv_consts = (wqt, wk, wvt, gqc, gqsc, gk, gks, ones)
    return pl.pallas_call(
        _in_proj_kernel,
        grid=(B, L // tl),
        in_specs=[pl.BlockSpec((1, tl, D_MODEL), lambda b, i: (b, i, 0))]
        + [_resident(a.shape) for a in consts] + [tab, tab, tab_t, tab_t] + [_resident(a.shape) for a in qkv_consts],
        out_specs=[
            pl.BlockSpec((1, 4 * HY_WIDTH, tl), lambda b, i: (b, 0, i)),
            pl.BlockSpec((1, tl, ATTN_WIDTH), lambda b, i: (b, i, 0)),
            pl.BlockSpec((1, tl, 2 * D_MODEL), lambda b, i: (b, i, 0)),
            pl.BlockSpec((1, MLA_HEADS, QK_DIM, tl), lambda b, i: (b, 0, 0, i)),
            pl.BlockSpec((1, MLA_HEADS, tl, QK_DIM), lambda b, i: (b, 0, i, 0)),
            pl.BlockSpec((1, MLA_HEADS, V_ROWS, tl), lambda b, i: (b, 0, 0, i)),
        ],
        out_shape=[
            jax.ShapeDtypeStruct((B, 4 * HY_WIDTH, L), BF16),
            jax.ShapeDtypeStruct((B, L, ATTN_WIDTH), BF16),
            jax.ShapeDtypeStruct((B, L, n_rest - ATTN_WIDTH), BF16),
            jax.ShapeDtypeStruct((B, MLA_HEADS, QK_DIM, L), BF16),
            jax.ShapeDtypeStruct((B, MLA_HEADS, L, QK_DIM), BF16),
            jax.ShapeDtypeStruct((B, MLA_HEADS, V_ROWS, L), BF16),
        ],
        compiler_params=_cparams(("parallel", "parallel")),
        name="in_proj",
    )(x, *consts, ctab, stab, ctab_t, stab_t, *qkv_consts)


def _fmlp_kernel(zt_ref, w1t_ref, b1_ref, f1_ref, w2t_ref, b2_ref, f2_ref, out_ref):
    hp = lax.Precision.HIGHEST
    h = jnp.sin(f1_ref[...] * (jnp.dot(w1t_ref[...], zt_ref[0], precision=hp, preferred_element_type=F32) + b1_ref[...]))
    out_ref[0] = jnp.sin(f2_ref[...] * (jnp.dot(w2t_ref[...], h, precision=hp, preferred_element_type=F32) + b2_ref[...]))


def _fmlp(zt, w1t, b1, f1, w2t, b2, f2):
    _, _, L = zt.shape
    return pl.pallas_call(
        _fmlp_kernel,
        grid=(2,),
        in_specs=[pl.BlockSpec((1, LANES, L), lambda d: (d, 0, 0)), _resident(w1t.shape), _resident(b1.shape),
                  _resident(f1.shape), _resident(w2t.shape), _resident(b2.shape), _resident(f2.shape)],
        out_specs=pl.BlockSpec((1, FILTER_HIDDEN, L), lambda d: (d, 0, 0)),
        out_shape=jax.ShapeDtypeStruct((2, FILTER_HIDDEN, L), F32),
        compiler_params=_cparams(("parallel",)),
        name="fmlp",
    )(zt, w1t, b1, f1, w2t, b2, f2)


def _stage_a_forward(load, g_scr, tw_ref, n1, rows_total):
    def body(r, carry):
        rs = pl.ds(pl.multiple_of(r * SUBLANES, SUBLANES), SUBLANES)
        for half in range(DFT_LANE // LANES):
            ls = slice(half * LANES, (half + 1) * LANES)
            li = slice(DFT_LANE + half * LANES, DFT_LANE + (half + 1) * LANES)
            xs = [load(i, rs, ls) for i in range(n1)]
            xs = [(None, None) if x is None else x for x in xs]
            X = _fft_dif(xs, -1.0)
            for k1 in range(n1):
                if tw_ref is None:
                    yr, yi = X[k1]
                else:
                    yr, yi = _cmul(X[k1], (tw_ref[k1, 0, :, ls], tw_ref[k1, 1, :, ls]))
                g_scr[k1, rs, ls] = yr
                g_scr[k1, rs, li] = yi
        return carry

    lax.fori_loop(0, rows_total // SUBLANES, body, 0)


def _fspec_kernel(h_ref, t_ref, w3f_ref, w3b_ref, dl_ref, wf_ref, tw_ref, out_ref, k_scr, g_scr,
                  *, L, n1, rows, group):
    dl = dl_ref[...]
    for d, w_ref in enumerate((w3f_ref, w3b_ref)):
        k = jnp.dot(w_ref[...], h_ref[d], precision=lax.Precision.HIGHEST, preferred_element_type=F32)
        k_scr[:, d * L:(d + 1) * L] = k * jnp.exp(-(dl * t_ref[d]))

    def load(i, rs, ls):
        return (k_scr[rs, i * DFT_LANE + ls.start:i * DFT_LANE + ls.stop], None)

    _stage_a_forward(load, g_scr, tw_ref, n1, rows)
    inv_n = 1.0 / (n1 * DFT_LANE)
    w = wf_ref[...]
    w_hi = w.astype(BF16)
    w_lo = (w - w_hi.astype(F32)).astype(BF16)
    for g0 in range(0, n1, group):
        lhs = g_scr[g0:g0 + group].reshape(group * rows, 2 * DFT_LANE)
        l_hi = lhs.astype(BF16)
        l_lo = (lhs - l_hi.astype(F32)).astype(BF16)
        res = (jnp.dot(l_hi, w_hi, preferred_element_type=F32) + jnp.dot(l_hi, w_lo, preferred_element_type=F32)
               + jnp.dot(l_lo, w_hi, preferred_element_type=F32))
        out_ref[0, g0:g0 + group] = res.reshape(group, rows, 2 * DFT_LANE) * inv_n


def _fspec(h2t, t2, w3f, w3b, dl, wf, tw, L, n1, rows, group):
    n_rows = w3f.shape[0]
    tiles_per_order = HY_WIDTH // rows
    row_spec = lambda shape: pl.BlockSpec(shape, lambda i: (i, 0))
    return pl.pallas_call(
        functools.partial(_fspec_kernel, L=L, n1=n1, rows=rows, group=group),
        grid=(n_rows // rows,),
        in_specs=[_resident(h2t.shape), _resident(t2.shape),
                  row_spec((rows, FILTER_HIDDEN)), row_spec((rows, FILTER_HIDDEN)), row_spec((rows, 1)),
                  _resident(wf.shape), _resident(tw.shape)],
        out_specs=pl.BlockSpec((1, n1, rows, 2 * DFT_LANE),
                               lambda i: (i // tiles_per_order, 0, i % tiles_per_order, 0)),
        out_shape=jax.ShapeDtypeStruct((n_rows // HY_WIDTH, n1, HY_WIDTH, 2 * DFT_LANE), F32),
        scratch_shapes=[pltpu.VMEM((rows, 2 * L), F32), pltpu.VMEM((n1, rows, 2 * DFT_LANE), F32)],
        compiler_params=_cparams(("parallel",)),
        name="fspec",
    )(h2t, t2, w3f, w3b, dl, wf, tw)


def _hyena_kernel(hy_ref, wsh_ref, bsh_ref, ksp_ref, bias_ref, wf_ref, wi_ref, tw_ref, out_ref,
                  u_scr, x1_scr, x2_scr, g_scr, *, L, n1, ct, group):
    lane = lax.broadcasted_iota(jnp.int32, (ct, LANES), 1)

    def short_conv(g, dst):
        for bi in range(2):
            p = hy_ref[bi, g].astype(F32)
            left = pltpu.roll(p, 1, 1)
            left = jnp.concatenate([jnp.where(lane == 0, 0.0, left[:, :LANES]), left[:, LANES:]], axis=1)
            right = pltpu.roll(p, L - 1, 1)
            right = jnp.concatenate([right[:, :L - LANES], jnp.where(lane == LANES - 1, 0.0, right[:, L - LANES:])],
                                    axis=1)
            conv = wsh_ref[g, 0] * left + wsh_ref[g, 1] * p + wsh_ref[g, 2] * right + bsh_ref[g]
            if g == 2:
                z = hy_ref[bi, 3].astype(F32)
                conv = conv * (z * jax.nn.sigmoid(z))
            dst[bi] = conv

    short_conv(0, u_scr)
    gate_scr = (x1_scr, x2_scr)

    half_blocks = n1 // 2
    wf = wf_ref[...]
    wi = wi_ref[...]
    re, im = slice(0, DFT_LANE), slice(DFT_LANE, 2 * DFT_LANE)

    def mxu_dft(x, w):
        res = jnp.dot(x.reshape(group * ct, 2 * DFT_LANE).astype(BF16), w, preferred_element_type=F32)
        res = res.reshape(group, ct, 2 * DFT_LANE)
        return res[..., re], res[..., im]

    for order in range(2):
        def load(i, rs, ls):
            if i >= half_blocks:
                return None
            cs = slice(i * DFT_LANE + ls.start, i * DFT_LANE + ls.stop)
            return (u_scr[0, rs, cs], u_scr[1, rs, cs])

        _stage_a_forward(load, g_scr, None, n1, ct)

        def twiddled(gs):
            g = g_scr[gs]
            gr, gi = g[..., re], g[..., im]
            twr = tw_ref[gs, 0, 0:1, :]
            twi = tw_ref[gs, 1, 0:1, :]
            return jnp.concatenate([gr * twr - gi * twi, gr * twi + gi * twr], axis=-1)

        def untwiddle(gs, q):
            qr, qi = q
            twr = tw_ref[gs, 0, 0:1, :]
            twi = tw_ref[gs, 1, 0:1, :]
            g_scr[gs, :, re] = qr * twr + qi * twi
            g_scr[gs, :, im] = qi * twr - qr * twi

        groups = [slice(g0, g0 + group) for g0 in range(0, n1, group)]
        spectra = [mxu_dft(twiddled(gs), wf) for gs in groups]
        short_conv(order + 1, gate_scr[order])
        prev = None
        for gs, (hr, hi) in zip(groups, spectra):
            kr, ki = ksp_ref[order, gs, :, re], ksp_ref[order, gs, :, im]
            q = mxu_dft(jnp.concatenate([hr * kr - hi * ki, hr * ki + hi * kr], axis=-1), wi)
            if prev is not None:
                untwiddle(*prev)
            prev = (gs, q)
        untwiddle(*prev)

        def body(r, carry, order=order):
            rs = pl.ds(pl.multiple_of(r * SUBLANES, SUBLANES), SUBLANES)
            bias = bias_ref[order, rs, :]
            for half in range(DFT_LANE // LANES):
                ls = slice(half * LANES, (half + 1) * LANES)
                li = slice(DFT_LANE + half * LANES, DFT_LANE + (half + 1) * LANES)
                q = [(g_scr[k1, rs, ls], g_scr[k1, rs, li]) for k1 in range(n1)]
                ev = _fft_dif(q[0::2], 1.0)
                od = _fft_dif(q[1::2], 1.0)
                for i in range(half_blocks):
                    y = _cadd(ev[i], _cmulc(od[i], np.exp(2j * np.pi * i / n1)))
                    cs = slice(i * DFT_LANE + ls.start, i * DFT_LANE + ls.stop)
                    for bi in range(2):
                        u = u_scr[bi, rs, cs]
                        conv = y[bi] + bias * u
                        if order == 0:
                            u_scr[bi, rs, cs] = x1_scr[bi, rs, cs] * conv
                        else:
                            out_ref[bi, rs, cs] = x2_scr[bi, rs, cs] * conv
            return carry

        lax.fori_loop(0, ct // SUBLANES, body, 0)


def _hyena(hyt4, wsh, bsh, kspec, bias, wf, wi, tw, n1, ct, group):
    B, _, C, L = hyt4.shape
    return pl.pallas_call(
        functools.partial(_hyena_kernel, L=L, n1=n1, ct=ct, group=group),
        grid=(C // ct, B // 2),
        in_specs=[
            pl.BlockSpec((2, 4, ct, L), lambda c, p: (p, 0, c, 0)),
            pl.BlockSpec((3, 3, ct, 1), lambda c, p: (0, 0, c, 0)),
            pl.BlockSpec((3, ct, 1), lambda c, p: (0, c, 0)),
            pl.BlockSpec((2, n1, ct, 2 * DFT_LANE), lambda c, p: (0, 0, c, 0), pipeline_mode=pl.Buffered(1)),
            pl.BlockSpec((2, ct, 1), lambda c, p: (0, c, 0)),
            _resident(wf.shape), _resident(wi.shape), _resident(tw.shape),
        ],
        out_specs=pl.BlockSpec((2, ct, L), lambda c, p: (p, c, 0)),
        out_shape=jax.ShapeDtypeStruct((B, C, L), F32),
        scratch_shapes=[pltpu.VMEM((2, ct, L), F32), pltpu.VMEM((2, ct, L), F32), pltpu.VMEM((2, ct, L), F32),
                        pltpu.VMEM((n1, ct, 2 * DFT_LANE), F32)],
        compiler_params=_cparams(("parallel", "parallel")),
        name="hyena",
    )(hyt4, wsh, bsh, kspec, bias, wf, wi, tw)


def _attn_kernel(qt_ref, k_ref, vt_ref, o_ref, *, L, tq):
    units = [(j, t0) for j in range(2) for t0 in range(0, tq, Q_TILE)]
    starts = list(range(0, L, KV_CHUNK))

    def scores(u, c0):
        j, t0 = u
        return jnp.dot(k_ref[0, j, c0:c0 + KV_CHUNK, :], qt_ref[0, j, :, t0:t0 + Q_TILE],
                       preferred_element_type=F32)

    m = {u: None for u in units}
    acc = {u: None for u in units}
    cur = {u: scores(u, starts[0]) for u in units}
    for i, c0 in enumerate(starts):
        nxt = {}
        for u in units:
            if i + 1 < len(starts):
                nxt[u] = scores(u, starts[i + 1])
            s = cur[u]
            mc = jnp.max(s, axis=0, keepdims=True)
            m_new = mc if m[u] is None else jnp.maximum(m[u], mc)
            p = jnp.exp2(s - m_new).astype(BF16)
            pv = jnp.dot(vt_ref[0, u[0], :, c0:c0 + KV_CHUNK], p, preferred_element_type=F32)
            acc[u] = pv if m[u] is None else jnp.exp2(m[u] - m_new) * acc[u] + pv
            m[u] = m_new
        cur = nxt
    for j, t0 in units:
        a = acc[(j, t0)]
        o_ref[0, j * V_DIM:(j + 1) * V_DIM, t0:t0 + Q_TILE] = (a[:V_DIM] / a[V_DIM:V_DIM + 1]).astype(BF16)


def _attn(qt, k, vt, tq):
    B, _, L, _ = k.shape
    return pl.pallas_call(
        functools.partial(_attn_kernel, L=L, tq=tq),
        grid=(B, HEAD_PAIRS, L // tq),
        in_specs=[pl.BlockSpec((1, 2, QK_DIM, tq), lambda b, p, i: (b, p, 0, i)),
                  pl.BlockSpec((1, 2, L, QK_DIM), lambda b, p, i: (b, p, 0, 0)),
                  pl.BlockSpec((1, 2, V_ROWS, L), lambda b, p, i: (b, p, 0, 0))],
        out_specs=pl.BlockSpec((1, 2 * V_DIM, tq), lambda b, p, i: (b, p, i)),
        out_shape=jax.ShapeDtypeStruct((B, ATTN_WIDTH, L), BF16),
        compiler_params=_cparams(("parallel", "parallel", "parallel")),
        name="attn",
    )(qt, k, vt)


def _final_kernel(x_ref, yt_ref, at_ref, zat_ref, gate_ref, bg_ref, why_ref, wat_ref, wout_ref, o_ref):
    yt = yt_ref[0]
    y = yt.T.astype(BF16)
    u_hy = jnp.dot(y, why_ref[...], preferred_element_type=F32)
    z = zat_ref[0].astype(F32)
    a = (at_ref[0].astype(F32).T * (z * jax.nn.sigmoid(z))).astype(BF16)
    u_at = jnp.dot(a, wat_ref[...], preferred_element_type=F32)
    gates = jax.nn.sigmoid(gate_ref[0].astype(F32) + bg_ref[...])
    merged = gates[:, :D_MODEL] * u_hy + gates[:, D_MODEL:] * u_at
    o_ref[0] = x_ref[0] + jnp.dot(merged.astype(BF16), wout_ref[...], preferred_element_type=F32)


def _final(x, yt, at, zat, gate, bg, why, wat, wout, tl):
    B, L, _ = x.shape
    tok = lambda w: pl.BlockSpec((1, tl, w), lambda b, i: (b, i, 0))
    return pl.pallas_call(
        _final_kernel,
        grid=(B, L // tl),
        in_specs=[tok(D_MODEL), pl.BlockSpec((1, HY_WIDTH, tl), lambda b, i: (b, 0, i)),
                  pl.BlockSpec((1, ATTN_WIDTH, tl), lambda b, i: (b, 0, i)), tok(ATTN_WIDTH), tok(2 * D_MODEL),
                  _resident(bg.shape), _resident(why.shape), _resident(wat.shape), _resident(wout.shape)],
        out_specs=tok(D_MODEL),
        out_shape=jax.ShapeDtypeStruct((B, L, D_MODEL), F32),
        compiler_params=_cparams(("parallel", "parallel")),
        name="final",
    )(x, yt, at, zat, gate, bg, why, wat, wout)


def _layer(x, g_norm, w_in, b_gate, w_short, b_short, w_f1, b_f1, freq_1, w_f2, b_f2, freq_2, w_f3,
           hy_bias, w_hy_out, g_cq, w_uq, g_ckv, w_ukv, g_qn, g_kn, w_attn_out, w_out):
    B, L, _ = x.shape
    n = 2 * L
    n1 = n // DFT_LANE
    assert B % 2 == 0 and n1 * DFT_LANE == n and n1 >= 2 and (n1 & (n1 - 1)) == 0
    tl = min(512, L)
    ct = 64
    group = min(8, n1)

    o_z, o_cq, o_ckv, o_kr, o_za = 3 * HY_WIDTH, 4 * HY_WIDTH, 4 * HY_WIDTH + Q_LORA, 4 * HY_WIDTH + Q_LORA + KV_LORA, \
        4 * HY_WIDTH + Q_LORA + KV_LORA + QK_ROPE
    del o_z
    w_in = w_in.astype(BF16)
    wt_hy = w_in[:, :4 * HY_WIDTH].T
    w_cq = w_in[:, o_cq:o_ckv]
    w_ckv = w_in[:, o_ckv:o_kr]
    half = QK_ROPE // 2
    w_rope = w_in[:, o_kr:o_za]
    w_rope_sw = jnp.concatenate([w_rope[:, half:], w_rope[:, :half]], axis=1)
    zl, zr = jnp.zeros((D_MODEL, QK_NOPE), BF16), jnp.zeros((D_MODEL, LANES - QK_DIM), BF16)
    w_kr = jnp.concatenate([zl, w_rope, zr, zl, w_rope_sw, zr], axis=1)
    w_rest = w_in[:, o_za:]

    perm = np.concatenate([np.arange(QK_NOPE), QK_NOPE + half + np.arange(half), QK_NOPE + np.arange(half)])
    wqt = w_uq.T.astype(BF16)
    wkv = w_ukv.reshape(KV_LORA, MLA_HEADS, QK_NOPE + V_DIM)
    wk = jnp.pad(wkv[:, :, :QK_NOPE], ((0, 0), (0, 0), (0, LANES - QK_NOPE)))
    wk = wk.reshape(KV_LORA, MLA_HEADS * LANES).astype(BF16)
    wvt = jnp.pad(wkv[:, :, QK_NOPE:], ((0, 0), (0, 0), (0, V_ROWS - V_DIM)))
    wvt = wvt.reshape(KV_LORA, MLA_HEADS * V_ROWS).T.astype(BF16)
    pad_lanes = lambda g: jnp.pad(g, ((0, 0), (0, LANES - QK_DIM)))
    gqc = g_qn.reshape(QK_DIM, 1)
    gqsc = gqc[perm]
    gk = g_kn.reshape(1, QK_DIM)
    gks = pad_lanes(gk[:, perm])
    gk = pad_lanes(gk)

    ctab, stab = _rope_tables(L)
    pad_tab = lambda tab: np.pad(tab, ((0, 0), (0, LANES - QK_DIM)))
    hyt, zat, gate, qt, k, vt = _in_proj(
        x, g_norm.reshape(1, D_MODEL), wt_hy, w_cq, w_ckv, w_kr, w_rest, g_cq.reshape(1, Q_LORA),
        g_ckv.reshape(1, KV_LORA), jnp.asarray(pad_tab(ctab)), jnp.asarray(pad_tab(stab)),
        jnp.asarray(np.ascontiguousarray(ctab.T)), jnp.asarray(np.ascontiguousarray(stab.T)),
        wqt, wk, wvt, gqc, gqsc, gk, gks, tl)

    zfeat, t = _filter_features(L)
    w1 = jnp.concatenate([w_f1, jnp.zeros((LANES - FILTER_EMB, FILTER_HIDDEN), F32)], axis=0)
    w3 = w_f3.reshape(FILTER_HIDDEN, 2, 2, HY_WIDTH)
    w3f = w3[:, :, 0].reshape(FILTER_HIDDEN, 2 * HY_WIDTH).T
    w3b = w3[:, :, 1].reshape(FILTER_HIDDEN, 2 * HY_WIDTH).T
    deltas = np.abs(np.linspace(MIN_DECAY, MAX_DECAY, HY_WIDTH))
    dl = np.tile(deltas, 2).reshape(2 * HY_WIDTH, 1).astype(np.float32)
    col = lambda v: v.reshape(FILTER_HIDDEN, 1)
    zt = np.ascontiguousarray(np.stack([zfeat.T, zfeat[::-1].T]))
    t2 = np.ascontiguousarray(np.stack([t, t[::-1]]).reshape(2, 1, L))
    h2t = _fmlp(jnp.asarray(zt), w1.T, col(b_f1), col(freq_1), w_f2.T, col(b_f2), col(freq_2))
    wf32, wi32 = (jnp.asarray(w) for w in _dft_mats())
    tw = jnp.asarray(_twiddle(n1))
    kspec = _fspec(h2t, jnp.asarray(t2), w3f, w3b, jnp.asarray(dl), wf32, tw, L, n1, ct, group)

    wsh = w_short.reshape(3, 3, HY_WIDTH).transpose(1, 0, 2)[..., None]
    bsh = b_short.reshape(3, HY_WIDTH, 1)
    yt = _hyena(hyt.reshape(B, 4, HY_WIDTH, L), wsh, bsh, kspec, hy_bias.reshape(2, HY_WIDTH, 1),
                wf32.astype(BF16), wi32.astype(BF16), tw, n1, ct, group)

    at = _attn(qt, k, vt, min(1024, L))

    return _final(x, yt, at, zat, gate, b_gate.reshape(1, 2 * D_MODEL), w_hy_out.astype(BF16),
                  w_attn_out.astype(BF16), w_out.astype(BF16), tl)


def kernel(x, g_norm, w_in, b_gate, w_short, b_short, w_f1, b_f1, freq_1, w_f2, b_f2, freq_2, w_f3, hy_bias, w_hy_out, g_cq, w_uq, g_ckv, w_ukv, g_qn, g_kn, w_attn_out, w_out):
    depth = g_norm.shape[0]
    for l in range(depth):
        x = _layer(x, g_norm[l], w_in[l], b_gate[l], w_short[l], b_short[l], w_f1[l], b_f1[l], freq_1[l], w_f2[l],
                   b_f2[l], freq_2[l], w_f3[l], hy_bias[l], w_hy_out[l], g_cq[l], w_uq[l], g_ckv[l], w_ukv[l],
                   g_qn[l], g_kn[l], w_attn_out[l], w_out[l])
    return x
```

```python
import functools
import math

import numpy as np
import jax
import jax.numpy as jnp
from jax import lax
from jax.experimental import pallas as pl
from jax.experimental.pallas import tpu as pltpu

D_MODEL = 1024
HY_WIDTH = 512
FILTER_EMB = 33
FILTER_HIDDEN = 64
DECAY_TARGET = 1e-2
FAST_DECAY = 0.3
SLOW_DECAY = 1.5
MIN_DECAY = math.log(DECAY_TARGET) / SLOW_DECAY
MAX_DECAY = math.log(DECAY_TARGET) / FAST_DECAY
MLA_HEADS = 8
HEAD_PAIRS = MLA_HEADS // 2
QK_NOPE = 64
QK_ROPE = 32
QK_DIM = QK_NOPE + QK_ROPE
V_DIM = 64
Q_LORA = 384
KV_LORA = 256
ATTN_WIDTH = MLA_HEADS * V_DIM
ROPE_THETA = 10000.0
EPS = 1e-6

LANES = 128
SUBLANES = 8
DFT_LANE = 256
V_ROWS = 80
KV_CHUNK = 256
Q_TILE = 256
VMEM_LIMIT = 56 * 1024 * 1024

F32 = jnp.float32
BF16 = jnp.bfloat16
_NT = (((1,), (1,)), ((), ()))
_NN = (((1,), (0,)), ((), ()))


def _cparams(sem):
    return pltpu.CompilerParams(dimension_semantics=sem, vmem_limit_bytes=VMEM_LIMIT)


def _resident(shape):
    nd = len(shape)
    return pl.BlockSpec(shape, lambda *_: (0,) * nd, pipeline_mode=pl.Buffered(1))


def _add(a, b):
    if a is None:
        return b
    if b is None:
        return a
    return a + b


def _sub(a, b):
    if b is None:
        return a
    if a is None:
        return -b
    return a - b


def _cadd(x, y):
    return (_add(x[0], y[0]), _add(x[1], y[1]))


def _csub(x, y):
    return (_sub(x[0], y[0]), _sub(x[1], y[1]))


def _scale(a, s):
    if a is None or s == 0.0:
        return None
    if s == 1.0:
        return a
    if s == -1.0:
        return -a
    return a * s


def _cmulc(x, w):
    wr, wi = float(np.real(w)), float(np.imag(w))
    if abs(wr) < 1e-15:
        wr = 0.0
    if abs(wi) < 1e-15:
        wi = 0.0
    xr, xi = x
    if wr != 0.0 and abs(abs(wr) - abs(wi)) < 1e-12:
        sr, si, c = np.sign(wr), np.sign(wi), abs(wr)
        return (_scale(_sub(_scale(xr, sr), _scale(xi, si)), c), _scale(_add(_scale(xr, si), _scale(xi, sr)), c))
    re = _sub(_scale(xr, wr), _scale(xi, wi))
    im = _add(_scale(xr, wi), _scale(xi, wr))
    return (re, im)


def _cmul(x, y):
    xr, xi = x
    yr, yi = y
    re = _sub(None if xr is None else xr * yr, None if xi is None else xi * yi)
    im = _add(None if xr is None else xr * yi, None if xi is None else xi * yr)
    return (re, im)


def _fft_dif(xs, sign):
    n = len(xs)
    if n == 1:
        return xs
    half = n // 2
    a = [_cadd(xs[i], xs[i + half]) for i in range(half)]
    b = [_cmulc(_csub(xs[i], xs[i + half]), np.exp(sign * 2j * np.pi * i / n)) for i in range(half)]
    ev = _fft_dif(a, sign)
    od = _fft_dif(b, sign)
    out = [None] * n
    out[0::2] = ev
    out[1::2] = od
    return out


def _dft_mats():
    n = np.arange(DFT_LANE)
    ang = 2.0 * np.pi * np.outer(n, n) / DFT_LANE
    fr, fi = np.cos(ang), -np.sin(ang)
    fwd = np.block([[fr, fi], [-fi, fr]]).astype(np.float32)
    inv = np.block([[fr, -fi], [fi, fr]]).astype(np.float32)
    return fwd, inv


def _twiddle(n1):
    n = n1 * DFT_LANE
    ang = 2.0 * np.pi * np.outer(np.arange(n1), np.arange(DFT_LANE)) / n
    tw = np.stack([np.cos(ang), -np.sin(ang)], axis=1)
    return np.ascontiguousarray(np.broadcast_to(tw[:, :, None, :], (n1, 2, SUBLANES, DFT_LANE))).astype(np.float32)


def _filter_features(L):
    t = np.linspace(0.0, 1.0, L)[:, None]
    bands = (FILTER_EMB - 1) // 2
    f = np.linspace(1e-4, bands - 1, bands)
    ang = (2.0 * np.pi / L) * np.arange(L)[:, None] * f[None, :]
    z = np.concatenate([t, np.cos(ang), -np.sin(ang)], axis=-1)
    zp = np.zeros((L, LANES), np.float32)
    zp[:, :FILTER_EMB] = z
    return zp, t[:, 0].astype(np.float32)


def _rope_tables(L):
    pos = np.arange(L, dtype=np.float64)
    inv_freq = ROPE_THETA ** (-np.arange(0, QK_ROPE, 2, dtype=np.float64) / QK_ROPE)
    ang = pos[:, None] * inv_freq[None, :]
    c, s = np.cos(ang), np.sin(ang)
    ctab = np.concatenate([np.ones((L, QK_NOPE)), c, c], axis=1)
    stab = np.concatenate([np.zeros((L, QK_NOPE)), -s, s], axis=1)
    return ctab.astype(np.float32), stab.astype(np.float32)


def _in_proj_kernel(x_ref, gn_ref, wt_hy_ref, w_cq_ref, w_ckv_ref, w_rest_ref, gcq_ref, gckv_ref,
                    ct_ref, st_ref, ctt_ref, stt_ref, wqt_ref, wk_ref, wvt_ref,
                    gqc_ref, gqsc_ref, gk_ref, gks_ref, ones_ref,
                    hyt_ref, zat_ref, gate_ref, qt_ref, k_ref, vt_ref):
    x = x_ref[0]
    ms = jnp.mean(x * x, axis=-1, keepdims=True)
    h = (x * lax.rsqrt(ms + EPS) * gn_ref[...]).astype(BF16)

    def latent(c, g_ref):
        r = lax.rsqrt(jnp.mean(c * c, axis=-1, keepdims=True) + EPS)
        return (c * r * g_ref[...]).astype(BF16)

    cqr = jnp.dot(h, w_cq_ref[...], preferred_element_type=F32)
    cq = latent(cqr[:, :Q_LORA], gcq_ref)
    ckv = latent(jnp.dot(h, w_ckv_ref[...], preferred_element_type=F32), gckv_ref)
    kr_a = cqr[:, Q_LORA:]
    qt_all = lax.dot_general(wqt_ref[...], cq, _NT, preferred_element_type=F32)
    ka_all = jnp.dot(ckv, wk_ref[...], preferred_element_type=F32)
    vt_all = lax.dot_general(wvt_ref[...], ckv, _NT, preferred_element_type=F32) + ones_ref[...]

    hyt_ref[0] = lax.dot_general(wt_hy_ref[...], h, _NT, preferred_element_type=F32).astype(BF16)
    rest = jnp.dot(h, w_rest_ref[...], preferred_element_type=F32)
    zat_ref[0] = rest[:, :ATTN_WIDTH].astype(BF16)
    gate_ref[0] = rest[:, ATTN_WIDTH:].astype(BF16)

    half = QK_ROPE // 2
    scale = math.log2(math.e) / math.sqrt(QK_DIM)
    qc, qs = gqc_ref[...] * ctt_ref[...], gqsc_ref[...] * stt_ref[...]
    kc, ks = gk_ref[...] * ct_ref[...], gks_ref[...] * st_ref[...]
    lane = lax.broadcasted_iota(jnp.int32, kr_a.shape, 1)
    kr_w = jnp.where(lane < QK_NOPE + half, pltpu.roll(kr_a, LANES - half, 1), pltpu.roll(kr_a, half, 1))
    for hd in range(MLA_HEADS):
        qa = qt_all[hd * QK_DIM:(hd + 1) * QK_DIM]
        qw = jnp.concatenate([qa[:QK_NOPE], qa[QK_NOPE + half:], qa[QK_NOPE:QK_NOPE + half]], axis=0)
        r = lax.rsqrt(jnp.mean(qa * qa, axis=0, keepdims=True) + EPS) * scale
        qt_ref[0, hd] = ((qa * qc + qw * qs) * r).astype(BF16)
        ka = ka_all[:, hd * LANES:(hd + 1) * LANES] + kr_a
        rk = lax.rsqrt(jnp.sum(ka * ka, axis=-1, keepdims=True) * (1.0 / QK_DIM) + EPS)
        k_ref[0, hd] = ((ka * kc + kr_w * ks) * rk)[:, :QK_DIM].astype(BF16)
        vt_ref[0, hd] = vt_all[hd * V_ROWS:(hd + 1) * V_ROWS].astype(BF16)


def _in_proj(x, gn, wt_hy, w_cq, w_ckv, w_rest, gcq, gckv, ctab, stab, ctab_t, stab_t,
             wqt, wk, wvt, gqc, gqsc, gk, gks, tl):
    ones = np.zeros((MLA_HEADS, V_ROWS, 1), np.float32)
    ones[:, V_DIM, 0] = 1.0
    ones = jnp.asarray(ones.reshape(MLA_HEADS * V_ROWS, 1))
    B, L, _ = x.shape
    n_rest = w_rest.shape[1]
    tab = pl.BlockSpec((tl, LANES), lambda b, i: (i, 0))
    tab_t = pl.BlockSpec((QK_DIM, tl), lambda b, i: (0, i))
    consts = (gn, wt_hy, w_cq, w_ckv, w_rest, gcq, gckv)
    qkv_consts = (wqt, wk, wvt, gqc, gqsc, gk, gks, ones)
    return pl.pallas_call(
        _in_proj_kernel,
        grid=(B, L // tl),
        in_specs=[pl.BlockSpec((1, tl, D_MODEL), lambda b, i: (b, i, 0))]
        + [_resident(a.shape) for a in consts] + [tab, tab, tab_t, tab_t] + [_resident(a.shape) for a in qkv_consts],
        out_specs=[
            pl.BlockSpec((1, 4 * HY_WIDTH, tl), lambda b, i: (b, 0, i)),
            pl.BlockSpec((1, tl, ATTN_WIDTH), lambda b, i: (b, i, 0)),
            pl.BlockSpec((1, tl, 2 * D_MODEL), lambda b, i: (b, i, 0)),
            pl.BlockSpec((1, MLA_HEADS, QK_DIM, tl), lambda b, i: (b, 0, 0, i)),
            pl.BlockSpec((1, MLA_HEADS, tl, QK_DIM), lambda b, i: (b, 0, i, 0)),
            pl.BlockSpec((1, MLA_HEADS, V_ROWS, tl), lambda b, i: (b, 0, 0, i)),
        ],
        out_shape=[
            jax.ShapeDtypeStruct((B, 4 * HY_WIDTH, L), BF16),
            jax.ShapeDtypeStruct((B, L, ATTN_WIDTH), BF16),
            jax.ShapeDtypeStruct((B, L, n_rest - ATTN_WIDTH), BF16),
            jax.ShapeDtypeStruct((B, MLA_HEADS, QK_DIM, L), BF16),
            jax.ShapeDtypeStruct((B, MLA_HEADS, L, QK_DIM), BF16),
            jax.ShapeDtypeStruct((B, MLA_HEADS, V_ROWS, L), BF16),
        ],
        compiler_params=_cparams(("parallel", "parallel")),
        name="in_proj",
    )(x, *consts, ctab, stab, ctab_t, stab_t, *qkv_consts)


def _fmlp_kernel(zt_ref, w1t_ref, b1_ref, f1_ref, w2t_ref, b2_ref, f2_ref, out_ref):
    hp = lax.Precision.HIGHEST
    h = jnp.sin(f1_ref[...] * (jnp.dot(w1t_ref[...], zt_ref[0], precision=hp, preferred_element_type=F32) + b1_ref[...]))
    out_ref[0] = jnp.sin(f2_ref[...] * (jnp.dot(w2t_ref[...], h, precision=hp, preferred_element_type=F32) + b2_ref[...]))


def _fmlp(zt, w1t, b1, f1, w2t, b2, f2):
    _, _, L = zt.shape
    return pl.pallas_call(
        _fmlp_kernel,
        grid=(2,),
        in_specs=[pl.BlockSpec((1, LANES, L), lambda d: (d, 0, 0)), _resident(w1t.shape), _resident(b1.shape),
                  _resident(f1.shape), _resident(w2t.shape), _resident(b2.shape), _resident(f2.shape)],
        out_specs=pl.BlockSpec((1, FILTER_HIDDEN, L), lambda d: (d, 0, 0)),
        out_shape=jax.ShapeDtypeStruct((2, FILTER_HIDDEN, L), F32),
        compiler_params=_cparams(("parallel",)),
        name="fmlp",
    )(zt, w1t, b1, f1, w2t, b2, f2)


def _stage_a_forward(load, g_scr, tw_ref, n1, rows_total):
    def body(r, carry):
        rs = pl.ds(pl.multiple_of(r * SUBLANES, SUBLANES), SUBLANES)
        for half in range(DFT_LANE // LANES):
            ls = slice(half * LANES, (half + 1) * LANES)
            li = slice(DFT_LANE + half * LANES, DFT_LANE + (half + 1) * LANES)
            xs = [load(i, rs, ls) for i in range(n1)]
            xs = [(None, None) if x is None else x for x in xs]
            X = _fft_dif(xs, -1.0)
            for k1 in range(n1):
                if tw_ref is None:
                    yr, yi = X[k1]
                else:
                    yr, yi = _cmul(X[k1], (tw_ref[k1, 0, :, ls], tw_ref[k1, 1, :, ls]))
                g_scr[k1, rs, ls] = yr
                g_scr[k1, rs, li] = yi
        return carry

    lax.fori_loop(0, rows_total // SUBLANES, body, 0)


def _fspec_kernel(h_ref, t_ref, w3f_ref, w3b_ref, dl_ref, wf_ref, tw_ref, out_ref, k_scr, g_scr,
                  *, L, n1, rows, group):
    dl = dl_ref[...]
    for d, w_ref in enumerate((w3f_ref, w3b_ref)):
        k = jnp.dot(w_ref[...], h_ref[d], precision=lax.Precision.HIGHEST, preferred_element_type=F32)
        k_scr[:, d * L:(d + 1) * L] = k * jnp.exp(-(dl * t_ref[d]))

    def load(i, rs, ls):
        return (k_scr[rs, i * DFT_LANE + ls.start:i * DFT_LANE + ls.stop], None)

    _stage_a_forward(load, g_scr, tw_ref, n1, rows)
    inv_n = 1.0 / (n1 * DFT_LANE)
    w = wf_ref[...]
    for g0 in range(0, n1, group):
        lhs = g_scr[g0:g0 + group].reshape(group * rows, 2 * DFT_LANE).astype(BF16)
        res = jnp.dot(lhs, w, preferred_element_type=F32)
        out_ref[0, g0:g0 + group] = res.reshape(group, rows, 2 * DFT_LANE) * inv_n


def _fspec(h2t, t2, w3f, w3b, dl, wf, tw, L, n1, rows, group):
    n_rows = w3f.shape[0]
    tiles_per_order = HY_WIDTH // rows
    row_spec = lambda shape: pl.BlockSpec(shape, lambda i: (i, 0))
    return pl.pallas_call(
        functools.partial(_fspec_kernel, L=L, n1=n1, rows=rows, group=group),
        grid=(n_rows // rows,),
        in_specs=[_resident(h2t.shape), _resident(t2.shape),
                  row_spec((rows, FILTER_HIDDEN)), row_spec((rows, FILTER_HIDDEN)), row_spec((rows, 1)),
                  _resident(wf.shape), _resident(tw.shape)],
        out_specs=pl.BlockSpec((1, n1, rows, 2 * DFT_LANE),
                               lambda i: (i // tiles_per_order, 0, i % tiles_per_order, 0)),
        out_shape=jax.ShapeDtypeStruct((n_rows // HY_WIDTH, n1, HY_WIDTH, 2 * DFT_LANE), F32),
        scratch_shapes=[pltpu.VMEM((rows, 2 * L), F32), pltpu.VMEM((n1, rows, 2 * DFT_LANE), F32)],
        compiler_params=_cparams(("parallel",)),
        name="fspec",
    )(h2t, t2, w3f, w3b, dl, wf, tw)


def _hyena_kernel(hy_ref, wsh_ref, bsh_ref, ksp_ref, bias_ref, wf_ref, wi_ref, tw_ref, out_ref,
                  u_scr, x1_scr, x2_scr, g_scr, *, L, n1, ct, group):
    lane = lax.broadcasted_iota(jnp.int32, (ct, LANES), 1)

    def short_conv(g, dst):
        for bi in range(2):
            p = hy_ref[bi, g].astype(F32)
            left = pltpu.roll(p, 1, 1)
            left = jnp.concatenate([jnp.where(lane == 0, 0.0, left[:, :LANES]), left[:, LANES:]], axis=1)
            right = pltpu.roll(p, L - 1, 1)
            right = jnp.concatenate([right[:, :L - LANES], jnp.where(lane == LANES - 1, 0.0, right[:, L - LANES:])],
                                    axis=1)
            conv = wsh_ref[g, 0] * left + wsh_ref[g, 1] * p + wsh_ref[g, 2] * right + bsh_ref[g]
            if g == 2:
                z = hy_ref[bi, 3].astype(F32)
                conv = conv * (z * jax.nn.sigmoid(z))
            dst[bi] = conv

    short_conv(0, u_scr)
    gate_scr = (x1_scr, x2_scr)

    half_blocks = n1 // 2
    wf = wf_ref[...]
    wi = wi_ref[...]
    re, im = slice(0, DFT_LANE), slice(DFT_LANE, 2 * DFT_LANE)

    def mxu_dft(x, w):
        res = jnp.dot(x.reshape(group * ct, 2 * DFT_LANE).astype(BF16), w, preferred_element_type=F32)
        res = res.reshape(group, ct, 2 * DFT_LANE)
        return res[..., re], res[..., im]

    for order in range(2):
        def load(i, rs, ls):
            if i >= half_blocks:
                return None
            cs = slice(i * DFT_LANE + ls.start, i * DFT_LANE + ls.stop)
            return (u_scr[0, rs, cs], u_scr[1, rs, cs])

        _stage_a_forward(load, g_scr, None, n1, ct)

        def twiddled(gs):
            g = g_scr[gs]
            gr, gi = g[..., re], g[..., im]
            twr = tw_ref[gs, 0, 0:1, :]
            twi = tw_ref[gs, 1, 0:1, :]
            return jnp.concatenate([gr * twr - gi * twi, gr * twi + gi * twr], axis=-1)

        def untwiddle(gs, q):
            qr, qi = q
            twr = tw_ref[gs, 0, 0:1, :]
            twi = tw_ref[gs, 1, 0:1, :]
            g_scr[gs, :, re] = qr * twr + qi * twi
            g_scr[gs, :, im] = qi * twr - qr * twi

        groups = [slice(g0, g0 + group) for g0 in range(0, n1, group)]
        spectra = [mxu_dft(twiddled(gs), wf) for gs in groups]
        short_conv(order + 1, gate_scr[order])
        prev = None
        for gs, (hr, hi) in zip(groups, spectra):
            kr, ki = ksp_ref[order, gs, :, re], ksp_ref[order, gs, :, im]
            q = mxu_dft(jnp.concatenate([hr * kr - hi * ki, hr * ki + hi * kr], axis=-1), wi)
            if prev is not None:
                untwiddle(*prev)
            prev = (gs, q)
        untwiddle(*prev)

        def body(r, carry, order=order):
            rs = pl.ds(pl.multiple_of(r * SUBLANES, SUBLANES), SUBLANES)
            bias = bias_ref[order, rs, :]
            for half in range(DFT_LANE // LANES):
                ls = slice(half * LANES, (half + 1) * LANES)
                li = slice(DFT_LANE + half * LANES, DFT_LANE + (half + 1) * LANES)
                q = [(g_scr[k1, rs, ls], g_scr[k1, rs, li]) for k1 in range(n1)]
                ev = _fft_dif(q[0::2], 1.0)
                od = _fft_dif(q[1::2], 1.0)
                for i in range(half_blocks):
                    y = _cadd(ev[i], _cmulc(od[i], np.exp(2j * np.pi * i / n1)))
                    cs = slice(i * DFT_LANE + ls.start, i * DFT_LANE + ls.stop)
                    for bi in range(2):
                        u = u_scr[bi, rs, cs]
                        conv = y[bi] + bias * u
                        if order == 0:
                            u_scr[bi, rs, cs] = x1_scr[bi, rs, cs] * conv
                        else:
                            out_ref[bi, rs, cs] = x2_scr[bi, rs, cs] * conv
            return carry

        lax.fori_loop(0, ct // SUBLANES, body, 0)


def _hyena(hyt4, wsh, bsh, kspec, bias, wf, wi, tw, n1, ct, group):
    B, _, C, L = hyt4.shape
    return pl.pallas_call(
        functools.partial(_hyena_kernel, L=L, n1=n1, ct=ct, group=group),
        grid=(C // ct, B // 2),
        in_specs=[
            pl.BlockSpec((2, 4, ct, L), lambda c, p: (p, 0, c, 0)),
            pl.BlockSpec((3, 3, ct, 1), lambda c, p: (0, 0, c, 0)),
            pl.BlockSpec((3, ct, 1), lambda c, p: (0, c, 0)),
            pl.BlockSpec((2, n1, ct, 2 * DFT_LANE), lambda c, p: (0, 0, c, 0), pipeline_mode=pl.Buffered(1)),
            pl.BlockSpec((2, ct, 1), lambda c, p: (0, c, 0)),
            _resident(wf.shape), _resident(wi.shape), _resident(tw.shape),
        ],
        out_specs=pl.BlockSpec((2, ct, L), lambda c, p: (p, c, 0)),
        out_shape=jax.ShapeDtypeStruct((B, C, L), F32),
        scratch_shapes=[pltpu.VMEM((2, ct, L), F32), pltpu.VMEM((2, ct, L), F32), pltpu.VMEM((2, ct, L), F32),
                        pltpu.VMEM((n1, ct, 2 * DFT_LANE), F32)],
        compiler_params=_cparams(("parallel", "parallel")),
        name="hyena",
    )(hyt4, wsh, bsh, kspec, bias, wf, wi, tw)


def _attn_kernel(qt_ref, k_ref, vt_ref, o_ref, *, L, tq):
    units = [(j, t0) for j in range(2) for t0 in range(0, tq, Q_TILE)]
    starts = list(range(0, L, KV_CHUNK))

    def scores(u, c0):
        j, t0 = u
        return jnp.dot(k_ref[0, j, c0:c0 + KV_CHUNK, :], qt_ref[0, j, :, t0:t0 + Q_TILE],
                       preferred_element_type=F32)

    m = {u: None for u in units}
    acc = {u: None for u in units}
    cur = {u: scores(u, starts[0]) for u in units}
    for i, c0 in enumerate(starts):
        nxt = {}
        for u in units:
            if i + 1 < len(starts):
                nxt[u] = scores(u, starts[i + 1])
            s = cur[u]
            mc = jnp.max(s, axis=0, keepdims=True)
            m_new = mc if m[u] is None else jnp.maximum(m[u], mc)
            p = jnp.exp2(s - m_new).astype(BF16)
            pv = jnp.dot(vt_ref[0, u[0], :, c0:c0 + KV_CHUNK], p, preferred_element_type=F32)
            acc[u] = pv if m[u] is None else jnp.exp2(m[u] - m_new) * acc[u] + pv
            m[u] = m_new
        cur = nxt
    for j, t0 in units:
        a = acc[(j, t0)]
        o_ref[0, j * V_DIM:(j + 1) * V_DIM, t0:t0 + Q_TILE] = (a[:V_DIM] / a[V_DIM:V_DIM + 1]).astype(BF16)


def _attn(qt, k, vt, tq):
    B, _, L, _ = k.shape
    return pl.pallas_call(
        functools.partial(_attn_kernel, L=L, tq=tq),
        grid=(B, HEAD_PAIRS, L // tq),
        in_specs=[pl.BlockSpec((1, 2, QK_DIM, tq), lambda b, p, i: (b, p, 0, i)),
                  pl.BlockSpec((1, 2, L, QK_DIM), lambda b, p, i: (b, p, 0, 0)),
                  pl.BlockSpec((1, 2, V_ROWS, L), lambda b, p, i: (b, p, 0, 0))],
        out_specs=pl.BlockSpec((1, 2 * V_DIM, tq), lambda b, p, i: (b, p, i)),
        out_shape=jax.ShapeDtypeStruct((B, ATTN_WIDTH, L), BF16),
        compiler_params=_cparams(("parallel", "parallel", "parallel")),
        name="attn",
    )(qt, k, vt)


def _final_kernel(x_ref, yt_ref, at_ref, zat_ref, gate_ref, bg_ref, why_ref, wat_ref, wout_ref, o_ref):
    yt = yt_ref[0]
    y = yt.T.astype(BF16)
    u_hy = jnp.dot(y, why_ref[...], preferred_element_type=F32)
    z = zat_ref[0].astype(F32)
    a = (at_ref[0].astype(F32).T * (z * jax.nn.sigmoid(z))).astype(BF16)
    u_at = jnp.dot(a, wat_ref[...], preferred_element_type=F32)
    gates = jax.nn.sigmoid(gate_ref[0].astype(F32) + bg_ref[...])
    merged = gates[:, :D_MODEL] * u_hy + gates[:, D_MODEL:] * u_at
    o_ref[0] = x_ref[0] + jnp.dot(merged.astype(BF16), wout_ref[...], preferred_element_type=F32)


def _final(x, yt, at, zat, gate, bg, why, wat, wout, tl):
    B, L, _ = x.shape
    tok = lambda w: pl.BlockSpec((1, tl, w), lambda b, i: (b, i, 0))
    return pl.pallas_call(
        _final_kernel,
        grid=(B, L // tl),
        in_specs=[tok(D_MODEL), pl.BlockSpec((1, HY_WIDTH, tl), lambda b, i: (b, 0, i)),
                  pl.BlockSpec((1, ATTN_WIDTH, tl), lambda b, i: (b, 0, i)), tok(ATTN_WIDTH), tok(2 * D_MODEL),
                  _resident(bg.shape), _resident(why.shape), _resident(wat.shape), _resident(wout.shape)],
        out_specs=tok(D_MODEL),
        out_shape=jax.ShapeDtypeStruct((B, L, D_MODEL), F32),
        compiler_params=_cparams(("parallel", "parallel")),
        name="final",
    )(x, yt, at, zat, gate, bg, why, wat, wout)


def _layer(x, g_norm, w_in, b_gate, w_short, b_short, w_f1, b_f1, freq_1, w_f2, b_f2, freq_2, w_f3,
           hy_bias, w_hy_out, g_cq, w_uq, g_ckv, w_ukv, g_qn, g_kn, w_attn_out, w_out):
    B, L, _ = x.shape
    n = 2 * L
    n1 = n // DFT_LANE
    assert B % 2 == 0 and n1 * DFT_LANE == n and n1 >= 2 and (n1 & (n1 - 1)) == 0
    tl = min(512, L)
    ct = 64
    group = min(8, n1)

    o_z, o_cq, o_ckv, o_kr, o_za = 3 * HY_WIDTH, 4 * HY_WIDTH, 4 * HY_WIDTH + Q_LORA, 4 * HY_WIDTH + Q_LORA + KV_LORA, \
        4 * HY_WIDTH + Q_LORA + KV_LORA + QK_ROPE
    del o_z
    w_in = w_in.astype(BF16)
    wt_hy = w_in[:, :4 * HY_WIDTH].T
    half = QK_ROPE // 2
    w_cq = jnp.concatenate([w_in[:, o_cq:o_ckv], jnp.zeros((D_MODEL, QK_NOPE), BF16), w_in[:, o_kr:o_za],
                            jnp.zeros((D_MODEL, LANES - QK_DIM), BF16)], axis=1)
    w_ckv = w_in[:, o_ckv:o_kr]
    w_rest = w_in[:, o_za:]

    perm = np.concatenate([np.arange(QK_NOPE), QK_NOPE + half + np.arange(half), QK_NOPE + np.arange(half)])
    wqt = w_uq.T.astype(BF16)
    wkv = w_ukv.reshape(KV_LORA, MLA_HEADS, QK_NOPE + V_DIM)
    wk = jnp.pad(wkv[:, :, :QK_NOPE], ((0, 0), (0, 0), (0, LANES - QK_NOPE)))
    wk = wk.reshape(KV_LORA, MLA_HEADS * LANES).astype(BF16)
    wvt = jnp.pad(wkv[:, :, QK_NOPE:], ((0, 0), (0, 0), (0, V_ROWS - V_DIM)))
    wvt = wvt.reshape(KV_LORA, MLA_HEADS * V_ROWS).T.astype(BF16)
    pad_lanes = lambda g: jnp.pad(g, ((0, 0), (0, LANES - QK_DIM)))
    gqc = g_qn.reshape(QK_DIM, 1)
    gqsc = gqc[perm]
    gk = g_kn.reshape(1, QK_DIM)
    gks = pad_lanes(gk[:, perm])
    gk = pad_lanes(gk)

    ctab, stab = _rope_tables(L)
    pad_tab = lambda tab: np.pad(tab, ((0, 0), (0, LANES - QK_DIM)))
    hyt, zat, gate, qt, k, vt = _in_proj(
        x, g_norm.reshape(1, D_MODEL), wt_hy, w_cq, w_ckv, w_rest, g_cq.reshape(1, Q_LORA),
        g_ckv.reshape(1, KV_LORA), jnp.asarray(pad_tab(ctab)), jnp.asarray(pad_tab(stab)),
        jnp.asarray(np.ascontiguousarray(ctab.T)), jnp.asarray(np.ascontiguousarray(stab.T)),
        wqt, wk, wvt, gqc, gqsc, gk, gks, tl)

    zfeat, t = _filter_features(L)
    w1 = jnp.concatenate([w_f1, jnp.zeros((LANES - FILTER_EMB, FILTER_HIDDEN), F32)], axis=0)
    w3 = w_f3.reshape(FILTER_HIDDEN, 2, 2, HY_WIDTH)
    w3f = w3[:, :, 0].reshape(FILTER_HIDDEN, 2 * HY_WIDTH).T
    w3b = w3[:, :, 1].reshape(FILTER_HIDDEN, 2 * HY_WIDTH).T
    deltas = np.abs(np.linspace(MIN_DECAY, MAX_DECAY, HY_WIDTH))
    dl = np.tile(deltas, 2).reshape(2 * HY_WIDTH, 1).astype(np.float32)
    col = lambda v: v.reshape(FILTER_HIDDEN, 1)
    zt = np.ascontiguousarray(np.stack([zfeat.T, zfeat[::-1].T]))
    t2 = np.ascontiguousarray(np.stack([t, t[::-1]]).reshape(2, 1, L))
    h2t = _fmlp(jnp.asarray(zt), w1.T, col(b_f1), col(freq_1), w_f2.T, col(b_f2), col(freq_2))
    wf, wi = (jnp.asarray(w).astype(BF16) for w in _dft_mats())
    tw = jnp.asarray(_twiddle(n1))
    kspec = _fspec(h2t, jnp.asarray(t2), w3f, w3b, jnp.asarray(dl), wf, tw, L, n1, ct, group)

    wsh = w_short.reshape(3, 3, HY_WIDTH).transpose(1, 0, 2)[..., None]
    bsh = b_short.reshape(3, HY_WIDTH, 1)
    yt = _hyena(hyt.reshape(B, 4, HY_WIDTH, L), wsh, bsh, kspec, hy_bias.reshape(2, HY_WIDTH, 1),
                wf, wi, tw, n1, ct, group)

    at = _attn(qt, k, vt, min(1024, L))

    return _final(x, yt, at, zat, gate, b_gate.reshape(1, 2 * D_MODEL), w_hy_out.astype(BF16),
                  w_attn_out.astype(BF16), w_out.astype(BF16), tl)


def kernel(x, g_norm, w_in, b_gate, w_short, b_short, w_f1, b_f1, freq_1, w_f2, b_f2, freq_2, w_f3, hy_bias, w_hy_out, g_cq, w_uq, g_ckv, w_ukv, g_qn, g_kn, w_attn_out, w_out):
    depth = g_norm.shape[0]
    for l in range(depth):
        x = _layer(x, g_norm[l], w_in[l], b_gate[l], w_short[l], b_short[l], w_f1[l], b_f1[l], freq_1[l], w_f2[l],
                   b_f2[l], freq_2[l], w_f3[l], hy_bias[l], w_hy_out[l], g_cq[l], w_uq[l], g_ckv[l], w_ukv[l],
                   g_qn[l], g_kn[l], w_attn_out[l], w_out[l])
    return x
```

```python
import functools
import math

import numpy as np
import jax
import jax.numpy as jnp
from jax import lax
from jax.experimental import pallas as pl
from jax.experimental.pallas import tpu as pltpu

D_MODEL = 1024
HY_WIDTH = 512
FILTER_EMB = 33
FILTER_HIDDEN = 64
DECAY_TARGET = 1e-2
FAST_DECAY = 0.3
SLOW_DECAY = 1.5
MIN_DECAY = math.log(DECAY_TARGET) / SLOW_DECAY
MAX_DECAY = math.log(DECAY_TARGET) / FAST_DECAY
MLA_HEADS = 8
HEAD_PAIRS = MLA_HEADS // 2
QK_NOPE = 64
QK_ROPE = 32
QK_DIM = QK_NOPE + QK_ROPE
V_DIM = 64
Q_LORA = 384
KV_LORA = 256
ATTN_WIDTH = MLA_HEADS * V_DIM
ROPE_THETA = 10000.0
EPS = 1e-6

LANES = 128
SUBLANES = 8
DFT_LANE = 256
V_ROWS = 80
KV_CHUNK = 256
Q_TILE = 256
VMEM_LIMIT = 56 * 1024 * 1024

F32 = jnp.float32
BF16 = jnp.bfloat16
_NT = (((1,), (1,)), ((), ()))
_NN = (((1,), (0,)), ((), ()))


def _cparams(sem):
    return pltpu.CompilerParams(dimension_semantics=sem, vmem_limit_bytes=VMEM_LIMIT)


def _resident(shape):
    nd = len(shape)
    return pl.BlockSpec(shape, lambda *_: (0,) * nd, pipeline_mode=pl.Buffered(1))


def _add(a, b):
    if a is None:
        return b
    if b is None:
        return a
    return a + b


def _sub(a, b):
    if b is None:
        return a
    if a is None:
        return -b
    return a - b


def _cadd(x, y):
    return (_add(x[0], y[0]), _add(x[1], y[1]))


def _csub(x, y):
    return (_sub(x[0], y[0]), _sub(x[1], y[1]))


def _scale(a, s):
    if a is None or s == 0.0:
        return None
    if s == 1.0:
        return a
    if s == -1.0:
        return -a
    return a * s


def _signed_sum(terms):
    pos = [a for sg, a in terms if a is not None and sg > 0]
    neg = [a for sg, a in terms if a is not None and sg < 0]
    if pos:
        r = pos[0]
        for a in pos[1:]:
            r = r + a
        for a in neg:
            r = r - a
        return r, 1.0
    if neg:
        r = neg[0]
        for a in neg[1:]:
            r = r + a
        return r, -1.0
    return None, 1.0


def _cmulc(x, w):
    wr, wi = float(np.real(w)), float(np.imag(w))
    if abs(wr) < 1e-15:
        wr = 0.0
    if abs(wi) < 1e-15:
        wi = 0.0
    xr, xi = x
    if wr != 0.0 and abs(abs(wr) - abs(wi)) < 1e-12:
        sr, si, c = np.sign(wr), np.sign(wi), abs(wr)
        re, sre = _signed_sum([(sr, xr), (-si, xi)])
        im, sim = _signed_sum([(si, xr), (sr, xi)])
        return (_scale(re, sre * c), _scale(im, sim * c))
    re = _sub(_scale(xr, wr), _scale(xi, wi))
    im = _add(_scale(xr, wi), _scale(xi, wr))
    return (re, im)


def _cdiff_mulc(x, y, w):
    wr, wi = float(np.real(w)), float(np.imag(w))
    (xr, xi), (yr, yi) = x, y
    unit = {(1, 0): ((1, xr, yr), (1, xi, yi)), (-1, 0): ((-1, xr, yr), (-1, xi, yi)),
            (0, 1): ((-1, xi, yi), (1, xr, yr)), (0, -1): ((1, xi, yi), (-1, xr, yr))}
    key = (int(round(wr)), int(round(wi)))
    if abs(wr - key[0]) < 1e-12 and abs(wi - key[1]) < 1e-12 and key in unit:
        parts = []
        for sg, a, b in unit[key]:
            v, sv = _signed_sum([(sg, a), (-sg, b)])
            parts.append(_scale(v, sv))
        return tuple(parts)
    return _cmulc(_csub(x, y), w)


def _cmul(x, y):
    xr, xi = x
    yr, yi = y
    re = _sub(None if xr is None else xr * yr, None if xi is None else xi * yi)
    im = _add(None if xr is None else xr * yi, None if xi is None else xi * yr)
    return (re, im)


def _fft_dif(xs, sign):
    n = len(xs)
    if n == 1:
        return xs
    half = n // 2
    a = [_cadd(xs[i], xs[i + half]) for i in range(half)]
    b = [_cdiff_mulc(xs[i], xs[i + half], np.exp(sign * 2j * np.pi * i / n)) for i in range(half)]
    ev = _fft_dif(a, sign)
    od = _fft_dif(b, sign)
    out = [None] * n
    out[0::2] = ev
    out[1::2] = od
    return out


def _dft_mats():
    n = np.arange(DFT_LANE)
    ang = 2.0 * np.pi * np.outer(n, n) / DFT_LANE
    fr, fi = np.cos(ang), -np.sin(ang)
    fwd = np.block([[fr, fi], [-fi, fr]]).astype(np.float32)
    inv = np.block([[fr, -fi], [fi, fr]]).astype(np.float32)
    return fwd, inv


def _twiddle(n1):
    n = n1 * DFT_LANE
    ang = 2.0 * np.pi * np.outer(np.arange(n1), np.arange(DFT_LANE)) / n
    tw = np.stack([np.cos(ang), -np.sin(ang)], axis=1)
    return np.ascontiguousarray(np.broadcast_to(tw[:, :, None, :], (n1, 2, SUBLANES, DFT_LANE))).astype(np.float32)


def _filter_features(L):
    t = np.linspace(0.0, 1.0, L)[:, None]
    bands = (FILTER_EMB - 1) // 2
    f = np.linspace(1e-4, bands - 1, bands)
    ang = (2.0 * np.pi / L) * np.arange(L)[:, None] * f[None, :]
    z = np.concatenate([t, np.cos(ang), -np.sin(ang)], axis=-1)
    zp = np.zeros((L, LANES), np.float32)
    zp[:, :FILTER_EMB] = z
    return zp, t[:, 0].astype(np.float32)


def _rope_tables(L):
    pos = np.arange(L, dtype=np.float64)
    inv_freq = ROPE_THETA ** (-np.arange(0, QK_ROPE, 2, dtype=np.float64) / QK_ROPE)
    ang = pos[:, None] * inv_freq[None, :]
    c, s = np.cos(ang), np.sin(ang)
    ctab = np.concatenate([np.ones((L, QK_NOPE)), c, c], axis=1)
    stab = np.concatenate([np.zeros((L, QK_NOPE)), -s, s], axis=1)
    return ctab.astype(np.float32), stab.astype(np.float32)


def _in_proj_kernel(x_ref, gn_ref, wt_hy_ref, w_cq_ref, w_ckv_ref, w_rest_ref, gcq_ref, gckv_ref,
                    ct_ref, st_ref, ctt_ref, stt_ref, wqt_ref, wk_ref, wvt_ref,
                    gqc_ref, gqsc_ref, gk_ref, gks_ref, ones_ref,
                    hyt_ref, zat_ref, gate_ref, qt_ref, k_ref, vt_ref):
    x = x_ref[0]
    ms = jnp.mean(x * x, axis=-1, keepdims=True)
    h = (x * lax.rsqrt(ms + EPS) * gn_ref[...]).astype(BF16)

    def latent(c, g_ref):
        r = lax.rsqrt(jnp.mean(c * c, axis=-1, keepdims=True) + EPS)
        return (c * r * g_ref[...]).astype(BF16)

    cqr = jnp.dot(h, w_cq_ref[...], preferred_element_type=F32)
    cq = latent(cqr[:, :Q_LORA], gcq_ref)
    ckv = latent(jnp.dot(h, w_ckv_ref[...], preferred_element_type=F32), gckv_ref)
    kr_a = cqr[:, Q_LORA:]
    qt_all = lax.dot_general(wqt_ref[...], cq, _NT, preferred_element_type=F32)
    ka_all = jnp.dot(ckv, wk_ref[...], preferred_element_type=F32)
    vt_all = lax.dot_general(wvt_ref[...], ckv, _NT, preferred_element_type=F32) + ones_ref[...]

    hyt_ref[0] = lax.dot_general(wt_hy_ref[...], h, _NT, preferred_element_type=F32).astype(BF16)
    rest = jnp.dot(h, w_rest_ref[...], preferred_element_type=F32)
    zat_ref[0] = rest[:, :ATTN_WIDTH].astype(BF16)
    gate_ref[0] = rest[:, ATTN_WIDTH:].astype(BF16)

    half = QK_ROPE // 2
    scale = math.log2(math.e) / math.sqrt(QK_DIM)
    qc, qs = gqc_ref[...] * ctt_ref[...], gqsc_ref[...] * stt_ref[...]
    kc, ks = gk_ref[...] * ct_ref[...], gks_ref[...] * st_ref[...]
    lane = lax.broadcasted_iota(jnp.int32, kr_a.shape, 1)
    kr_w = jnp.where(lane < QK_NOPE + half, pltpu.roll(kr_a, LANES - half, 1), pltpu.roll(kr_a, half, 1))
    for hd in range(MLA_HEADS):
        qa = qt_all[hd * QK_DIM:(hd + 1) * QK_DIM]
        qw = jnp.concatenate([qa[:QK_NOPE], qa[QK_NOPE + half:], qa[QK_NOPE:QK_NOPE + half]], axis=0)
        r = lax.rsqrt(jnp.mean(qa * qa, axis=0, keepdims=True) + EPS) * scale
        qt_ref[0, hd] = ((qa * qc + qw * qs) * r).astype(BF16)
        ka = ka_all[:, hd * LANES:(hd + 1) * LANES] + kr_a
        rk = lax.rsqrt(jnp.sum(ka * ka, axis=-1, keepdims=True) * (1.0 / QK_DIM) + EPS)
        k_ref[0, hd] = ((ka * kc + kr_w * ks) * rk)[:, :QK_DIM].astype(BF16)
        vt_ref[0, hd] = vt_all[hd * V_ROWS:(hd + 1) * V_ROWS].astype(BF16)


def _in_proj(x, gn, wt_hy, w_cq, w_ckv, w_rest, gcq, gckv, ctab, stab, ctab_t, stab_t,
             wqt, wk, wvt, gqc, gqsc, gk, gks, tl):
    ones = np.zeros((MLA_HEADS, V_ROWS, 1), np.float32)
    ones[:, V_DIM, 0] = 1.0
    ones = jnp.asarray(ones.reshape(MLA_HEADS * V_ROWS, 1))
    B, L, _ = x.shape
    n_rest = w_rest.shape[1]
    tab = pl.BlockSpec((tl, LANES), lambda b, i: (i, 0))
    tab_t = pl.BlockSpec((QK_DIM, tl), lambda b, i: (0, i))
    consts = (gn, wt_hy, w_cq, w_ckv, w_rest, gcq, gckv)
    qkv_consts = (wqt, wk, wvt, gqc, gqsc, gk, gks, ones)
    return pl.pallas_call(
        _in_proj_kernel,
        grid=(B, L // tl),
        in_specs=[pl.BlockSpec((1, tl, D_MODEL), lambda b, i: (b, i, 0))]
        + [_resident(a.shape) for a in consts] + [tab, tab, tab_t, tab_t] + [_resident(a.shape) for a in qkv_consts],
        out_specs=[
            pl.BlockSpec((1, 4 * HY_WIDTH, tl), lambda b, i: (b, 0, i)),
            pl.BlockSpec((1, tl, ATTN_WIDTH), lambda b, i: (b, i, 0)),
            pl.BlockSpec((1, tl, 2 * D_MODEL), lambda b, i: (b, i, 0)),
            pl.BlockSpec((1, MLA_HEADS, QK_DIM, tl), lambda b, i: (b, 0, 0, i)),
            pl.BlockSpec((1, MLA_HEADS, tl, QK_DIM), lambda b, i: (b, 0, i, 0)),
            pl.BlockSpec((1, MLA_HEADS, V_ROWS, tl), lambda b, i: (b, 0, 0, i)),
        ],
        out_shape=[
            jax.ShapeDtypeStruct((B, 4 * HY_WIDTH, L), BF16),
            jax.ShapeDtypeStruct((B, L, ATTN_WIDTH), BF16),
            jax.ShapeDtypeStruct((B, L, n_rest - ATTN_WIDTH), BF16),
            jax.ShapeDtypeStruct((B, MLA_HEADS, QK_DIM, L), BF16),
            jax.ShapeDtypeStruct((B, MLA_HEADS, L, QK_DIM), BF16),
            jax.ShapeDtypeStruct((B, MLA_HEADS, V_ROWS, L), BF16),
        ],
        compiler_params=_cparams(("parallel", "parallel")),
        name="in_proj",
    )(x, *consts, ctab, stab, ctab_t, stab_t, *qkv_consts)


def _fmlp_kernel(zt_ref, w1t_ref, b1_ref, f1_ref, w2t_ref, b2_ref, f2_ref, out_ref):
    hp = lax.Precision.HIGHEST
    h = jnp.sin(f1_ref[...] * (jnp.dot(w1t_ref[...], zt_ref[0], precision=hp, preferred_element_type=F32) + b1_ref[...]))
    out_ref[0] = jnp.sin(f2_ref[...] * (jnp.dot(w2t_ref[...], h, precision=hp, preferred_element_type=F32) + b2_ref[...]))


def _fmlp(zt, w1t, b1, f1, w2t, b2, f2):
    _, _, L = zt.shape
    return pl.pallas_call(
        _fmlp_kernel,
        grid=(2,),
        in_specs=[pl.BlockSpec((1, LANES, L), lambda d: (d, 0, 0)), _resident(w1t.shape), _resident(b1.shape),
                  _resident(f1.shape), _resident(w2t.shape), _resident(b2.shape), _resident(f2.shape)],
        out_specs=pl.BlockSpec((1, FILTER_HIDDEN, L), lambda d: (d, 0, 0)),
        out_shape=jax.ShapeDtypeStruct((2, FILTER_HIDDEN, L), F32),
        compiler_params=_cparams(("parallel",)),
        name="fmlp",
    )(zt, w1t, b1, f1, w2t, b2, f2)


def _stage_a_forward(load, g_scr, tw_ref, n1, rows_total):
    def body(r, carry):
        rs = pl.ds(pl.multiple_of(r * SUBLANES, SUBLANES), SUBLANES)
        for half in range(DFT_LANE // LANES):
            ls = slice(half * LANES, (half + 1) * LANES)
            li = slice(DFT_LANE + half * LANES, DFT_LANE + (half + 1) * LANES)
            xs = [load(i, rs, ls) for i in range(n1)]
            xs = [(None, None) if x is None else x for x in xs]
            X = _fft_dif(xs, -1.0)
            for k1 in range(n1):
                if tw_ref is None:
                    yr, yi = X[k1]
                else:
                    yr, yi = _cmul(X[k1], (tw_ref[k1, 0, :, ls], tw_ref[k1, 1, :, ls]))
                g_scr[k1, rs, ls] = yr
                g_scr[k1, rs, li] = yi
        return carry

    lax.fori_loop(0, rows_total // SUBLANES, body, 0)


def _fspec_kernel(h_ref, t_ref, w3f_ref, w3b_ref, dl_ref, wf_ref, tw_ref, out_ref, k_scr, g_scr,
                  *, L, n1, rows, group):
    dl = dl_ref[...]
    for d, w_ref in enumerate((w3f_ref, w3b_ref)):
        k = jnp.dot(w_ref[...], h_ref[d], precision=lax.Precision.HIGHEST, preferred_element_type=F32)
        k_scr[:, d * L:(d + 1) * L] = k * jnp.exp(-(dl * t_ref[d]))

    def load(i, rs, ls):
        return (k_scr[rs, i * DFT_LANE + ls.start:i * DFT_LANE + ls.stop], None)

    _stage_a_forward(load, g_scr, tw_ref, n1, rows)
    inv_n = 1.0 / (n1 * DFT_LANE)
    w = wf_ref[...]
    for g0 in range(0, n1, group):
        lhs = g_scr[g0:g0 + group].reshape(group * rows, 2 * DFT_LANE).astype(BF16)
        res = jnp.dot(lhs, w, preferred_element_type=F32)
        out_ref[0, g0:g0 + group] = res.reshape(group, rows, 2 * DFT_LANE) * inv_n


def _fspec(h2t, t2, w3f, w3b, dl, wf, tw, L, n1, rows, group):
    n_rows = w3f.shape[0]
    tiles_per_order = HY_WIDTH // rows
    row_spec = lambda shape: pl.BlockSpec(shape, lambda i: (i, 0))
    return pl.pallas_call(
        functools.partial(_fspec_kernel, L=L, n1=n1, rows=rows, group=group),
        grid=(n_rows // rows,),
        in_specs=[_resident(h2t.shape), _resident(t2.shape),
                  row_spec((rows, FILTER_HIDDEN)), row_spec((rows, FILTER_HIDDEN)), row_spec((rows, 1)),
                  _resident(wf.shape), _resident(tw.shape)],
        out_specs=pl.BlockSpec((1, n1, rows, 2 * DFT_LANE),
                               lambda i: (i // tiles_per_order, 0, i % tiles_per_order, 0)),
        out_shape=jax.ShapeDtypeStruct((n_rows // HY_WIDTH, n1, HY_WIDTH, 2 * DFT_LANE), F32),
        scratch_shapes=[pltpu.VMEM((rows, 2 * L), F32), pltpu.VMEM((n1, rows, 2 * DFT_LANE), F32)],
        compiler_params=_cparams(("parallel",)),
        name="fspec",
    )(h2t, t2, w3f, w3b, dl, wf, tw)


def _hyena_kernel(hy_ref, wsh_ref, bsh_ref, ksp_ref, bias_ref, wf_ref, wi_ref, tw_ref, out_ref,
                  u_scr, x1_scr, x2_scr, g_scr, *, L, n1, ct, group):
    lane = lax.broadcasted_iota(jnp.int32, (ct, LANES), 1)

    def short_conv(g, dst):
        for bi in range(2):
            p = hy_ref[bi, g].astype(F32)
            left = pltpu.roll(p, 1, 1)
            left = jnp.concatenate([jnp.where(lane == 0, 0.0, left[:, :LANES]), left[:, LANES:]], axis=1)
            right = pltpu.roll(p, L - 1, 1)
            right = jnp.concatenate([right[:, :L - LANES], jnp.where(lane == LANES - 1, 0.0, right[:, L - LANES:])],
                                    axis=1)
            conv = wsh_ref[g, 0] * left + wsh_ref[g, 1] * p + wsh_ref[g, 2] * right + bsh_ref[g]
            if g == 2:
                z = hy_ref[bi, 3].astype(F32)
                conv = conv * (z * jax.nn.sigmoid(z))
            dst[bi] = conv

    short_conv(0, u_scr)
    gate_scr = (x1_scr, x2_scr)

    half_blocks = n1 // 2
    wf = wf_ref[...]
    wi = wi_ref[...]
    re, im = slice(0, DFT_LANE), slice(DFT_LANE, 2 * DFT_LANE)

    def mxu_dft(x, w):
        res = jnp.dot(x.reshape(group * ct, 2 * DFT_LANE).astype(BF16), w, preferred_element_type=F32)
        res = res.reshape(group, ct, 2 * DFT_LANE)
        return res[..., re], res[..., im]

    for order in range(2):
        def load(i, rs, ls):
            if i >= half_blocks:
                return None
            cs = slice(i * DFT_LANE + ls.start, i * DFT_LANE + ls.stop)
            return (u_scr[0, rs, cs], u_scr[1, rs, cs])

        _stage_a_forward(load, g_scr, None, n1, ct)

        def twiddled(gs):
            g = g_scr[gs]
            gr, gi = g[..., re], g[..., im]
            twr = tw_ref[gs, 0, 0:1, :]
            twi = tw_ref[gs, 1, 0:1, :]
            return jnp.concatenate([gr * twr - gi * twi, gr * twi + gi * twr], axis=-1)

        def untwiddle(gs, q):
            qr, qi = q
            twr = tw_ref[gs, 0, 0:1, :]
            twi = tw_ref[gs, 1, 0:1, :]
            g_scr[gs, :, re] = qr * twr + qi * twi
            g_scr[gs, :, im] = qi * twr - qr * twi

        groups = [slice(g0, g0 + group) for g0 in range(0, n1, group)]
        spectra = [mxu_dft(twiddled(gs), wf) for gs in groups]
        short_conv(order + 1, gate_scr[order])
        prev = None
        for gs, (hr, hi) in zip(groups, spectra):
            kr, ki = ksp_ref[order, gs, :, re], ksp_ref[order, gs, :, im]
            q = mxu_dft(jnp.concatenate([hr * kr - hi * ki, hr * ki + hi * kr], axis=-1), wi)
            if prev is not None:
                untwiddle(*prev)
            prev = (gs, q)
        untwiddle(*prev)

        def body(r, carry, order=order):
            rs = pl.ds(pl.multiple_of(r * SUBLANES, SUBLANES), SUBLANES)
            bias = bias_ref[order, rs, :]
            for half in range(DFT_LANE // LANES):
                ls = slice(half * LANES, (half + 1) * LANES)
                li = slice(DFT_LANE + half * LANES, DFT_LANE + (half + 1) * LANES)
                q = [(g_scr[k1, rs, ls], g_scr[k1, rs, li]) for k1 in range(n1)]
                ev = _fft_dif(q[0::2], 1.0)
                od = _fft_dif(q[1::2], 1.0)
                for i in range(half_blocks):
                    y = _cadd(ev[i], _cmulc(od[i], np.exp(2j * np.pi * i / n1)))
                    cs = slice(i * DFT_LANE + ls.start, i * DFT_LANE + ls.stop)
                    for bi in range(2):
                        u = u_scr[bi, rs, cs]
                        conv = y[bi] + bias * u
                        if order == 0:
                            u_scr[bi, rs, cs] = x1_scr[bi, rs, cs] * conv
                        else:
                            out_ref[bi, rs, cs] = x2_scr[bi, rs, cs] * conv
            return carry

        lax.fori_loop(0, ct // SUBLANES, body, 0)


def _hyena(hyt4, wsh, bsh, kspec, bias, wf, wi, tw, n1, ct, group):
    B, _, C, L = hyt4.shape
    return pl.pallas_call(
        functools.partial(_hyena_kernel, L=L, n1=n1, ct=ct, group=group),
        grid=(C // ct, B // 2),
        in_specs=[
            pl.BlockSpec((2, 4, ct, L), lambda c, p: (p, 0, c, 0)),
            pl.BlockSpec((3, 3, ct, 1), lambda c, p: (0, 0, c, 0)),
            pl.BlockSpec((3, ct, 1), lambda c, p: (0, c, 0)),
            pl.BlockSpec((2, n1, ct, 2 * DFT_LANE), lambda c, p: (0, 0, c, 0), pipeline_mode=pl.Buffered(1)),
            pl.BlockSpec((2, ct, 1), lambda c, p: (0, c, 0)),
            _resident(wf.shape), _resident(wi.shape), _resident(tw.shape),
        ],
        out_specs=pl.BlockSpec((2, ct, L), lambda c, p: (p, c, 0)),
        out_shape=jax.ShapeDtypeStruct((B, C, L), F32),
        scratch_shapes=[pltpu.VMEM((2, ct, L), F32), pltpu.VMEM((2, ct, L), F32), pltpu.VMEM((2, ct, L), F32),
                        pltpu.VMEM((n1, ct, 2 * DFT_LANE), F32)],
        compiler_params=_cparams(("parallel", "parallel")),
        name="hyena",
    )(hyt4, wsh, bsh, kspec, bias, wf, wi, tw)


def _attn_kernel(qt_ref, k_ref, vt_ref, o_ref, *, L, tq):
    units = [(j, t0) for j in range(2) for t0 in range(0, tq, Q_TILE)]
    starts = list(range(0, L, KV_CHUNK))

    def scores(u, c0):
        j, t0 = u
        return jnp.dot(k_ref[0, j, c0:c0 + KV_CHUNK, :], qt_ref[0, j, :, t0:t0 + Q_TILE],
                       preferred_element_type=F32)

    m = {u: None for u in units}
    acc = {u: None for u in units}
    cur = {u: scores(u, starts[0]) for u in units}
    for i, c0 in enumerate(starts):
        nxt = {}
        for u in units:
            if i + 1 < len(starts):
                nxt[u] = scores(u, starts[i + 1])
            s = cur[u]
            mc = jnp.max(s, axis=0, keepdims=True)
            m_new = mc if m[u] is None else jnp.maximum(m[u], mc)
            p = jnp.exp2(s - m_new).astype(BF16)
            pv = jnp.dot(vt_ref[0, u[0], :, c0:c0 + KV_CHUNK], p, preferred_element_type=F32)
            acc[u] = pv if m[u] is None else jnp.exp2(m[u] - m_new) * acc[u] + pv
            m[u] = m_new
        cur = nxt
    for j, t0 in units:
        a = acc[(j, t0)]
        o_ref[0, j * V_DIM:(j + 1) * V_DIM, t0:t0 + Q_TILE] = (a[:V_DIM] / a[V_DIM:V_DIM + 1]).astype(BF16)


def _attn(qt, k, vt, tq):
    B, _, L, _ = k.shape
    return pl.pallas_call(
        functools.partial(_attn_kernel, L=L, tq=tq),
        grid=(B, HEAD_PAIRS, L // tq),
        in_specs=[pl.BlockSpec((1, 2, QK_DIM, tq), lambda b, p, i: (b, p, 0, i)),
                  pl.BlockSpec((1, 2, L, QK_DIM), lambda b, p, i: (b, p, 0, 0)),
                  pl.BlockSpec((1, 2, V_ROWS, L), lambda b, p, i: (b, p, 0, 0))],
        out_specs=pl.BlockSpec((1, 2 * V_DIM, tq), lambda b, p, i: (b, p, i)),
        out_shape=jax.ShapeDtypeStruct((B, ATTN_WIDTH, L), BF16),
        compiler_params=_cparams(("parallel", "parallel", "parallel")),
        name="attn",
    )(qt, k, vt)


def _final_kernel(x_ref, yt_ref, at_ref, zat_ref, gate_ref, bg_ref, why_ref, wat_ref, wout_ref, o_ref):
    yt = yt_ref[0]
    y = yt.T.astype(BF16)
    u_hy = jnp.dot(y, why_ref[...], preferred_element_type=F32)
    z = zat_ref[0].astype(F32)
    a = (at_ref[0].astype(F32).T * (z * jax.nn.sigmoid(z))).astype(BF16)
    u_at = jnp.dot(a, wat_ref[...], preferred_element_type=F32)
    gates = jax.nn.sigmoid(gate_ref[0].astype(F32) + bg_ref[...])
    merged = gates[:, :D_MODEL] * u_hy + gates[:, D_MODEL:] * u_at
    o_ref[0] = x_ref[0] + jnp.dot(merged.astype(BF16), wout_ref[...], preferred_element_type=F32)


def _final(x, yt, at, zat, gate, bg, why, wat, wout, tl):
    B, L, _ = x.shape
    tok = lambda w: pl.BlockSpec((1, tl, w), lambda b, i: (b, i, 0))
    return pl.pallas_call(
        _final_kernel,
        grid=(B, L // tl),
        in_specs=[tok(D_MODEL), pl.BlockSpec((1, HY_WIDTH, tl), lambda b, i: (b, 0, i)),
                  pl.BlockSpec((1, ATTN_WIDTH, tl), lambda b, i: (b, 0, i)), tok(ATTN_WIDTH), tok(2 * D_MODEL),
                  _resident(bg.shape), _resident(why.shape), _resident(wat.shape), _resident(wout.shape)],
        out_specs=tok(D_MODEL),
        out_shape=jax.ShapeDtypeStruct((B, L, D_MODEL), F32),
        compiler_params=_cparams(("parallel", "parallel")),
        name="final",
    )(x, yt, at, zat, gate, bg, why, wat, wout)


def _layer(x, g_norm, w_in, b_gate, w_short, b_short, w_f1, b_f1, freq_1, w_f2, b_f2, freq_2, w_f3,
           hy_bias, w_hy_out, g_cq, w_uq, g_ckv, w_ukv, g_qn, g_kn, w_attn_out, w_out):
    B, L, _ = x.shape
    n = 2 * L
    n1 = n // DFT_LANE
    assert B % 2 == 0 and n1 * DFT_LANE == n and n1 >= 2 and (n1 & (n1 - 1)) == 0
    tl = min(512, L)
    ct = 64
    group = min(8, n1)

    o_z, o_cq, o_ckv, o_kr, o_za = 3 * HY_WIDTH, 4 * HY_WIDTH, 4 * HY_WIDTH + Q_LORA, 4 * HY_WIDTH + Q_LORA + KV_LORA, \
        4 * HY_WIDTH + Q_LORA + KV_LORA + QK_ROPE
    del o_z
    w_in = w_in.astype(BF16)
    wt_hy = w_in[:, :4 * HY_WIDTH].T
    half = QK_ROPE // 2
    w_cq = jnp.concatenate([w_in[:, o_cq:o_ckv], jnp.zeros((D_MODEL, QK_NOPE), BF16), w_in[:, o_kr:o_za],
                            jnp.zeros((D_MODEL, LANES - QK_DIM), BF16)], axis=1)
    w_ckv = w_in[:, o_ckv:o_kr]
    w_rest = w_in[:, o_za:]

    perm = np.concatenate([np.arange(QK_NOPE), QK_NOPE + half + np.arange(half), QK_NOPE + np.arange(half)])
    wqt = w_uq.T.astype(BF16)
    wkv = w_ukv.reshape(KV_LORA, MLA_HEADS, QK_NOPE + V_DIM)
    wk = jnp.pad(wkv[:, :, :QK_NOPE], ((0, 0), (0, 0), (0, LANES - QK_NOPE)))
    wk = wk.reshape(KV_LORA, MLA_HEADS * LANES).astype(BF16)
    wvt = jnp.pad(wkv[:, :, QK_NOPE:], ((0, 0), (0, 0), (0, V_ROWS - V_DIM)))
    wvt = wvt.reshape(KV_LORA, MLA_HEADS * V_ROWS).T.astype(BF16)
    pad_lanes = lambda g: jnp.pad(g, ((0, 0), (0, LANES - QK_DIM)))
    gqc = g_qn.reshape(QK_DIM, 1)
    gqsc = gqc[perm]
    gk = g_kn.reshape(1, QK_DIM)
    gks = pad_lanes(gk[:, perm])
    gk = pad_lanes(gk)

    ctab, stab = _rope_tables(L)
    pad_tab = lambda tab: np.pad(tab, ((0, 0), (0, LANES - QK_DIM)))
    hyt, zat, gate, qt, k, vt = _in_proj(
        x, g_norm.reshape(1, D_MODEL), wt_hy, w_cq, w_ckv, w_rest, g_cq.reshape(1, Q_LORA),
        g_ckv.reshape(1, KV_LORA), jnp.asarray(pad_tab(ctab)), jnp.asarray(pad_tab(stab)),
        jnp.asarray(np.ascontiguousarray(ctab.T)), jnp.asarray(np.ascontiguousarray(stab.T)),
        wqt, wk, wvt, gqc, gqsc, gk, gks, tl)

    zfeat, t = _filter_features(L)
    w1 = jnp.concatenate([w_f1, jnp.zeros((LANES - FILTER_EMB, FILTER_HIDDEN), F32)], axis=0)
    w3 = w_f3.reshape(FILTER_HIDDEN, 2, 2, HY_WIDTH)
    w3f = w3[:, :, 0].reshape(FILTER_HIDDEN, 2 * HY_WIDTH).T
    w3b = w3[:, :, 1].reshape(FILTER_HIDDEN, 2 * HY_WIDTH).T
    deltas = np.abs(np.linspace(MIN_DECAY, MAX_DECAY, HY_WIDTH))
    dl = np.tile(deltas, 2).reshape(2 * HY_WIDTH, 1).astype(np.float32)
    col = lambda v: v.reshape(FILTER_HIDDEN, 1)
    zt = np.ascontiguousarray(np.stack([zfeat.T, zfeat[::-1].T]))
    t2 = np.ascontiguousarray(np.stack([t, t[::-1]]).reshape(2, 1, L))
    h2t = _fmlp(jnp.asarray(zt), w1.T, col(b_f1), col(freq_1), w_f2.T, col(b_f2), col(freq_2))
    wf, wi = (jnp.asarray(w).astype(BF16) for w in _dft_mats())
    tw = jnp.asarray(_twiddle(n1))
    kspec = _fspec(h2t, jnp.asarray(t2), w3f, w3b, jnp.asarray(dl), wf, tw, L, n1, ct, group)

    wsh = w_short.reshape(3, 3, HY_WIDTH).transpose(1, 0, 2)[..., None]
    bsh = b_short.reshape(3, HY_WIDTH, 1)
    yt = _hyena(hyt.reshape(B, 4, HY_WIDTH, L), wsh, bsh, kspec, hy_bias.reshape(2, HY_WIDTH, 1),
                wf, wi, tw, n1, ct, group)

    at = _attn(qt, k, vt, min(2048, L))

    return _final(x, yt, at, zat, gate, b_gate.reshape(1, 2 * D_MODEL), w_hy_out.astype(BF16),
                  w_attn_out.astype(BF16), w_out.astype(BF16), tl)


def kernel(x, g_norm, w_in, b_gate, w_short, b_short, w_f1, b_f1, freq_1, w_f2, b_f2, freq_2, w_f3, hy_bias, w_hy_out, g_cq, w_uq, g_ckv, w_ukv, g_qn, g_kn, w_attn_out, w_out):
    depth = g_norm.shape[0]
    for l in range(depth):
        x = _layer(x, g_norm[l], w_in[l], b_gate[l], w_short[l], b_short[l], w_f1[l], b_f1[l], freq_1[l], w_f2[l],
                   b_f2[l], freq_2[l], w_f3[l], hy_bias[l], w_hy_out[l], g_cq[l], w_uq[l], g_ckv[l], w_ukv[l],
                   g_qn[l], g_kn[l], w_attn_out[l], w_out[l])
    return x
```

```python
import functools
import math

import numpy as np
import jax
import jax.numpy as jnp
from jax import lax
from jax.experimental import pallas as pl
from jax.experimental.pallas import tpu as pltpu

D_MODEL = 1024
HY_WIDTH = 512
FILTER_EMB = 33
FILTER_HIDDEN = 64
DECAY_TARGET = 1e-2
FAST_DECAY = 0.3
SLOW_DECAY = 1.5
MIN_DECAY = math.log(DECAY_TARGET) / SLOW_DECAY
MAX_DECAY = math.log(DECAY_TARGET) / FAST_DECAY
MLA_HEADS = 8
HEAD_PAIRS = MLA_HEADS // 2
QK_NOPE = 64
QK_ROPE = 32
QK_DIM = QK_NOPE + QK_ROPE
V_DIM = 64
Q_LORA = 384
KV_LORA = 256
ATTN_WIDTH = MLA_HEADS * V_DIM
ROPE_THETA = 10000.0
EPS = 1e-6

LANES = 128
SUBLANES = 8
DFT_LANE = 256
V_ROWS = 80
KV_CHUNK = 256
Q_TILE = 256
VMEM_LIMIT = 56 * 1024 * 1024

F32 = jnp.float32
BF16 = jnp.bfloat16
_NT = (((1,), (1,)), ((), ()))
_NN = (((1,), (0,)), ((), ()))


def _cparams(sem):
    return pltpu.CompilerParams(dimension_semantics=sem, vmem_limit_bytes=VMEM_LIMIT)


def _resident(shape):
    nd = len(shape)
    return pl.BlockSpec(shape, lambda *_: (0,) * nd, pipeline_mode=pl.Buffered(1))


def _add(a, b):
    if a is None:
        return b
    if b is None:
        return a
    return a + b


def _sub(a, b):
    if b is None:
        return a
    if a is None:
        return -b
    return a - b


def _cadd(x, y):
    return (_add(x[0], y[0]), _add(x[1], y[1]))


def _csub(x, y):
    return (_sub(x[0], y[0]), _sub(x[1], y[1]))


def _scale(a, s):
    if a is None or s == 0.0:
        return None
    if s == 1.0:
        return a
    if s == -1.0:
        return -a
    return a * s


def _signed_sum(terms):
    pos = [a for sg, a in terms if a is not None and sg > 0]
    neg = [a for sg, a in terms if a is not None and sg < 0]
    if pos:
        r = pos[0]
        for a in pos[1:]:
            r = r + a
        for a in neg:
            r = r - a
        return r, 1.0
    if neg:
        r = neg[0]
        for a in neg[1:]:
            r = r + a
        return r, -1.0
    return None, 1.0


def _cmulc(x, w):
    wr, wi = float(np.real(w)), float(np.imag(w))
    if abs(wr) < 1e-15:
        wr = 0.0
    if abs(wi) < 1e-15:
        wi = 0.0
    xr, xi = x
    if wr != 0.0 and abs(abs(wr) - abs(wi)) < 1e-12:
        sr, si, c = np.sign(wr), np.sign(wi), abs(wr)
        re, sre = _signed_sum([(sr, xr), (-si, xi)])
        im, sim = _signed_sum([(si, xr), (sr, xi)])
        return (_scale(re, sre * c), _scale(im, sim * c))
    re = _sub(_scale(xr, wr), _scale(xi, wi))
    im = _add(_scale(xr, wi), _scale(xi, wr))
    return (re, im)


def _cdiff_mulc(x, y, w):
    wr, wi = float(np.real(w)), float(np.imag(w))
    (xr, xi), (yr, yi) = x, y
    unit = {(1, 0): ((1, xr, yr), (1, xi, yi)), (-1, 0): ((-1, xr, yr), (-1, xi, yi)),
            (0, 1): ((-1, xi, yi), (1, xr, yr)), (0, -1): ((1, xi, yi), (-1, xr, yr))}
    key = (int(round(wr)), int(round(wi)))
    if abs(wr - key[0]) < 1e-12 and abs(wi - key[1]) < 1e-12 and key in unit:
        parts = []
        for sg, a, b in unit[key]:
            v, sv = _signed_sum([(sg, a), (-sg, b)])
            parts.append(_scale(v, sv))
        return tuple(parts)
    return _cmulc(_csub(x, y), w)


def _cmul(x, y):
    xr, xi = x
    yr, yi = y
    re = _sub(None if xr is None else xr * yr, None if xi is None else xi * yi)
    im = _add(None if xr is None else xr * yi, None if xi is None else xi * yr)
    return (re, im)


def _fft_dif(xs, sign):
    n = len(xs)
    if n == 1:
        return xs
    half = n // 2
    a = [_cadd(xs[i], xs[i + half]) for i in range(half)]
    b = [_cdiff_mulc(xs[i], xs[i + half], np.exp(sign * 2j * np.pi * i / n)) for i in range(half)]
    ev = _fft_dif(a, sign)
    od = _fft_dif(b, sign)
    out = [None] * n
    out[0::2] = ev
    out[1::2] = od
    return out


def _dft_mats():
    n = np.arange(DFT_LANE)
    ang = 2.0 * np.pi * np.outer(n, n) / DFT_LANE
    fr, fi = np.cos(ang), -np.sin(ang)
    fwd = np.block([[fr, fi], [-fi, fr]]).astype(np.float32)
    inv = np.block([[fr, -fi], [fi, fr]]).astype(np.float32)
    return fwd, inv


def _twiddle(n1):
    n = n1 * DFT_LANE
    ang = 2.0 * np.pi * np.outer(np.arange(n1), np.arange(DFT_LANE)) / n
    tw = np.stack([np.cos(ang), -np.sin(ang)], axis=1)
    return np.ascontiguousarray(np.broadcast_to(tw[:, :, None, :], (n1, 2, SUBLANES, DFT_LANE))).astype(np.float32)


def _filter_features(L):
    t = np.linspace(0.0, 1.0, L)[:, None]
    bands = (FILTER_EMB - 1) // 2
    f = np.linspace(1e-4, bands - 1, bands)
    ang = (2.0 * np.pi / L) * np.arange(L)[:, None] * f[None, :]
    z = np.concatenate([t, np.cos(ang), -np.sin(ang)], axis=-1)
    zp = np.zeros((L, LANES), np.float32)
    zp[:, :FILTER_EMB] = z
    return zp, t[:, 0].astype(np.float32)


def _rope_tables(L):
    pos = np.arange(L, dtype=np.float64)
    inv_freq = ROPE_THETA ** (-np.arange(0, QK_ROPE, 2, dtype=np.float64) / QK_ROPE)
    ang = pos[:, None] * inv_freq[None, :]
    c, s = np.cos(ang), np.sin(ang)
    ctab = np.concatenate([np.ones((L, QK_NOPE)), c, c], axis=1)
    stab = np.concatenate([np.zeros((L, QK_NOPE)), -s, s], axis=1)
    return ctab.astype(np.float32), stab.astype(np.float32)


def _in_proj_kernel(x_ref, gn_ref, wt_hy_ref, w_cq_ref, w_ckv_ref, w_rest_ref, gcq_ref, gckv_ref,
                    ct_ref, st_ref, ctt_ref, stt_ref, wqt_ref, wk_ref, wvt_ref,
                    gqc_ref, gqsc_ref, gk_ref, gks_ref, ones_ref,
                    hyt_ref, zat_ref, gate_ref, qt_ref, k_ref, vt_ref):
    x = x_ref[0]
    ms = jnp.mean(x * x, axis=-1, keepdims=True)
    h = (x * lax.rsqrt(ms + EPS) * gn_ref[...]).astype(BF16)

    def latent(c, g_ref):
        r = lax.rsqrt(jnp.mean(c * c, axis=-1, keepdims=True) + EPS)
        return (c * r * g_ref[...]).astype(BF16)

    cqr = jnp.dot(h, w_cq_ref[...], preferred_element_type=F32)
    cq = latent(cqr[:, :Q_LORA], gcq_ref)
    ckv = latent(jnp.dot(h, w_ckv_ref[...], preferred_element_type=F32), gckv_ref)
    kr_a = cqr[:, Q_LORA:]
    qt_all = lax.dot_general(wqt_ref[...], cq, _NT, preferred_element_type=F32)
    ka_all = jnp.dot(ckv, wk_ref[...], preferred_element_type=F32)
    vt_all = lax.dot_general(wvt_ref[...], ckv, _NT, preferred_element_type=F32) + ones_ref[...]

    hyt_ref[0] = lax.dot_general(wt_hy_ref[...], h, _NT, preferred_element_type=F32).astype(BF16)
    rest = jnp.dot(h, w_rest_ref[...], preferred_element_type=F32)
    zat_ref[0] = rest[:, :ATTN_WIDTH].astype(BF16)
    gate_ref[0] = rest[:, ATTN_WIDTH:].astype(BF16)

    half = QK_ROPE // 2
    scale = math.log2(math.e) / math.sqrt(QK_DIM)
    qc, qs = gqc_ref[...] * ctt_ref[...], gqsc_ref[...] * stt_ref[...]
    kc, ks = gk_ref[...] * ct_ref[...], gks_ref[...] * st_ref[...]
    lane = lax.broadcasted_iota(jnp.int32, kr_a.shape, 1)
    kr_w = jnp.where(lane < QK_NOPE + half, pltpu.roll(kr_a, LANES - half, 1), pltpu.roll(kr_a, half, 1))
    for hd in range(MLA_HEADS):
        qa = qt_all[hd * QK_DIM:(hd + 1) * QK_DIM]
        qw = jnp.concatenate([qa[:QK_NOPE], qa[QK_NOPE + half:], qa[QK_NOPE:QK_NOPE + half]], axis=0)
        r = lax.rsqrt(jnp.mean(qa * qa, axis=0, keepdims=True) + EPS) * scale
        qt_ref[0, hd] = ((qa * qc + qw * qs) * r).astype(BF16)
        ka = ka_all[:, hd * LANES:(hd + 1) * LANES] + kr_a
        rk = lax.rsqrt(jnp.sum(ka * ka, axis=-1, keepdims=True) * (1.0 / QK_DIM) + EPS)
        k_ref[0, hd] = ((ka * kc + kr_w * ks) * rk)[:, :QK_DIM].astype(BF16)
        vt_ref[0, hd] = vt_all[hd * V_ROWS:(hd + 1) * V_ROWS].astype(BF16)


def _in_proj(x, gn, wt_hy, w_cq, w_ckv, w_rest, gcq, gckv, ctab, stab, ctab_t, stab_t,
             wqt, wk, wvt, gqc, gqsc, gk, gks, tl):
    ones = np.zeros((MLA_HEADS, V_ROWS, 1), np.float32)
    ones[:, V_DIM, 0] = 1.0
    ones = jnp.asarray(ones.reshape(MLA_HEADS * V_ROWS, 1))
    B, L, _ = x.shape
    n_rest = w_rest.shape[1]
    tab = pl.BlockSpec((tl, LANES), lambda b, i: (i, 0))
    tab_t = pl.BlockSpec((QK_DIM, tl), lambda b, i: (0, i))
    consts = (gn, wt_hy, w_cq, w_ckv, w_rest, gcq, gckv)
    qkv_consts = (wqt, wk, wvt, gqc, gqsc, gk, gks, ones)
    return pl.pallas_call(
        _in_proj_kernel,
        grid=(B, L // tl),
        in_specs=[pl.BlockSpec((1, tl, D_MODEL), lambda b, i: (b, i, 0))]
        + [_resident(a.shape) for a in consts] + [tab, tab, tab_t, tab_t] + [_resident(a.shape) for a in qkv_consts],
        out_specs=[
            pl.BlockSpec((1, 4 * HY_WIDTH, tl), lambda b, i: (b, 0, i)),
            pl.BlockSpec((1, tl, ATTN_WIDTH), lambda b, i: (b, i, 0)),
            pl.BlockSpec((1, tl, 2 * D_MODEL), lambda b, i: (b, i, 0)),
            pl.BlockSpec((1, MLA_HEADS, QK_DIM, tl), lambda b, i: (b, 0, 0, i)),
            pl.BlockSpec((1, MLA_HEADS, tl, QK_DIM), lambda b, i: (b, 0, i, 0)),
            pl.BlockSpec((1, MLA_HEADS, V_ROWS, tl), lambda b, i: (b, 0, 0, i)),
        ],
        out_shape=[
            jax.ShapeDtypeStruct((B, 4 * HY_WIDTH, L), BF16),
            jax.ShapeDtypeStruct((B, L, ATTN_WIDTH), BF16),
            jax.ShapeDtypeStruct((B, L, n_rest - ATTN_WIDTH), BF16),
            jax.ShapeDtypeStruct((B, MLA_HEADS, QK_DIM, L), BF16),
            jax.ShapeDtypeStruct((B, MLA_HEADS, L, QK_DIM), BF16),
            jax.ShapeDtypeStruct((B, MLA_HEADS, V_ROWS, L), BF16),
        ],
        compiler_params=_cparams(("parallel", "parallel")),
        name="in_proj",
    )(x, *consts, ctab, stab, ctab_t, stab_t, *qkv_consts)


def _fmlp_kernel(zt_ref, w1t_ref, b1_ref, f1_ref, w2t_ref, b2_ref, f2_ref, out_ref):
    hp = lax.Precision.HIGHEST
    h = jnp.sin(f1_ref[...] * (jnp.dot(w1t_ref[...], zt_ref[0], precision=hp, preferred_element_type=F32) + b1_ref[...]))
    out_ref[0] = jnp.sin(f2_ref[...] * (jnp.dot(w2t_ref[...], h, precision=hp, preferred_element_type=F32) + b2_ref[...]))


def _fmlp(zt, w1t, b1, f1, w2t, b2, f2):
    _, _, L = zt.shape
    return pl.pallas_call(
        _fmlp_kernel,
        grid=(2,),
        in_specs=[pl.BlockSpec((1, LANES, L), lambda d: (d, 0, 0)), _resident(w1t.shape), _resident(b1.shape),
                  _resident(f1.shape), _resident(w2t.shape), _resident(b2.shape), _resident(f2.shape)],
        out_specs=pl.BlockSpec((1, FILTER_HIDDEN, L), lambda d: (d, 0, 0)),
        out_shape=jax.ShapeDtypeStruct((2, FILTER_HIDDEN, L), F32),
        compiler_params=_cparams(("parallel",)),
        name="fmlp",
    )(zt, w1t, b1, f1, w2t, b2, f2)


def _stage_a_forward(load, g_scr, tw_ref, n1, rows_total):
    def body(r, carry):
        rs = pl.ds(pl.multiple_of(r * SUBLANES, SUBLANES), SUBLANES)
        for half in range(DFT_LANE // LANES):
            ls = slice(half * LANES, (half + 1) * LANES)
            li = slice(DFT_LANE + half * LANES, DFT_LANE + (half + 1) * LANES)
            xs = [load(i, rs, ls) for i in range(n1)]
            xs = [(None, None) if x is None else x for x in xs]
            X = _fft_dif(xs, -1.0)
            for k1 in range(n1):
                if tw_ref is None:
                    yr, yi = X[k1]
                else:
                    yr, yi = _cmul(X[k1], (tw_ref[k1, 0, :, ls], tw_ref[k1, 1, :, ls]))
                g_scr[k1, rs, ls] = yr
                g_scr[k1, rs, li] = yi
        return carry

    lax.fori_loop(0, rows_total // SUBLANES, body, 0)


def _fspec_kernel(h_ref, t_ref, w3f_ref, w3b_ref, dl_ref, wf_ref, tw_ref, out_ref, k_scr, g_scr,
                  *, L, n1, rows, group):
    dl = dl_ref[...]
    for d, w_ref in enumerate((w3f_ref, w3b_ref)):
        k = jnp.dot(w_ref[...], h_ref[d], precision=lax.Precision.HIGHEST, preferred_element_type=F32)
        k_scr[:, d * L:(d + 1) * L] = k * jnp.exp(-(dl * t_ref[d]))

    def load(i, rs, ls):
        return (k_scr[rs, i * DFT_LANE + ls.start:i * DFT_LANE + ls.stop], None)

    _stage_a_forward(load, g_scr, tw_ref, n1, rows)
    inv_n = 1.0 / (n1 * DFT_LANE)
    w = wf_ref[...]
    for g0 in range(0, n1, group):
        lhs = g_scr[g0:g0 + group].reshape(group * rows, 2 * DFT_LANE).astype(BF16)
        res = jnp.dot(lhs, w, preferred_element_type=F32)
        out_ref[0, g0:g0 + group] = res.reshape(group, rows, 2 * DFT_LANE) * inv_n


def _fspec(h2t, t2, w3f, w3b, dl, wf, tw, L, n1, rows, group):
    n_rows = w3f.shape[0]
    tiles_per_order = HY_WIDTH // rows
    row_spec = lambda shape: pl.BlockSpec(shape, lambda i: (i, 0))
    return pl.pallas_call(
        functools.partial(_fspec_kernel, L=L, n1=n1, rows=rows, group=group),
        grid=(n_rows // rows,),
        in_specs=[_resident(h2t.shape), _resident(t2.shape),
                  row_spec((rows, FILTER_HIDDEN)), row_spec((rows, FILTER_HIDDEN)), row_spec((rows, 1)),
                  _resident(wf.shape), _resident(tw.shape)],
        out_specs=pl.BlockSpec((1, n1, rows, 2 * DFT_LANE),
                               lambda i: (i // tiles_per_order, 0, i % tiles_per_order, 0)),
        out_shape=jax.ShapeDtypeStruct((n_rows // HY_WIDTH, n1, HY_WIDTH, 2 * DFT_LANE), F32),
        scratch_shapes=[pltpu.VMEM((rows, 2 * L), F32), pltpu.VMEM((n1, rows, 2 * DFT_LANE), F32)],
        compiler_params=_cparams(("parallel",)),
        name="fspec",
    )(h2t, t2, w3f, w3b, dl, wf, tw)


def _hyena_kernel(hy_ref, wsh_ref, bsh_ref, ksp_ref, bias_ref, wf_ref, wi_ref, tw_ref, out_ref,
                  u_scr, x1_scr, x2_scr, g_scr, *, L, n1, ct, group):
    lane = lax.broadcasted_iota(jnp.int32, (ct, LANES), 1)

    def short_conv(g, dst):
        for bi in range(2):
            p = hy_ref[bi, g].astype(F32)
            left = pltpu.roll(p, 1, 1)
            left = jnp.concatenate([jnp.where(lane == 0, 0.0, left[:, :LANES]), left[:, LANES:]], axis=1)
            right = pltpu.roll(p, L - 1, 1)
            right = jnp.concatenate([right[:, :L - LANES], jnp.where(lane == LANES - 1, 0.0, right[:, L - LANES:])],
                                    axis=1)
            conv = wsh_ref[g, 0] * left + wsh_ref[g, 1] * p + wsh_ref[g, 2] * right + bsh_ref[g]
            if g == 2:
                z = hy_ref[bi, 3].astype(F32)
                conv = conv * (z * jax.nn.sigmoid(z))
            dst[bi] = conv

    short_conv(0, u_scr)
    gate_scr = (x1_scr, x2_scr)

    half_blocks = n1 // 2
    wf = wf_ref[...]
    wi = wi_ref[...]
    re, im = slice(0, DFT_LANE), slice(DFT_LANE, 2 * DFT_LANE)

    def mxu_dft(x, w):
        res = jnp.dot(x.reshape(group * ct, 2 * DFT_LANE).astype(BF16), w, preferred_element_type=F32)
        res = res.reshape(group, ct, 2 * DFT_LANE)
        return res[..., re], res[..., im]

    for order in range(2):
        def load(i, rs, ls):
            if i >= half_blocks:
                return None
            cs = slice(i * DFT_LANE + ls.start, i * DFT_LANE + ls.stop)
            return (u_scr[0, rs, cs], u_scr[1, rs, cs])

        _stage_a_forward(load, g_scr, None, n1, ct)

        def twiddled(gs):
            g = g_scr[gs]
            gr, gi = g[..., re], g[..., im]
            twr = tw_ref[gs, 0, 0:1, :]
            twi = tw_ref[gs, 1, 0:1, :]
            return jnp.concatenate([gr * twr - gi * twi, gr * twi + gi * twr], axis=-1)

        def untwiddle(gs, q):
            qr, qi = q
            twr = tw_ref[gs, 0, 0:1, :]
            twi = tw_ref[gs, 1, 0:1, :]
            g_scr[gs, :, re] = qr * twr + qi * twi
            g_scr[gs, :, im] = qi * twr - qr * twi

        groups = [slice(g0, g0 + group) for g0 in range(0, n1, group)]
        spectra = [mxu_dft(twiddled(gs), wf) for gs in groups]
        short_conv(order + 1, gate_scr[order])
        prev = None
        for gs, (hr, hi) in zip(groups, spectra):
            kr, ki = ksp_ref[order, gs, :, re], ksp_ref[order, gs, :, im]
            q = mxu_dft(jnp.concatenate([hr * kr - hi * ki, hr * ki + hi * kr], axis=-1), wi)
            if prev is not None:
                untwiddle(*prev)
            prev = (gs, q)
        untwiddle(*prev)

        def body(r, carry, order=order):
            rs = pl.ds(pl.multiple_of(r * SUBLANES, SUBLANES), SUBLANES)
            bias = bias_ref[order, rs, :]
            for half in range(DFT_LANE // LANES):
                ls = slice(half * LANES, (half + 1) * LANES)
                li = slice(DFT_LANE + half * LANES, DFT_LANE + (half + 1) * LANES)
                q = [(g_scr[k1, rs, ls], g_scr[k1, rs, li]) for k1 in range(n1)]
                ev = _fft_dif(q[0::2], 1.0)
                od = _fft_dif(q[1::2], 1.0)
                for i in range(half_blocks):
                    y = _cadd(ev[i], _cmulc(od[i], np.exp(2j * np.pi * i / n1)))
                    cs = slice(i * DFT_LANE + ls.start, i * DFT_LANE + ls.stop)
                    for bi in range(2):
                        u = u_scr[bi, rs, cs]
                        conv = y[bi] + bias * u
                        if order == 0:
                            u_scr[bi, rs, cs] = x1_scr[bi, rs, cs] * conv
                        else:
                            out_ref[bi, rs, cs] = x2_scr[bi, rs, cs] * conv
            return carry

        lax.fori_loop(0, ct // SUBLANES, body, 0)


def _hyena(hyt4, wsh, bsh, kspec, bias, wf, wi, tw, n1, ct, group):
    B, _, C, L = hyt4.shape
    return pl.pallas_call(
        functools.partial(_hyena_kernel, L=L, n1=n1, ct=ct, group=group),
        grid=(C // ct, B // 2),
        in_specs=[
            pl.BlockSpec((2, 4, ct, L), lambda c, p: (p, 0, c, 0)),
            pl.BlockSpec((3, 3, ct, 1), lambda c, p: (0, 0, c, 0)),
            pl.BlockSpec((3, ct, 1), lambda c, p: (0, c, 0)),
            pl.BlockSpec((2, n1, ct, 2 * DFT_LANE), lambda c, p: (0, 0, c, 0)),
            pl.BlockSpec((2, ct, 1), lambda c, p: (0, c, 0)),
            _resident(wf.shape), _resident(wi.shape), _resident(tw.shape),
        ],
        out_specs=pl.BlockSpec((2, ct, L), lambda c, p: (p, c, 0)),
        out_shape=jax.ShapeDtypeStruct((B, C, L), F32),
        scratch_shapes=[pltpu.VMEM((2, ct, L), F32), pltpu.VMEM((2, ct, L), F32), pltpu.VMEM((2, ct, L), F32),
                        pltpu.VMEM((n1, ct, 2 * DFT_LANE), F32)],
        compiler_params=_cparams(("parallel", "parallel")),
        name="hyena",
    )(hyt4, wsh, bsh, kspec, bias, wf, wi, tw)


def _attn_kernel(qt_ref, k_ref, vt_ref, o_ref, *, L, tq):
    units = [(j, t0) for j in range(2) for t0 in range(0, tq, Q_TILE)]
    starts = list(range(0, L, KV_CHUNK))

    def scores(u, c0):
        j, t0 = u
        return jnp.dot(k_ref[0, j, c0:c0 + KV_CHUNK, :], qt_ref[0, j, :, t0:t0 + Q_TILE],
                       preferred_element_type=F32)

    m = {u: None for u in units}
    acc = {u: None for u in units}
    cur = {u: scores(u, starts[0]) for u in units}
    for i, c0 in enumerate(starts):
        nxt = {}
        for u in units:
            if i + 1 < len(starts):
                nxt[u] = scores(u, starts[i + 1])
            s = cur[u]
            mc = jnp.max(s, axis=0, keepdims=True)
            m_new = mc if m[u] is None else jnp.maximum(m[u], mc)
            p = jnp.exp2(s - m_new).astype(BF16)
            pv = jnp.dot(vt_ref[0, u[0], :, c0:c0 + KV_CHUNK], p, preferred_element_type=F32)
            acc[u] = pv if m[u] is None else jnp.exp2(m[u] - m_new) * acc[u] + pv
            m[u] = m_new
        cur = nxt
    for j, t0 in units:
        a = acc[(j, t0)]
        o_ref[0, j * V_DIM:(j + 1) * V_DIM, t0:t0 + Q_TILE] = (a[:V_DIM] / a[V_DIM:V_DIM + 1]).astype(BF16)


def _attn(qt, k, vt, tq):
    B, _, L, _ = k.shape
    return pl.pallas_call(
        functools.partial(_attn_kernel, L=L, tq=tq),
        grid=(B, HEAD_PAIRS, L // tq),
        in_specs=[pl.BlockSpec((1, 2, QK_DIM, tq), lambda b, p, i: (b, p, 0, i)),
                  pl.BlockSpec((1, 2, L, QK_DIM), lambda b, p, i: (b, p, 0, 0)),
                  pl.BlockSpec((1, 2, V_ROWS, L), lambda b, p, i: (b, p, 0, 0))],
        out_specs=pl.BlockSpec((1, 2 * V_DIM, tq), lambda b, p, i: (b, p, i)),
        out_shape=jax.ShapeDtypeStruct((B, ATTN_WIDTH, L), BF16),
        compiler_params=_cparams(("parallel", "parallel", "parallel")),
        name="attn",
    )(qt, k, vt)


def _final_kernel(x_ref, yt_ref, at_ref, zat_ref, gate_ref, bg_ref, why_ref, wat_ref, wout_ref, o_ref, *, tl, sub):
    for r0 in range(0, tl, sub):
        rows = slice(r0, r0 + sub)
        y = yt_ref[0, :, rows].T.astype(BF16)
        u_hy = jnp.dot(y, why_ref[...], preferred_element_type=F32)
        z = zat_ref[0, rows, :].astype(F32)
        a = (at_ref[0, :, rows].astype(F32).T * (z * jax.nn.sigmoid(z))).astype(BF16)
        u_at = jnp.dot(a, wat_ref[...], preferred_element_type=F32)
        gates = jax.nn.sigmoid(gate_ref[0, rows, :].astype(F32) + bg_ref[...])
        merged = gates[:, :D_MODEL] * u_hy + gates[:, D_MODEL:] * u_at
        o_ref[0, rows, :] = x_ref[0, rows, :] + jnp.dot(merged.astype(BF16), wout_ref[...],
                                                         preferred_element_type=F32)


def _final(x, yt, at, zat, gate, bg, why, wat, wout, tl):
    B, L, _ = x.shape
    tok = lambda w: pl.BlockSpec((1, tl, w), lambda b, i: (b, i, 0))
    return pl.pallas_call(
        functools.partial(_final_kernel, tl=tl, sub=min(512, tl)),
        grid=(B, L // tl),
        in_specs=[tok(D_MODEL), pl.BlockSpec((1, HY_WIDTH, tl), lambda b, i: (b, 0, i)),
                  pl.BlockSpec((1, ATTN_WIDTH, tl), lambda b, i: (b, 0, i)), tok(ATTN_WIDTH), tok(2 * D_MODEL),
                  _resident(bg.shape), _resident(why.shape), _resident(wat.shape), _resident(wout.shape)],
        out_specs=tok(D_MODEL),
        out_shape=jax.ShapeDtypeStruct((B, L, D_MODEL), F32),
        compiler_params=_cparams(("parallel", "parallel")),
        name="final",
    )(x, yt, at, zat, gate, bg, why, wat, wout)


def _layer(x, g_norm, w_in, b_gate, w_short, b_short, w_f1, b_f1, freq_1, w_f2, b_f2, freq_2, w_f3,
           hy_bias, w_hy_out, g_cq, w_uq, g_ckv, w_ukv, g_qn, g_kn, w_attn_out, w_out):
    B, L, _ = x.shape
    n = 2 * L
    n1 = n // DFT_LANE
    assert B % 2 == 0 and n1 * DFT_LANE == n and n1 >= 2 and (n1 & (n1 - 1)) == 0
    tl = min(512, L)
    ct = 64
    group = min(8, n1)

    o_z, o_cq, o_ckv, o_kr, o_za = 3 * HY_WIDTH, 4 * HY_WIDTH, 4 * HY_WIDTH + Q_LORA, 4 * HY_WIDTH + Q_LORA + KV_LORA, \
        4 * HY_WIDTH + Q_LORA + KV_LORA + QK_ROPE
    del o_z
    w_in = w_in.astype(BF16)
    wt_hy = w_in[:, :4 * HY_WIDTH].T
    half = QK_ROPE // 2
    w_cq = jnp.concatenate([w_in[:, o_cq:o_ckv], jnp.zeros((D_MODEL, QK_NOPE), BF16), w_in[:, o_kr:o_za],
                            jnp.zeros((D_MODEL, LANES - QK_DIM), BF16)], axis=1)
    w_ckv = w_in[:, o_ckv:o_kr]
    w_rest = w_in[:, o_za:]

    perm = np.concatenate([np.arange(QK_NOPE), QK_NOPE + half + np.arange(half), QK_NOPE + np.arange(half)])
    wqt = w_uq.T.astype(BF16)
    wkv = w_ukv.reshape(KV_LORA, MLA_HEADS, QK_NOPE + V_DIM)
    wk = jnp.pad(wkv[:, :, :QK_NOPE], ((0, 0), (0, 0), (0, LANES - QK_NOPE)))
    wk = wk.reshape(KV_LORA, MLA_HEADS * LANES).astype(BF16)
    wvt = jnp.pad(wkv[:, :, QK_NOPE:], ((0, 0), (0, 0), (0, V_ROWS - V_DIM)))
    wvt = wvt.reshape(KV_LORA, MLA_HEADS * V_ROWS).T.astype(BF16)
    pad_lanes = lambda g: jnp.pad(g, ((0, 0), (0, LANES - QK_DIM)))
    gqc = g_qn.reshape(QK_DIM, 1)
    gqsc = gqc[perm]
    gk = g_kn.reshape(1, QK_DIM)
    gks = pad_lanes(gk[:, perm])
    gk = pad_lanes(gk)

    ctab, stab = _rope_tables(L)
    pad_tab = lambda tab: np.pad(tab, ((0, 0), (0, LANES - QK_DIM)))
    hyt, zat, gate, qt, k, vt = _in_proj(
        x, g_norm.reshape(1, D_MODEL), wt_hy, w_cq, w_ckv, w_rest, g_cq.reshape(1, Q_LORA),
        g_ckv.reshape(1, KV_LORA), jnp.asarray(pad_tab(ctab)), jnp.asarray(pad_tab(stab)),
        jnp.asarray(np.ascontiguousarray(ctab.T)), jnp.asarray(np.ascontiguousarray(stab.T)),
        wqt, wk, wvt, gqc, gqsc, gk, gks, tl)

    zfeat, t = _filter_features(L)
    w1 = jnp.concatenate([w_f1, jnp.zeros((LANES - FILTER_EMB, FILTER_HIDDEN), F32)], axis=0)
    w3 = w_f3.reshape(FILTER_HIDDEN, 2, 2, HY_WIDTH)
    w3f = w3[:, :, 0].reshape(FILTER_HIDDEN, 2 * HY_WIDTH).T
    w3b = w3[:, :, 1].reshape(FILTER_HIDDEN, 2 * HY_WIDTH).T
    deltas = np.abs(np.linspace(MIN_DECAY, MAX_DECAY, HY_WIDTH))
    dl = np.tile(deltas, 2).reshape(2 * HY_WIDTH, 1).astype(np.float32)
    col = lambda v: v.reshape(FILTER_HIDDEN, 1)
    zt = np.ascontiguousarray(np.stack([zfeat.T, zfeat[::-1].T]))
    t2 = np.ascontiguousarray(np.stack([t, t[::-1]]).reshape(2, 1, L))
    h2t = _fmlp(jnp.asarray(zt), w1.T, col(b_f1), col(freq_1), w_f2.T, col(b_f2), col(freq_2))
    wf, wi = (jnp.asarray(w).astype(BF16) for w in _dft_mats())
    tw = jnp.asarray(_twiddle(n1))
    kspec = _fspec(h2t, jnp.asarray(t2), w3f, w3b, jnp.asarray(dl), wf, tw, L, n1, ct, group)

    wsh = w_short.reshape(3, 3, HY_WIDTH).transpose(1, 0, 2)[..., None]
    bsh = b_short.reshape(3, HY_WIDTH, 1)
    yt = _hyena(hyt.reshape(B, 4, HY_WIDTH, L), wsh, bsh, kspec, hy_bias.reshape(2, HY_WIDTH, 1),
                wf, wi, tw, n1, ct, group)

    at = _attn(qt, k, vt, min(2048, L))

    return _final(x, yt, at, zat, gate, b_gate.reshape(1, 2 * D_MODEL), w_hy_out.astype(BF16),
                  w_attn_out.astype(BF16), w_out.astype(BF16), min(1024, L))


def kernel(x, g_norm, w_in, b_gate, w_short, b_short, w_f1, b_f1, freq_1, w_f2, b_f2, freq_2, w_f3, hy_bias, w_hy_out, g_cq, w_uq, g_ckv, w_ukv, g_qn, g_kn, w_attn_out, w_out):
    depth = g_norm.shape[0]
    for l in range(depth):
        x = _layer(x, g_norm[l], w_in[l], b_gate[l], w_short[l], b_short[l], w_f1[l], b_f1[l], freq_1[l], w_f2[l],
                   b_f2[l], freq_2[l], w_f3[l], hy_bias[l], w_hy_out[l], g_cq[l], w_uq[l], g_ckv[l], w_ukv[l],
                   g_qn[l], g_kn[l], w_attn_out[l], w_out[l])
    return x
```

```python
import functools
import math

import numpy as np
import jax
import jax.numpy as jnp
from jax import lax
from jax.experimental import pallas as pl
from jax.experimental.pallas import tpu as pltpu

D_MODEL = 1024
HY_WIDTH = 512
FILTER_EMB = 33
FILTER_HIDDEN = 64
DECAY_TARGET = 1e-2
FAST_DECAY = 0.3
SLOW_DECAY = 1.5
MIN_DECAY = math.log(DECAY_TARGET) / SLOW_DECAY
MAX_DECAY = math.log(DECAY_TARGET) / FAST_DECAY
MLA_HEADS = 8
HEAD_PAIRS = MLA_HEADS // 2
QK_NOPE = 64
QK_ROPE = 32
QK_DIM = QK_NOPE + QK_ROPE
V_DIM = 64
Q_LORA = 384
KV_LORA = 256
ATTN_WIDTH = MLA_HEADS * V_DIM
ROPE_THETA = 10000.0
EPS = 1e-6

LANES = 128
SUBLANES = 8
DFT_LANE = 256
V_ROWS = 80
KV_CHUNK = 256
Q_TILE = 256
VMEM_LIMIT = 56 * 1024 * 1024

F32 = jnp.float32
BF16 = jnp.bfloat16
_NT = (((1,), (1,)), ((), ()))
_NN = (((1,), (0,)), ((), ()))


def _cparams(sem):
    return pltpu.CompilerParams(dimension_semantics=sem, vmem_limit_bytes=VMEM_LIMIT)


def _resident(shape):
    nd = len(shape)
    return pl.BlockSpec(shape, lambda *_: (0,) * nd, pipeline_mode=pl.Buffered(1))


def _add(a, b):
    if a is None:
        return b
    if b is None:
        return a
    return a + b


def _sub(a, b):
    if b is None:
        return a
    if a is None:
        return -b
    return a - b


def _cadd(x, y):
    return (_add(x[0], y[0]), _add(x[1], y[1]))


def _csub(x, y):
    return (_sub(x[0], y[0]), _sub(x[1], y[1]))


def _scale(a, s):
    if a is None or s == 0.0:
        return None
    if s == 1.0:
        return a
    if s == -1.0:
        return -a
    return a * s


def _signed_sum(terms):
    pos = [a for sg, a in terms if a is not None and sg > 0]
    neg = [a for sg, a in terms if a is not None and sg < 0]
    if pos:
        r = pos[0]
        for a in pos[1:]:
            r = r + a
        for a in neg:
            r = r - a
        return r, 1.0
    if neg:
        r = neg[0]
        for a in neg[1:]:
            r = r + a
        return r, -1.0
    return None, 1.0


def _cmulc(x, w):
    wr, wi = float(np.real(w)), float(np.imag(w))
    if abs(wr) < 1e-15:
        wr = 0.0
    if abs(wi) < 1e-15:
        wi = 0.0
    xr, xi = x
    if wr != 0.0 and abs(abs(wr) - abs(wi)) < 1e-12:
        sr, si, c = np.sign(wr), np.sign(wi), abs(wr)
        re, sre = _signed_sum([(sr, xr), (-si, xi)])
        im, sim = _signed_sum([(si, xr), (sr, xi)])
        return (_scale(re, sre * c), _scale(im, sim * c))
    re = _sub(_scale(xr, wr), _scale(xi, wi))
    im = _add(_scale(xr, wi), _scale(xi, wr))
    return (re, im)


def _cdiff_mulc(x, y, w):
    wr, wi = float(np.real(w)), float(np.imag(w))
    (xr, xi), (yr, yi) = x, y
    unit = {(1, 0): ((1, xr, yr), (1, xi, yi)), (-1, 0): ((-1, xr, yr), (-1, xi, yi)),
            (0, 1): ((-1, xi, yi), (1, xr, yr)), (0, -1): ((1, xi, yi), (-1, xr, yr))}
    key = (int(round(wr)), int(round(wi)))
    if abs(wr - key[0]) < 1e-12 and abs(wi - key[1]) < 1e-12 and key in unit:
        parts = []
        for sg, a, b in unit[key]:
            v, sv = _signed_sum([(sg, a), (-sg, b)])
            parts.append(_scale(v, sv))
        return tuple(parts)
    return _cmulc(_csub(x, y), w)


def _cmul(x, y):
    xr, xi = x
    yr, yi = y
    re = _sub(None if xr is None else xr * yr, None if xi is None else xi * yi)
    im = _add(None if xr is None else xr * yi, None if xi is None else xi * yr)
    return (re, im)


def _fft_dif(xs, sign):
    n = len(xs)
    if n == 1:
        return xs
    half = n // 2
    a = [_cadd(xs[i], xs[i + half]) for i in range(half)]
    b = [_cdiff_mulc(xs[i], xs[i + half], np.exp(sign * 2j * np.pi * i / n)) for i in range(half)]
    ev = _fft_dif(a, sign)
    od = _fft_dif(b, sign)
    out = [None] * n
    out[0::2] = ev
    out[1::2] = od
    return out


def _dft_mats():
    n = np.arange(DFT_LANE)
    ang = 2.0 * np.pi * np.outer(n, n) / DFT_LANE
    fr, fi = np.cos(ang), -np.sin(ang)
    fwd = np.block([[fr, fi], [-fi, fr]]).astype(np.float32)
    inv = np.block([[fr, -fi], [fi, fr]]).astype(np.float32)
    return fwd, inv


def _twiddle(n1):
    n = n1 * DFT_LANE
    ang = 2.0 * np.pi * np.outer(np.arange(n1), np.arange(DFT_LANE)) / n
    tw = np.stack([np.cos(ang), -np.sin(ang)], axis=1)
    return np.ascontiguousarray(np.broadcast_to(tw[:, :, None, :], (n1, 2, SUBLANES, DFT_LANE))).astype(np.float32)


def _filter_features(L):
    t = np.linspace(0.0, 1.0, L)[:, None]
    bands = (FILTER_EMB - 1) // 2
    f = np.linspace(1e-4, bands - 1, bands)
    ang = (2.0 * np.pi / L) * np.arange(L)[:, None] * f[None, :]
    z = np.concatenate([t, np.cos(ang), -np.sin(ang)], axis=-1)
    zp = np.zeros((L, LANES), np.float32)
    zp[:, :FILTER_EMB] = z
    return zp, t[:, 0].astype(np.float32)


def _rope_tables(L):
    pos = np.arange(L, dtype=np.float64)
    inv_freq = ROPE_THETA ** (-np.arange(0, QK_ROPE, 2, dtype=np.float64) / QK_ROPE)
    ang = pos[:, None] * inv_freq[None, :]
    c, s = np.cos(ang), np.sin(ang)
    ctab = np.concatenate([np.ones((L, QK_NOPE)), c, c], axis=1)
    stab = np.concatenate([np.zeros((L, QK_NOPE)), -s, s], axis=1)
    return ctab.astype(np.float32), stab.astype(np.float32)


def _in_proj_kernel(x_ref, gn_ref, wt_hy_ref, w_cq_ref, w_ckv_ref, w_rest_ref, gcq_ref, gckv_ref,
                    ct_ref, st_ref, ctt_ref, stt_ref, wqt_ref, wk_ref, wvt_ref,
                    gqc_ref, gqsc_ref, gk_ref, gks_ref, ones_ref,
                    hyt_ref, zat_ref, gate_ref, qt_ref, k_ref, vt_ref):
    x = x_ref[0]
    ms = jnp.mean(x * x, axis=-1, keepdims=True)
    h = (x * lax.rsqrt(ms + EPS) * gn_ref[...]).astype(BF16)

    def latent(c, g_ref):
        r = lax.rsqrt(jnp.mean(c * c, axis=-1, keepdims=True) + EPS)
        return (c * r * g_ref[...]).astype(BF16)

    cqr = jnp.dot(h, w_cq_ref[...], preferred_element_type=F32)
    cq = latent(cqr[:, :Q_LORA], gcq_ref)
    ckv = latent(jnp.dot(h, w_ckv_ref[...], preferred_element_type=F32), gckv_ref)
    kr_a = cqr[:, Q_LORA:]
    qt_all = lax.dot_general(wqt_ref[...], cq, _NT, preferred_element_type=F32)
    ka_all = jnp.dot(ckv, wk_ref[...], preferred_element_type=F32)
    vt_all = lax.dot_general(wvt_ref[...], ckv, _NT, preferred_element_type=F32) + ones_ref[...]

    hyt_ref[0] = lax.dot_general(wt_hy_ref[...], h, _NT, preferred_element_type=F32).astype(BF16)
    rest = jnp.dot(h, w_rest_ref[...], preferred_element_type=F32)
    zat_ref[0] = rest[:, :ATTN_WIDTH].astype(BF16)
    gate_ref[0] = rest[:, ATTN_WIDTH:].astype(BF16)

    half = QK_ROPE // 2
    scale = math.log2(math.e) / math.sqrt(QK_DIM)
    qc, qs = gqc_ref[...] * ctt_ref[...], gqsc_ref[...] * stt_ref[...]
    kc, ks = gk_ref[...] * ct_ref[...], gks_ref[...] * st_ref[...]
    lane = lax.broadcasted_iota(jnp.int32, kr_a.shape, 1)
    kr_w = jnp.where(lane < QK_NOPE + half, pltpu.roll(kr_a, LANES - half, 1), pltpu.roll(kr_a, half, 1))
    for hd in range(MLA_HEADS):
        qa = qt_all[hd * QK_DIM:(hd + 1) * QK_DIM]
        qw = jnp.concatenate([qa[:QK_NOPE], qa[QK_NOPE + half:], qa[QK_NOPE:QK_NOPE + half]], axis=0)
        r = lax.rsqrt(jnp.mean(qa * qa, axis=0, keepdims=True) + EPS) * scale
        qt_ref[0, hd] = ((qa * qc + qw * qs) * r).astype(BF16)
        ka = ka_all[:, hd * LANES:(hd + 1) * LANES] + kr_a
        rk = lax.rsqrt(jnp.sum(ka * ka, axis=-1, keepdims=True) * (1.0 / QK_DIM) + EPS)
        k_ref[0, hd] = ((ka * kc + kr_w * ks) * rk)[:, :QK_DIM].astype(BF16)
        vt_ref[0, hd] = vt_all[hd * V_ROWS:(hd + 1) * V_ROWS].astype(BF16)


def _in_proj(x, gn, wt_hy, w_cq, w_ckv, w_rest, gcq, gckv, ctab, stab, ctab_t, stab_t,
             wqt, wk, wvt, gqc, gqsc, gk, gks, tl):
    ones = np.zeros((MLA_HEADS, V_ROWS, 1), np.float32)
    ones[:, V_DIM, 0] = 1.0
    ones = jnp.asarray(ones.reshape(MLA_HEADS * V_ROWS, 1))
    B, L, _ = x.shape
    n_rest = w_rest.shape[1]
    tab = pl.BlockSpec((tl, LANES), lambda b, i: (i, 0))
    tab_t = pl.BlockSpec((QK_DIM, tl), lambda b, i: (0, i))
    consts = (gn, wt_hy, w_cq, w_ckv, w_rest, gcq, gckv)
    qkv_consts = (wqt, wk, wvt, gqc, gqsc, gk, gks, ones)
    return pl.pallas_call(
        _in_proj_kernel,
        grid=(B, L // tl),
        in_specs=[pl.BlockSpec((1, tl, D_MODEL), lambda b, i: (b, i, 0))]
        + [_resident(a.shape) for a in consts] + [tab, tab, tab_t, tab_t] + [_resident(a.shape) for a in qkv_consts],
        out_specs=[
            pl.BlockSpec((1, 4 * HY_WIDTH, tl), lambda b, i: (b, 0, i)),
            pl.BlockSpec((1, tl, ATTN_WIDTH), lambda b, i: (b, i, 0)),
            pl.BlockSpec((1, tl, 2 * D_MODEL), lambda b, i: (b, i, 0)),
            pl.BlockSpec((1, MLA_HEADS, QK_DIM, tl), lambda b, i: (b, 0, 0, i)),
            pl.BlockSpec((1, MLA_HEADS, tl, QK_DIM), lambda b, i: (b, 0, i, 0)),
            pl.BlockSpec((1, MLA_HEADS, V_ROWS, tl), lambda b, i: (b, 0, 0, i)),
        ],
        out_shape=[
            jax.ShapeDtypeStruct((B, 4 * HY_WIDTH, L), BF16),
            jax.ShapeDtypeStruct((B, L, ATTN_WIDTH), BF16),
            jax.ShapeDtypeStruct((B, L, n_rest - ATTN_WIDTH), BF16),
            jax.ShapeDtypeStruct((B, MLA_HEADS, QK_DIM, L), BF16),
            jax.ShapeDtypeStruct((B, MLA_HEADS, L, QK_DIM), BF16),
            jax.ShapeDtypeStruct((B, MLA_HEADS, V_ROWS, L), BF16),
        ],
        compiler_params=_cparams(("parallel", "parallel")),
        name="in_proj",
    )(x, *consts, ctab, stab, ctab_t, stab_t, *qkv_consts)


def _fmlp_kernel(zt_ref, w1t_ref, b1_ref, f1_ref, w2t_ref, b2_ref, f2_ref, out_ref):
    hp = lax.Precision.HIGHEST
    h = jnp.sin(f1_ref[...] * (jnp.dot(w1t_ref[...], zt_ref[0], precision=hp, preferred_element_type=F32) + b1_ref[...]))
    out_ref[0] = jnp.sin(f2_ref[...] * (jnp.dot(w2t_ref[...], h, precision=hp, preferred_element_type=F32) + b2_ref[...]))


def _fmlp(zt, w1t, b1, f1, w2t, b2, f2):
    _, _, L = zt.shape
    return pl.pallas_call(
        _fmlp_kernel,
        grid=(2,),
        in_specs=[pl.BlockSpec((1, LANES, L), lambda d: (d, 0, 0)), _resident(w1t.shape), _resident(b1.shape),
                  _resident(f1.shape), _resident(w2t.shape), _resident(b2.shape), _resident(f2.shape)],
        out_specs=pl.BlockSpec((1, FILTER_HIDDEN, L), lambda d: (d, 0, 0)),
        out_shape=jax.ShapeDtypeStruct((2, FILTER_HIDDEN, L), F32),
        compiler_params=_cparams(("parallel",)),
        name="fmlp",
    )(zt, w1t, b1, f1, w2t, b2, f2)


def _stage_a_forward(load, g_scr, tw_ref, n1, rows_total):
    def body(r, carry):
        rs = pl.ds(pl.multiple_of(r * SUBLANES, SUBLANES), SUBLANES)
        for half in range(DFT_LANE // LANES):
            ls = slice(half * LANES, (half + 1) * LANES)
            li = slice(DFT_LANE + half * LANES, DFT_LANE + (half + 1) * LANES)
            xs = [load(i, rs, ls) for i in range(n1)]
            xs = [(None, None) if x is None else x for x in xs]
            X = _fft_dif(xs, -1.0)
            for k1 in range(n1):
                if tw_ref is None:
                    yr, yi = X[k1]
                else:
                    yr, yi = _cmul(X[k1], (tw_ref[k1, 0, :, ls], tw_ref[k1, 1, :, ls]))
                g_scr[k1, rs, ls] = yr
                g_scr[k1, rs, li] = yi
        return carry

    lax.fori_loop(0, rows_total // SUBLANES, body, 0)


def _fspec_kernel(h_ref, t_ref, w3f_ref, w3b_ref, dl_ref, wf_ref, tw_ref, out_ref, k_scr, g_scr,
                  *, L, n1, rows, group):
    dl = dl_ref[...]
    for d, w_ref in enumerate((w3f_ref, w3b_ref)):
        k = jnp.dot(w_ref[...].astype(BF16), h_ref[d].astype(BF16), preferred_element_type=F32)
        k_scr[:, d * L:(d + 1) * L] = k * jnp.exp(-(dl * t_ref[d]))

    def load(i, rs, ls):
        return (k_scr[rs, i * DFT_LANE + ls.start:i * DFT_LANE + ls.stop], None)

    _stage_a_forward(load, g_scr, tw_ref, n1, rows)
    inv_n = 1.0 / (n1 * DFT_LANE)
    w = wf_ref[...]
    for g0 in range(0, n1, group):
        lhs = g_scr[g0:g0 + group].reshape(group * rows, 2 * DFT_LANE).astype(BF16)
        res = jnp.dot(lhs, w, preferred_element_type=F32)
        out_ref[0, g0:g0 + group] = res.reshape(group, rows, 2 * DFT_LANE) * inv_n


def _fspec(h2t, t2, w3f, w3b, dl, wf, tw, L, n1, rows, group):
    n_rows = w3f.shape[0]
    tiles_per_order = HY_WIDTH // rows
    row_spec = lambda shape: pl.BlockSpec(shape, lambda i: (i, 0))
    return pl.pallas_call(
        functools.partial(_fspec_kernel, L=L, n1=n1, rows=rows, group=group),
        grid=(n_rows // rows,),
        in_specs=[_resident(h2t.shape), _resident(t2.shape),
                  row_spec((rows, FILTER_HIDDEN)), row_spec((rows, FILTER_HIDDEN)), row_spec((rows, 1)),
                  _resident(wf.shape), _resident(tw.shape)],
        out_specs=pl.BlockSpec((1, n1, rows, 2 * DFT_LANE),
                               lambda i: (i // tiles_per_order, 0, i % tiles_per_order, 0)),
        out_shape=jax.ShapeDtypeStruct((n_rows // HY_WIDTH, n1, HY_WIDTH, 2 * DFT_LANE), F32),
        scratch_shapes=[pltpu.VMEM((rows, 2 * L), F32), pltpu.VMEM((n1, rows, 2 * DFT_LANE), F32)],
        compiler_params=_cparams(("parallel",)),
        name="fspec",
    )(h2t, t2, w3f, w3b, dl, wf, tw)


def _hyena_kernel(hy_ref, wsh_ref, bsh_ref, ksp_ref, bias_ref, wf_ref, wi_ref, tw_ref, out_ref,
                  u_scr, x1_scr, x2_scr, g_scr, *, L, n1, ct, group):
    lane = lax.broadcasted_iota(jnp.int32, (ct, LANES), 1)

    def short_conv(g, dst):
        for bi in range(2):
            p = hy_ref[bi, g].astype(F32)
            left = pltpu.roll(p, 1, 1)
            left = jnp.concatenate([jnp.where(lane == 0, 0.0, left[:, :LANES]), left[:, LANES:]], axis=1)
            right = pltpu.roll(p, L - 1, 1)
            right = jnp.concatenate([right[:, :L - LANES], jnp.where(lane == LANES - 1, 0.0, right[:, L - LANES:])],
                                    axis=1)
            conv = wsh_ref[g, 0] * left + wsh_ref[g, 1] * p + wsh_ref[g, 2] * right + bsh_ref[g]
            if g == 2:
                z = hy_ref[bi, 3].astype(F32)
                conv = conv * (z * jax.nn.sigmoid(z))
            dst[bi] = conv

    short_conv(0, u_scr)
    gate_scr = (x1_scr, x2_scr)

    half_blocks = n1 // 2
    wf = wf_ref[...]
    wi = wi_ref[...]
    re, im = slice(0, DFT_LANE), slice(DFT_LANE, 2 * DFT_LANE)

    def mxu_dft(x, w):
        res = jnp.dot(x.reshape(group * ct, 2 * DFT_LANE).astype(BF16), w, preferred_element_type=F32)
        res = res.reshape(group, ct, 2 * DFT_LANE)
        return res[..., re], res[..., im]

    for order in range(2):
        def load(i, rs, ls):
            if i >= half_blocks:
                return None
            cs = slice(i * DFT_LANE + ls.start, i * DFT_LANE + ls.stop)
            return (u_scr[0, rs, cs], u_scr[1, rs, cs])

        _stage_a_forward(load, g_scr, None, n1, ct)

        def twiddled(gs):
            g = g_scr[gs]
            gr, gi = g[..., re], g[..., im]
            twr = tw_ref[gs, 0, 0:1, :]
            twi = tw_ref[gs, 1, 0:1, :]
            return jnp.concatenate([gr * twr - gi * twi, gr * twi + gi * twr], axis=-1)

        def untwiddle(gs, q):
            qr, qi = q
            twr = tw_ref[gs, 0, 0:1, :]
            twi = tw_ref[gs, 1, 0:1, :]
            g_scr[gs, :, re] = qr * twr + qi * twi
            g_scr[gs, :, im] = qi * twr - qr * twi

        groups = [slice(g0, g0 + group) for g0 in range(0, n1, group)]
        spectra = [mxu_dft(twiddled(gs), wf) for gs in groups]
        short_conv(order + 1, gate_scr[order])
        prev = None
        for gs, (hr, hi) in zip(groups, spectra):
            kr, ki = ksp_ref[order, gs, :, re], ksp_ref[order, gs, :, im]
            q = mxu_dft(jnp.concatenate([hr * kr - hi * ki, hr * ki + hi * kr], axis=-1), wi)
            if prev is not None:
                untwiddle(*prev)
            prev = (gs, q)
        untwiddle(*prev)

        def body(r, carry, order=order):
            rs = pl.ds(pl.multiple_of(r * SUBLANES, SUBLANES), SUBLANES)
            bias = bias_ref[order, rs, :]
            for half in range(DFT_LANE // LANES):
                ls = slice(half * LANES, (half + 1) * LANES)
                li = slice(DFT_LANE + half * LANES, DFT_LANE + (half + 1) * LANES)
                q = [(g_scr[k1, rs, ls], g_scr[k1, rs, li]) for k1 in range(n1)]
                ev = _fft_dif(q[0::2], 1.0)
                od = _fft_dif(q[1::2], 1.0)
                for i in range(half_blocks):
                    y = _cadd(ev[i], _cmulc(od[i], np.exp(2j * np.pi * i / n1)))
                    cs = slice(i * DFT_LANE + ls.start, i * DFT_LANE + ls.stop)
                    for bi in range(2):
                        u = u_scr[bi, rs, cs]
                        conv = y[bi] + bias * u
                        if order == 0:
                            u_scr[bi, rs, cs] = x1_scr[bi, rs, cs] * conv
                        else:
                            out_ref[bi, rs, cs] = x2_scr[bi, rs, cs] * conv
            return carry

        lax.fori_loop(0, ct // SUBLANES, body, 0)


def _hyena(hyt4, wsh, bsh, kspec, bias, wf, wi, tw, n1, ct, group):
    B, _, C, L = hyt4.shape
    return pl.pallas_call(
        functools.partial(_hyena_kernel, L=L, n1=n1, ct=ct, group=group),
        grid=(C // ct, B // 2),
        in_specs=[
            pl.BlockSpec((2, 4, ct, L), lambda c, p: (p, 0, c, 0)),
            pl.BlockSpec((3, 3, ct, 1), lambda c, p: (0, 0, c, 0)),
            pl.BlockSpec((3, ct, 1), lambda c, p: (0, c, 0)),
            pl.BlockSpec((2, n1, ct, 2 * DFT_LANE), lambda c, p: (0, 0, c, 0)),
            pl.BlockSpec((2, ct, 1), lambda c, p: (0, c, 0)),
            _resident(wf.shape), _resident(wi.shape), _resident(tw.shape),
        ],
        out_specs=pl.BlockSpec((2, ct, L), lambda c, p: (p, c, 0)),
        out_shape=jax.ShapeDtypeStruct((B, C, L), F32),
        scratch_shapes=[pltpu.VMEM((2, ct, L), F32), pltpu.VMEM((2, ct, L), F32), pltpu.VMEM((2, ct, L), F32),
                        pltpu.VMEM((n1, ct, 2 * DFT_LANE), F32)],
        compiler_params=_cparams(("parallel", "parallel")),
        name="hyena",
    )(hyt4, wsh, bsh, kspec, bias, wf, wi, tw)


def _attn_kernel(qt_ref, k_ref, vt_ref, o_ref, *, L, tq):
    units = [(j, t0) for t0 in range(0, tq, Q_TILE) for j in range(2)]
    starts = list(range(0, L, KV_CHUNK))

    def scores(u, c0):
        j, t0 = u
        return jnp.dot(k_ref[0, j, c0:c0 + KV_CHUNK, :], qt_ref[0, j, :, t0:t0 + Q_TILE],
                       preferred_element_type=F32)

    m = {u: None for u in units}
    acc = {u: None for u in units}
    cur = {u: scores(u, starts[0]) for u in units}
    for i, c0 in enumerate(starts):
        nxt = {}
        for u in units:
            if i + 1 < len(starts):
                nxt[u] = scores(u, starts[i + 1])
            s = cur[u]
            mc = jnp.max(s, axis=0, keepdims=True)
            m_new = mc if m[u] is None else jnp.maximum(m[u], mc)
            p = jnp.exp2(s - m_new).astype(BF16)
            pv = jnp.dot(vt_ref[0, u[0], :, c0:c0 + KV_CHUNK], p, preferred_element_type=F32)
            acc[u] = pv if m[u] is None else jnp.exp2(m[u] - m_new) * acc[u] + pv
            m[u] = m_new
        cur = nxt
    for j, t0 in units:
        a = acc[(j, t0)]
        o_ref[0, j * V_DIM:(j + 1) * V_DIM, t0:t0 + Q_TILE] = (a[:V_DIM] / a[V_DIM:V_DIM + 1]).astype(BF16)


def _attn(qt, k, vt, tq):
    B, _, L, _ = k.shape
    return pl.pallas_call(
        functools.partial(_attn_kernel, L=L, tq=tq),
        grid=(B, HEAD_PAIRS, L // tq),
        in_specs=[pl.BlockSpec((1, 2, QK_DIM, tq), lambda b, p, i: (b, p, 0, i)),
                  pl.BlockSpec((1, 2, L, QK_DIM), lambda b, p, i: (b, p, 0, 0)),
                  pl.BlockSpec((1, 2, V_ROWS, L), lambda b, p, i: (b, p, 0, 0))],
        out_specs=pl.BlockSpec((1, 2 * V_DIM, tq), lambda b, p, i: (b, p, i)),
        out_shape=jax.ShapeDtypeStruct((B, ATTN_WIDTH, L), BF16),
        compiler_params=_cparams(("parallel", "parallel", "parallel")),
        name="attn",
    )(qt, k, vt)


def _final_kernel(x_ref, yt_ref, at_ref, zat_ref, gate_ref, bg_ref, why_ref, wat_ref, wout_ref, o_ref, *, tl, sub):
    for r0 in range(0, tl, sub):
        rows = slice(r0, r0 + sub)
        y = yt_ref[0, :, rows].T.astype(BF16)
        u_hy = jnp.dot(y, why_ref[...], preferred_element_type=F32)
        z = zat_ref[0, rows, :].astype(F32)
        a = (at_ref[0, :, rows].astype(F32).T * (z * jax.nn.sigmoid(z))).astype(BF16)
        u_at = jnp.dot(a, wat_ref[...], preferred_element_type=F32)
        gates = jax.nn.sigmoid(gate_ref[0, rows, :].astype(F32) + bg_ref[...])
        merged = gates[:, :D_MODEL] * u_hy + gates[:, D_MODEL:] * u_at
        o_ref[0, rows, :] = x_ref[0, rows, :] + jnp.dot(merged.astype(BF16), wout_ref[...],
                                                         preferred_element_type=F32)


def _final(x, yt, at, zat, gate, bg, why, wat, wout, tl):
    B, L, _ = x.shape
    tok = lambda w: pl.BlockSpec((1, tl, w), lambda b, i: (b, i, 0))
    return pl.pallas_call(
        functools.partial(_final_kernel, tl=tl, sub=min(512, tl)),
        grid=(B, L // tl),
        in_specs=[tok(D_MODEL), pl.BlockSpec((1, HY_WIDTH, tl), lambda b, i: (b, 0, i)),
                  pl.BlockSpec((1, ATTN_WIDTH, tl), lambda b, i: (b, 0, i)), tok(ATTN_WIDTH), tok(2 * D_MODEL),
                  _resident(bg.shape), _resident(why.shape), _resident(wat.shape), _resident(wout.shape)],
        out_specs=tok(D_MODEL),
        out_shape=jax.ShapeDtypeStruct((B, L, D_MODEL), F32),
        compiler_params=_cparams(("parallel", "parallel")),
        name="final",
    )(x, yt, at, zat, gate, bg, why, wat, wout)


def _layer(x, g_norm, w_in, b_gate, w_short, b_short, w_f1, b_f1, freq_1, w_f2, b_f2, freq_2, w_f3,
           hy_bias, w_hy_out, g_cq, w_uq, g_ckv, w_ukv, g_qn, g_kn, w_attn_out, w_out):
    B, L, _ = x.shape
    n = 2 * L
    n1 = n // DFT_LANE
    assert B % 2 == 0 and n1 * DFT_LANE == n and n1 >= 2 and (n1 & (n1 - 1)) == 0
    tl = min(512, L)
    ct = 64
    group = min(8, n1)

    o_z, o_cq, o_ckv, o_kr, o_za = 3 * HY_WIDTH, 4 * HY_WIDTH, 4 * HY_WIDTH + Q_LORA, 4 * HY_WIDTH + Q_LORA + KV_LORA, \
        4 * HY_WIDTH + Q_LORA + KV_LORA + QK_ROPE
    del o_z
    w_in = w_in.astype(BF16)
    wt_hy = w_in[:, :4 * HY_WIDTH].T
    half = QK_ROPE // 2
    w_cq = jnp.concatenate([w_in[:, o_cq:o_ckv], jnp.zeros((D_MODEL, QK_NOPE), BF16), w_in[:, o_kr:o_za],
                            jnp.zeros((D_MODEL, LANES - QK_DIM), BF16)], axis=1)
    w_ckv = w_in[:, o_ckv:o_kr]
    w_rest = w_in[:, o_za:]

    perm = np.concatenate([np.arange(QK_NOPE), QK_NOPE + half + np.arange(half), QK_NOPE + np.arange(half)])
    wqt = w_uq.T.astype(BF16)
    wkv = w_ukv.reshape(KV_LORA, MLA_HEADS, QK_NOPE + V_DIM)
    wk = jnp.pad(wkv[:, :, :QK_NOPE], ((0, 0), (0, 0), (0, LANES - QK_NOPE)))
    wk = wk.reshape(KV_LORA, MLA_HEADS * LANES).astype(BF16)
    wvt = jnp.pad(wkv[:, :, QK_NOPE:], ((0, 0), (0, 0), (0, V_ROWS - V_DIM)))
    wvt = wvt.reshape(KV_LORA, MLA_HEADS * V_ROWS).T.astype(BF16)
    pad_lanes = lambda g: jnp.pad(g, ((0, 0), (0, LANES - QK_DIM)))
    gqc = g_qn.reshape(QK_DIM, 1)
    gqsc = gqc[perm]
    gk = g_kn.reshape(1, QK_DIM)
    gks = pad_lanes(gk[:, perm])
    gk = pad_lanes(gk)

    ctab, stab = _rope_tables(L)
    pad_tab = lambda tab: np.pad(tab, ((0, 0), (0, LANES - QK_DIM)))
    hyt, zat, gate, qt, k, vt = _in_proj(
        x, g_norm.reshape(1, D_MODEL), wt_hy, w_cq, w_ckv, w_rest, g_cq.reshape(1, Q_LORA),
        g_ckv.reshape(1, KV_LORA), jnp.asarray(pad_tab(ctab)), jnp.asarray(pad_tab(stab)),
        jnp.asarray(np.ascontiguousarray(ctab.T)), jnp.asarray(np.ascontiguousarray(stab.T)),
        wqt, wk, wvt, gqc, gqsc, gk, gks, tl)

    zfeat, t = _filter_features(L)
    w1 = jnp.concatenate([w_f1, jnp.zeros((LANES - FILTER_EMB, FILTER_HIDDEN), F32)], axis=0)
    w3 = w_f3.reshape(FILTER_HIDDEN, 2, 2, HY_WIDTH)
    w3f = w3[:, :, 0].reshape(FILTER_HIDDEN, 2 * HY_WIDTH).T
    w3b = w3[:, :, 1].reshape(FILTER_HIDDEN, 2 * HY_WIDTH).T
    deltas = np.abs(np.linspace(MIN_DECAY, MAX_DECAY, HY_WIDTH))
    dl = np.tile(deltas, 2).reshape(2 * HY_WIDTH, 1).astype(np.float32)
    col = lambda v: v.reshape(FILTER_HIDDEN, 1)
    zt = np.ascontiguousarray(np.stack([zfeat.T, zfeat[::-1].T]))
    t2 = np.ascontiguousarray(np.stack([t, t[::-1]]).reshape(2, 1, L))
    h2t = _fmlp(jnp.asarray(zt), w1.T, col(b_f1), col(freq_1), w_f2.T, col(b_f2), col(freq_2))
    wf, wi = (jnp.asarray(w).astype(BF16) for w in _dft_mats())
    tw = jnp.asarray(_twiddle(n1))
    kspec = _fspec(h2t, jnp.asarray(t2), w3f, w3b, jnp.asarray(dl), wf, tw, L, n1, ct, group)

    wsh = w_short.reshape(3, 3, HY_WIDTH).transpose(1, 0, 2)[..., None]
    bsh = b_short.reshape(3, HY_WIDTH, 1)
    yt = _hyena(hyt.reshape(B, 4, HY_WIDTH, L), wsh, bsh, kspec, hy_bias.reshape(2, HY_WIDTH, 1),
                wf, wi, tw, n1, ct, group)

    at = _attn(qt, k, vt, min(2048, L))

    return _final(x, yt, at, zat, gate, b_gate.reshape(1, 2 * D_MODEL), w_hy_out.astype(BF16),
                  w_attn_out.astype(BF16), w_out.astype(BF16), min(1024, L))


def kernel(x, g_norm, w_in, b_gate, w_short, b_short, w_f1, b_f1, freq_1, w_f2, b_f2, freq_2, w_f3, hy_bias, w_hy_out, g_cq, w_uq, g_ckv, w_ukv, g_qn, g_kn, w_attn_out, w_out):
    depth = g_norm.shape[0]
    for l in range(depth):
        x = _layer(x, g_norm[l], w_in[l], b_gate[l], w_short[l], b_short[l], w_f1[l], b_f1[l], freq_1[l], w_f2[l],
                   b_f2[l], freq_2[l], w_f3[l], hy_bias[l], w_hy_out[l], g_cq[l], w_uq[l], g_ckv[l], w_ukv[l],
                   g_qn[l], g_kn[l], w_attn_out[l], w_out[l])
    return x
```

```python
import functools
import math

import numpy as np
import jax
import jax.numpy as jnp
from jax import lax
from jax.experimental import pallas as pl
from jax.experimental.pallas import tpu as pltpu

D_MODEL = 1024
HY_WIDTH = 512
FILTER_EMB = 33
FILTER_HIDDEN = 64
DECAY_TARGET = 1e-2
FAST_DECAY = 0.3
SLOW_DECAY = 1.5
MIN_DECAY = math.log(DECAY_TARGET) / SLOW_DECAY
MAX_DECAY = math.log(DECAY_TARGET) / FAST_DECAY
MLA_HEADS = 8
HEAD_PAIRS = MLA_HEADS // 2
QK_NOPE = 64
QK_ROPE = 32
QK_DIM = QK_NOPE + QK_ROPE
V_DIM = 64
Q_LORA = 384
KV_LORA = 256
ATTN_WIDTH = MLA_HEADS * V_DIM
ROPE_THETA = 10000.0
EPS = 1e-6

LANES = 128
SUBLANES = 8
DFT_LANE = 256
V_ROWS = 80
KV_CHUNK = 256
Q_TILE = 256
VMEM_LIMIT = 56 * 1024 * 1024

F32 = jnp.float32
BF16 = jnp.bfloat16
_NT = (((1,), (1,)), ((), ()))
_NN = (((1,), (0,)), ((), ()))


def _cparams(sem):
    return pltpu.CompilerParams(dimension_semantics=sem, vmem_limit_bytes=VMEM_LIMIT)


def _resident(shape):
    nd = len(shape)
    return pl.BlockSpec(shape, lambda *_: (0,) * nd, pipeline_mode=pl.Buffered(1))


def _add(a, b):
    if a is None:
        return b
    if b is None:
        return a
    return a + b


def _sub(a, b):
    if b is None:
        return a
    if a is None:
        return -b
    return a - b


def _cadd(x, y):
    return (_add(x[0], y[0]), _add(x[1], y[1]))


def _csub(x, y):
    return (_sub(x[0], y[0]), _sub(x[1], y[1]))


def _scale(a, s):
    if a is None or s == 0.0:
        return None
    if s == 1.0:
        return a
    if s == -1.0:
        return -a
    return a * s


def _signed_sum(terms):
    pos = [a for sg, a in terms if a is not None and sg > 0]
    neg = [a for sg, a in terms if a is not None and sg < 0]
    if pos:
        r = pos[0]
        for a in pos[1:]:
            r = r + a
        for a in neg:
            r = r - a
        return r, 1.0
    if neg:
        r = neg[0]
        for a in neg[1:]:
            r = r + a
        return r, -1.0
    return None, 1.0


def _cmulc(x, w):
    wr, wi = float(np.real(w)), float(np.imag(w))
    if abs(wr) < 1e-15:
        wr = 0.0
    if abs(wi) < 1e-15:
        wi = 0.0
    xr, xi = x
    if wr != 0.0 and abs(abs(wr) - abs(wi)) < 1e-12:
        sr, si, c = np.sign(wr), np.sign(wi), abs(wr)
        re, sre = _signed_sum([(sr, xr), (-si, xi)])
        im, sim = _signed_sum([(si, xr), (sr, xi)])
        return (_scale(re, sre * c), _scale(im, sim * c))
    re = _sub(_scale(xr, wr), _scale(xi, wi))
    im = _add(_scale(xr, wi), _scale(xi, wr))
    return (re, im)


def _cdiff_mulc(x, y, w):
    wr, wi = float(np.real(w)), float(np.imag(w))
    (xr, xi), (yr, yi) = x, y
    unit = {(1, 0): ((1, xr, yr), (1, xi, yi)), (-1, 0): ((-1, xr, yr), (-1, xi, yi)),
            (0, 1): ((-1, xi, yi), (1, xr, yr)), (0, -1): ((1, xi, yi), (-1, xr, yr))}
    key = (int(round(wr)), int(round(wi)))
    if abs(wr - key[0]) < 1e-12 and abs(wi - key[1]) < 1e-12 and key in unit:
        parts = []
        for sg, a, b in unit[key]:
            v, sv = _signed_sum([(sg, a), (-sg, b)])
            parts.append(_scale(v, sv))
        return tuple(parts)
    return _cmulc(_csub(x, y), w)


def _cmul(x, y):
    xr, xi = x
    yr, yi = y
    re = _sub(None if xr is None else xr * yr, None if xi is None else xi * yi)
    im = _add(None if xr is None else xr * yi, None if xi is None else xi * yr)
    return (re, im)


def _fft_dif(xs, sign):
    n = len(xs)
    if n == 1:
        return xs
    half = n // 2
    a = [_cadd(xs[i], xs[i + half]) for i in range(half)]
    b = [_cdiff_mulc(xs[i], xs[i + half], np.exp(sign * 2j * np.pi * i / n)) for i in range(half)]
    ev = _fft_dif(a, sign)
    od = _fft_dif(b, sign)
    out = [None] * n
    out[0::2] = ev
    out[1::2] = od
    return out


def _dft_mats():
    n = np.arange(DFT_LANE)
    ang = 2.0 * np.pi * np.outer(n, n) / DFT_LANE
    fr, fi = np.cos(ang), -np.sin(ang)
    fwd = np.block([[fr, fi], [-fi, fr]]).astype(np.float32)
    inv = np.block([[fr, -fi], [fi, fr]]).astype(np.float32)
    return fwd, inv


def _twiddle(n1):
    n = n1 * DFT_LANE
    ang = 2.0 * np.pi * np.outer(np.arange(n1), np.arange(DFT_LANE)) / n
    tw = np.stack([np.cos(ang), -np.sin(ang)], axis=1)
    return np.ascontiguousarray(np.broadcast_to(tw[:, :, None, :], (n1, 2, SUBLANES, DFT_LANE))).astype(np.float32)


def _filter_features(L):
    t = np.linspace(0.0, 1.0, L)[:, None]
    bands = (FILTER_EMB - 1) // 2
    f = np.linspace(1e-4, bands - 1, bands)
    ang = (2.0 * np.pi / L) * np.arange(L)[:, None] * f[None, :]
    z = np.concatenate([t, np.cos(ang), -np.sin(ang)], axis=-1)
    zp = np.zeros((L, LANES), np.float32)
    zp[:, :FILTER_EMB] = z
    return zp, t[:, 0].astype(np.float32)


def _rope_tables(L):
    pos = np.arange(L, dtype=np.float64)
    inv_freq = ROPE_THETA ** (-np.arange(0, QK_ROPE, 2, dtype=np.float64) / QK_ROPE)
    ang = pos[:, None] * inv_freq[None, :]
    c, s = np.cos(ang), np.sin(ang)
    ctab = np.concatenate([np.ones((L, QK_NOPE)), c, c], axis=1)
    stab = np.concatenate([np.zeros((L, QK_NOPE)), -s, s], axis=1)
    return ctab.astype(np.float32), stab.astype(np.float32)


def _in_proj_kernel(x_ref, gn_ref, wt_hy_ref, w_cq_ref, w_ckv_ref, w_za_ref, gcq_ref, gckv_ref,
                    ct_ref, st_ref, ctt_ref, stt_ref, wqt_ref, wk_ref, wvt_ref,
                    gqc_ref, gqsc_ref, gk_ref, gks_ref, ones_ref,
                    hyt_ref, zat_ref, qt_ref, k_ref, vt_ref):
    x = x_ref[0]
    ms = jnp.mean(x * x, axis=-1, keepdims=True)
    h = (x * lax.rsqrt(ms + EPS) * gn_ref[...]).astype(BF16)

    def latent(c, g_ref):
        r = lax.rsqrt(jnp.mean(c * c, axis=-1, keepdims=True) + EPS)
        return (c * r * g_ref[...]).astype(BF16)

    cqr = jnp.dot(h, w_cq_ref[...], preferred_element_type=F32)
    cq = latent(cqr[:, :Q_LORA], gcq_ref)
    ckv = latent(jnp.dot(h, w_ckv_ref[...], preferred_element_type=F32), gckv_ref)
    kr_a = cqr[:, Q_LORA:]
    qt_all = lax.dot_general(wqt_ref[...], cq, _NT, preferred_element_type=F32)
    ka_all = jnp.dot(ckv, wk_ref[...], preferred_element_type=F32)
    vt_all = lax.dot_general(wvt_ref[...], ckv, _NT, preferred_element_type=F32) + ones_ref[...]

    hyt_ref[0] = lax.dot_general(wt_hy_ref[...], h, _NT, preferred_element_type=F32).astype(BF16)
    zat_ref[0] = jnp.dot(h, w_za_ref[...], preferred_element_type=F32).astype(BF16)

    half = QK_ROPE // 2
    scale = math.log2(math.e) / math.sqrt(QK_DIM)
    qc, qs = gqc_ref[...] * ctt_ref[...], gqsc_ref[...] * stt_ref[...]
    kc, ks = gk_ref[...] * ct_ref[...], gks_ref[...] * st_ref[...]
    lane = lax.broadcasted_iota(jnp.int32, kr_a.shape, 1)
    kr_w = jnp.where(lane < QK_NOPE + half, pltpu.roll(kr_a, LANES - half, 1), pltpu.roll(kr_a, half, 1))
    for hd in range(MLA_HEADS):
        qa = qt_all[hd * QK_DIM:(hd + 1) * QK_DIM]
        qw = jnp.concatenate([qa[:QK_NOPE], qa[QK_NOPE + half:], qa[QK_NOPE:QK_NOPE + half]], axis=0)
        r = lax.rsqrt(jnp.mean(qa * qa, axis=0, keepdims=True) + EPS) * scale
        qt_ref[0, hd] = ((qa * qc + qw * qs) * r).astype(BF16)
        ka = ka_all[:, hd * LANES:(hd + 1) * LANES] + kr_a
        rk = lax.rsqrt(jnp.sum(ka * ka, axis=-1, keepdims=True) * (1.0 / QK_DIM) + EPS)
        k_ref[0, hd] = ((ka * kc + kr_w * ks) * rk)[:, :QK_DIM].astype(BF16)
        vt_ref[0, hd] = vt_all[hd * V_ROWS:(hd + 1) * V_ROWS].astype(BF16)


def _in_proj(x, gn, wt_hy, w_cq, w_ckv, w_za, gcq, gckv, ctab, stab, ctab_t, stab_t,
             wqt, wk, wvt, gqc, gqsc, gk, gks, tl):
    ones = np.zeros((MLA_HEADS, V_ROWS, 1), np.float32)
    ones[:, V_DIM, 0] = 1.0
    ones = jnp.asarray(ones.reshape(MLA_HEADS * V_ROWS, 1))
    B, L, _ = x.shape
    tab = pl.BlockSpec((tl, LANES), lambda b, i: (i, 0))
    tab_t = pl.BlockSpec((QK_DIM, tl), lambda b, i: (0, i))
    consts = (gn, wt_hy, w_cq, w_ckv, w_za, gcq, gckv)
    qkv_consts = (wqt, wk, wvt, gqc, gqsc, gk, gks, ones)
    return pl.pallas_call(
        _in_proj_kernel,
        grid=(B, L // tl),
        in_specs=[pl.BlockSpec((1, tl, D_MODEL), lambda b, i: (b, i, 0))]
        + [_resident(a.shape) for a in consts] + [tab, tab, tab_t, tab_t] + [_resident(a.shape) for a in qkv_consts],
        out_specs=[
            pl.BlockSpec((1, 4 * HY_WIDTH, tl), lambda b, i: (b, 0, i)),
            pl.BlockSpec((1, tl, ATTN_WIDTH), lambda b, i: (b, i, 0)),
            pl.BlockSpec((1, MLA_HEADS, QK_DIM, tl), lambda b, i: (b, 0, 0, i)),
            pl.BlockSpec((1, MLA_HEADS, tl, QK_DIM), lambda b, i: (b, 0, i, 0)),
            pl.BlockSpec((1, MLA_HEADS, V_ROWS, tl), lambda b, i: (b, 0, 0, i)),
        ],
        out_shape=[
            jax.ShapeDtypeStruct((B, 4 * HY_WIDTH, L), BF16),
            jax.ShapeDtypeStruct((B, L, ATTN_WIDTH), BF16),
            jax.ShapeDtypeStruct((B, MLA_HEADS, QK_DIM, L), BF16),
            jax.ShapeDtypeStruct((B, MLA_HEADS, L, QK_DIM), BF16),
            jax.ShapeDtypeStruct((B, MLA_HEADS, V_ROWS, L), BF16),
        ],
        compiler_params=_cparams(("parallel", "parallel")),
        name="in_proj",
    )(x, *consts, ctab, stab, ctab_t, stab_t, *qkv_consts)


def _fmlp_kernel(zt_ref, w1t_ref, b1_ref, f1_ref, w2t_ref, b2_ref, f2_ref, out_ref):
    hp = lax.Precision.HIGHEST
    h = jnp.sin(f1_ref[...] * (jnp.dot(w1t_ref[...], zt_ref[0], precision=hp, preferred_element_type=F32) + b1_ref[...]))
    out_ref[0] = jnp.sin(f2_ref[...] * (jnp.dot(w2t_ref[...], h, precision=hp, preferred_element_type=F32) + b2_ref[...]))


def _fmlp(zt, w1t, b1, f1, w2t, b2, f2):
    _, _, L = zt.shape
    return pl.pallas_call(
        _fmlp_kernel,
        grid=(2,),
        in_specs=[pl.BlockSpec((1, LANES, L), lambda d: (d, 0, 0)), _resident(w1t.shape), _resident(b1.shape),
                  _resident(f1.shape), _resident(w2t.shape), _resident(b2.shape), _resident(f2.shape)],
        out_specs=pl.BlockSpec((1, FILTER_HIDDEN, L), lambda d: (d, 0, 0)),
        out_shape=jax.ShapeDtypeStruct((2, FILTER_HIDDEN, L), F32),
        compiler_params=_cparams(("parallel",)),
        name="fmlp",
    )(zt, w1t, b1, f1, w2t, b2, f2)


def _stage_a_forward(load, g_scr, tw_ref, n1, rows_total):
    def body(r, carry):
        rs = pl.ds(pl.multiple_of(r * SUBLANES, SUBLANES), SUBLANES)
        for half in range(DFT_LANE // LANES):
            ls = slice(half * LANES, (half + 1) * LANES)
            li = slice(DFT_LANE + half * LANES, DFT_LANE + (half + 1) * LANES)
            xs = [load(i, rs, ls) for i in range(n1)]
            xs = [(None, None) if x is None else x for x in xs]
            X = _fft_dif(xs, -1.0)
            for k1 in range(n1):
                if tw_ref is None:
                    yr, yi = X[k1]
                else:
                    yr, yi = _cmul(X[k1], (tw_ref[k1, 0, :, ls], tw_ref[k1, 1, :, ls]))
                g_scr[k1, rs, ls] = yr
                g_scr[k1, rs, li] = yi
        return carry

    lax.fori_loop(0, rows_total // SUBLANES, body, 0)


def _fspec_kernel(h_ref, t_ref, w3f_ref, w3b_ref, dl_ref, wf_ref, tw_ref, out_ref, k_scr, g_scr,
                  *, L, n1, rows, group):
    dl = dl_ref[...]
    for d, w_ref in enumerate((w3f_ref, w3b_ref)):
        k = jnp.dot(w_ref[...].astype(BF16), h_ref[d].astype(BF16), preferred_element_type=F32)
        k_scr[:, d * L:(d + 1) * L] = k * jnp.exp(-(dl * t_ref[d]))

    def load(i, rs, ls):
        return (k_scr[rs, i * DFT_LANE + ls.start:i * DFT_LANE + ls.stop], None)

    _stage_a_forward(load, g_scr, tw_ref, n1, rows)
    inv_n = 1.0 / (n1 * DFT_LANE)
    w = wf_ref[...]
    for g0 in range(0, n1, group):
        lhs = g_scr[g0:g0 + group].reshape(group * rows, 2 * DFT_LANE).astype(BF16)
        res = jnp.dot(lhs, w, preferred_element_type=F32)
        out_ref[0, g0:g0 + group] = res.reshape(group, rows, 2 * DFT_LANE) * inv_n


def _fspec(h2t, t2, w3f, w3b, dl, wf, tw, L, n1, rows, group):
    n_rows = w3f.shape[0]
    tiles_per_order = HY_WIDTH // rows
    row_spec = lambda shape: pl.BlockSpec(shape, lambda i: (i, 0))
    return pl.pallas_call(
        functools.partial(_fspec_kernel, L=L, n1=n1, rows=rows, group=group),
        grid=(n_rows // rows,),
        in_specs=[_resident(h2t.shape), _resident(t2.shape),
                  row_spec((rows, FILTER_HIDDEN)), row_spec((rows, FILTER_HIDDEN)), row_spec((rows, 1)),
                  _resident(wf.shape), _resident(tw.shape)],
        out_specs=pl.BlockSpec((1, n1, rows, 2 * DFT_LANE),
                               lambda i: (i // tiles_per_order, 0, i % tiles_per_order, 0)),
        out_shape=jax.ShapeDtypeStruct((n_rows // HY_WIDTH, n1, HY_WIDTH, 2 * DFT_LANE), F32),
        scratch_shapes=[pltpu.VMEM((rows, 2 * L), F32), pltpu.VMEM((n1, rows, 2 * DFT_LANE), F32)],
        compiler_params=_cparams(("parallel",)),
        name="fspec",
    )(h2t, t2, w3f, w3b, dl, wf, tw)


def _hyena_kernel(hy_ref, wsh_ref, bsh_ref, ksp_ref, bias_ref, wf_ref, wi_ref, tw_ref, out_ref,
                  u_scr, x1_scr, x2_scr, g_scr, *, L, n1, ct, group):
    lane = lax.broadcasted_iota(jnp.int32, (ct, LANES), 1)

    def short_conv(g, dst):
        for bi in range(2):
            p = hy_ref[bi, g].astype(F32)
            left = pltpu.roll(p, 1, 1)
            left = jnp.concatenate([jnp.where(lane == 0, 0.0, left[:, :LANES]), left[:, LANES:]], axis=1)
            right = pltpu.roll(p, L - 1, 1)
            right = jnp.concatenate([right[:, :L - LANES], jnp.where(lane == LANES - 1, 0.0, right[:, L - LANES:])],
                                    axis=1)
            conv = wsh_ref[g, 0] * left + wsh_ref[g, 1] * p + wsh_ref[g, 2] * right + bsh_ref[g]
            if g == 2:
                z = hy_ref[bi, 3].astype(F32)
                conv = conv * (z * jax.nn.sigmoid(z))
            dst[bi] = conv

    short_conv(0, u_scr)
    gate_scr = (x1_scr, x2_scr)

    half_blocks = n1 // 2
    wf = wf_ref[...]
    wi = wi_ref[...]
    re, im = slice(0, DFT_LANE), slice(DFT_LANE, 2 * DFT_LANE)

    def mxu_dft(x, w):
        res = jnp.dot(x.reshape(group * ct, 2 * DFT_LANE).astype(BF16), w, preferred_element_type=F32)
        res = res.reshape(group, ct, 2 * DFT_LANE)
        return res[..., re], res[..., im]

    for order in range(2):
        def load(i, rs, ls):
            if i >= half_blocks:
                return None
            cs = slice(i * DFT_LANE + ls.start, i * DFT_LANE + ls.stop)
            return (u_scr[0, rs, cs], u_scr[1, rs, cs])

        _stage_a_forward(load, g_scr, None, n1, ct)

        def twiddled(gs):
            g = g_scr[gs]
            gr, gi = g[..., re], g[..., im]
            twr = tw_ref[gs, 0, 0:1, :]
            twi = tw_ref[gs, 1, 0:1, :]
            return jnp.concatenate([gr * twr - gi * twi, gr * twi + gi * twr], axis=-1)

        def untwiddle(gs, q):
            qr, qi = q
            twr = tw_ref[gs, 0, 0:1, :]
            twi = tw_ref[gs, 1, 0:1, :]
            g_scr[gs, :, re] = qr * twr + qi * twi
            g_scr[gs, :, im] = qi * twr - qr * twi

        groups = [slice(g0, g0 + group) for g0 in range(0, n1, group)]
        spectra = [mxu_dft(twiddled(gs), wf) for gs in groups]
        short_conv(order + 1, gate_scr[order])
        prev = None
        for gs, (hr, hi) in zip(groups, spectra):
            kr, ki = ksp_ref[order, gs, :, re], ksp_ref[order, gs, :, im]
            q = mxu_dft(jnp.concatenate([hr * kr - hi * ki, hr * ki + hi * kr], axis=-1), wi)
            if prev is not None:
                untwiddle(*prev)
            prev = (gs, q)
        untwiddle(*prev)

        def body(r, carry, order=order):
            rs = pl.ds(pl.multiple_of(r * SUBLANES, SUBLANES), SUBLANES)
            bias = bias_ref[order, rs, :]
            for half in range(DFT_LANE // LANES):
                ls = slice(half * LANES, (half + 1) * LANES)
                li = slice(DFT_LANE + half * LANES, DFT_LANE + (half + 1) * LANES)
                q = [(g_scr[k1, rs, ls], g_scr[k1, rs, li]) for k1 in range(n1)]
                ev = _fft_dif(q[0::2], 1.0)
                od = _fft_dif(q[1::2], 1.0)
                for i in range(half_blocks):
                    y = _cadd(ev[i], _cmulc(od[i], np.exp(2j * np.pi * i / n1)))
                    cs = slice(i * DFT_LANE + ls.start, i * DFT_LANE + ls.stop)
                    for bi in range(2):
                        u = u_scr[bi, rs, cs]
                        conv = y[bi] + bias * u
                        if order == 0:
                            u_scr[bi, rs, cs] = x1_scr[bi, rs, cs] * conv
                        else:
                            out_ref[bi, rs, cs] = x2_scr[bi, rs, cs] * conv
            return carry

        lax.fori_loop(0, ct // SUBLANES, body, 0)


def _hyena(hyt4, wsh, bsh, kspec, bias, wf, wi, tw, n1, ct, group):
    B, _, C, L = hyt4.shape
    return pl.pallas_call(
        functools.partial(_hyena_kernel, L=L, n1=n1, ct=ct, group=group),
        grid=(C // ct, B // 2),
        in_specs=[
            pl.BlockSpec((2, 4, ct, L), lambda c, p: (p, 0, c, 0)),
            pl.BlockSpec((3, 3, ct, 1), lambda c, p: (0, 0, c, 0)),
            pl.BlockSpec((3, ct, 1), lambda c, p: (0, c, 0)),
            pl.BlockSpec((2, n1, ct, 2 * DFT_LANE), lambda c, p: (0, 0, c, 0)),
            pl.BlockSpec((2, ct, 1), lambda c, p: (0, c, 0)),
            _resident(wf.shape), _resident(wi.shape), _resident(tw.shape),
        ],
        out_specs=pl.BlockSpec((2, ct, L), lambda c, p: (p, c, 0)),
        out_shape=jax.ShapeDtypeStruct((B, C, L), F32),
        scratch_shapes=[pltpu.VMEM((2, ct, L), F32), pltpu.VMEM((2, ct, L), F32), pltpu.VMEM((2, ct, L), F32),
                        pltpu.VMEM((n1, ct, 2 * DFT_LANE), F32)],
        compiler_params=_cparams(("parallel", "parallel")),
        name="hyena",
    )(hyt4, wsh, bsh, kspec, bias, wf, wi, tw)


def _attn_kernel(qt_ref, k_ref, vt_ref, o_ref, *, L, tq):
    units = [(j, t0) for t0 in range(0, tq, Q_TILE) for j in range(2)]
    starts = list(range(0, L, KV_CHUNK))

    def scores(u, c0):
        j, t0 = u
        return jnp.dot(k_ref[0, j, c0:c0 + KV_CHUNK, :], qt_ref[0, j, :, t0:t0 + Q_TILE],
                       preferred_element_type=F32)

    m = {u: None for u in units}
    acc = {u: None for u in units}
    cur = {u: scores(u, starts[0]) for u in units}
    for i, c0 in enumerate(starts):
        nxt = {}
        for u in units:
            if i + 1 < len(starts):
                nxt[u] = scores(u, starts[i + 1])
            s = cur[u]
            mc = jnp.max(s, axis=0, keepdims=True)
            m_new = mc if m[u] is None else jnp.maximum(m[u], mc)
            p = jnp.exp2(s - m_new).astype(BF16)
            pv = jnp.dot(vt_ref[0, u[0], :, c0:c0 + KV_CHUNK], p, preferred_element_type=F32)
            acc[u] = pv if m[u] is None else jnp.exp2(m[u] - m_new) * acc[u] + pv
            m[u] = m_new
        cur = nxt
    for j, t0 in units:
        a = acc[(j, t0)]
        o_ref[0, j * V_DIM:(j + 1) * V_DIM, t0:t0 + Q_TILE] = (a[:V_DIM] / a[V_DIM:V_DIM + 1]).astype(BF16)


def _attn(qt, k, vt, tq):
    B, _, L, _ = k.shape
    return pl.pallas_call(
        functools.partial(_attn_kernel, L=L, tq=tq),
        grid=(B, HEAD_PAIRS, L // tq),
        in_specs=[pl.BlockSpec((1, 2, QK_DIM, tq), lambda b, p, i: (b, p, 0, i)),
                  pl.BlockSpec((1, 2, L, QK_DIM), lambda b, p, i: (b, p, 0, 0)),
                  pl.BlockSpec((1, 2, V_ROWS, L), lambda b, p, i: (b, p, 0, 0))],
        out_specs=pl.BlockSpec((1, 2 * V_DIM, tq), lambda b, p, i: (b, p, i)),
        out_shape=jax.ShapeDtypeStruct((B, ATTN_WIDTH, L), BF16),
        compiler_params=_cparams(("parallel", "parallel", "parallel")),
        name="attn",
    )(qt, k, vt)


def _final_kernel(x_ref, yt_ref, at_ref, zat_ref, gn_ref, wg_ref, bg_ref, why_ref, wat_ref, wout_ref, o_ref,
                  *, tl, sub):
    for r0 in range(0, tl, sub):
        rows = slice(r0, r0 + sub)
        x = x_ref[0, rows, :]
        h = (x * lax.rsqrt(jnp.mean(x * x, axis=-1, keepdims=True) + EPS) * gn_ref[...]).astype(BF16)
        gates = jax.nn.sigmoid(jnp.dot(h, wg_ref[...], preferred_element_type=F32) + bg_ref[...])
        y = yt_ref[0, :, rows].T.astype(BF16)
        u_hy = jnp.dot(y, why_ref[...], preferred_element_type=F32)
        z = zat_ref[0, rows, :].astype(F32)
        a = (at_ref[0, :, rows].astype(F32).T * (z * jax.nn.sigmoid(z))).astype(BF16)
        u_at = jnp.dot(a, wat_ref[...], preferred_element_type=F32)
        merged = gates[:, :D_MODEL] * u_hy + gates[:, D_MODEL:] * u_at
        o_ref[0, rows, :] = x + jnp.dot(merged.astype(BF16), wout_ref[...], preferred_element_type=F32)


def _final(x, yt, at, zat, gn, wg, bg, why, wat, wout, tl):
    B, L, _ = x.shape
    tok = lambda w: pl.BlockSpec((1, tl, w), lambda b, i: (b, i, 0))
    consts = (gn, wg, bg, why, wat, wout)
    return pl.pallas_call(
        functools.partial(_final_kernel, tl=tl, sub=min(512, tl)),
        grid=(B, L // tl),
        in_specs=[tok(D_MODEL), pl.BlockSpec((1, HY_WIDTH, tl), lambda b, i: (b, 0, i)),
                  pl.BlockSpec((1, ATTN_WIDTH, tl), lambda b, i: (b, 0, i)), tok(ATTN_WIDTH)]
        + [_resident(a.shape) for a in consts],
        out_specs=tok(D_MODEL),
        out_shape=jax.ShapeDtypeStruct((B, L, D_MODEL), F32),
        compiler_params=_cparams(("parallel", "parallel")),
        name="final",
    )(x, yt, at, zat, *consts)


def _layer(x, g_norm, w_in, b_gate, w_short, b_short, w_f1, b_f1, freq_1, w_f2, b_f2, freq_2, w_f3,
           hy_bias, w_hy_out, g_cq, w_uq, g_ckv, w_ukv, g_qn, g_kn, w_attn_out, w_out):
    B, L, _ = x.shape
    n = 2 * L
    n1 = n // DFT_LANE
    assert B % 2 == 0 and n1 * DFT_LANE == n and n1 >= 2 and (n1 & (n1 - 1)) == 0
    tl = min(512, L)
    ct = 64
    group = min(8, n1)

    o_z, o_cq, o_ckv, o_kr, o_za = 3 * HY_WIDTH, 4 * HY_WIDTH, 4 * HY_WIDTH + Q_LORA, 4 * HY_WIDTH + Q_LORA + KV_LORA, \
        4 * HY_WIDTH + Q_LORA + KV_LORA + QK_ROPE
    del o_z
    w_in = w_in.astype(BF16)
    wt_hy = w_in[:, :4 * HY_WIDTH].T
    half = QK_ROPE // 2
    w_cq = jnp.concatenate([w_in[:, o_cq:o_ckv], jnp.zeros((D_MODEL, QK_NOPE), BF16), w_in[:, o_kr:o_za],
                            jnp.zeros((D_MODEL, LANES - QK_DIM), BF16)], axis=1)
    w_ckv = w_in[:, o_ckv:o_kr]
    w_za = w_in[:, o_za:o_za + ATTN_WIDTH]
    w_gate = w_in[:, o_za + ATTN_WIDTH:]

    perm = np.concatenate([np.arange(QK_NOPE), QK_NOPE + half + np.arange(half), QK_NOPE + np.arange(half)])
    wqt = w_uq.T.astype(BF16)
    wkv = w_ukv.reshape(KV_LORA, MLA_HEADS, QK_NOPE + V_DIM)
    wk = jnp.pad(wkv[:, :, :QK_NOPE], ((0, 0), (0, 0), (0, LANES - QK_NOPE)))
    wk = wk.reshape(KV_LORA, MLA_HEADS * LANES).astype(BF16)
    wvt = jnp.pad(wkv[:, :, QK_NOPE:], ((0, 0), (0, 0), (0, V_ROWS - V_DIM)))
    wvt = wvt.reshape(KV_LORA, MLA_HEADS * V_ROWS).T.astype(BF16)
    pad_lanes = lambda g: jnp.pad(g, ((0, 0), (0, LANES - QK_DIM)))
    gqc = g_qn.reshape(QK_DIM, 1)
    gqsc = gqc[perm]
    gk = g_kn.reshape(1, QK_DIM)
    gks = pad_lanes(gk[:, perm])
    gk = pad_lanes(gk)

    ctab, stab = _rope_tables(L)
    pad_tab = lambda tab: np.pad(tab, ((0, 0), (0, LANES - QK_DIM)))
    hyt, zat, qt, k, vt = _in_proj(
        x, g_norm.reshape(1, D_MODEL), wt_hy, w_cq, w_ckv, w_za, g_cq.reshape(1, Q_LORA),
        g_ckv.reshape(1, KV_LORA), jnp.asarray(pad_tab(ctab)), jnp.asarray(pad_tab(stab)),
        jnp.asarray(np.ascontiguousarray(ctab.T)), jnp.asarray(np.ascontiguousarray(stab.T)),
        wqt, wk, wvt, gqc, gqsc, gk, gks, tl)

    zfeat, t = _filter_features(L)
    w1 = jnp.concatenate([w_f1, jnp.zeros((LANES - FILTER_EMB, FILTER_HIDDEN), F32)], axis=0)
    w3 = w_f3.reshape(FILTER_HIDDEN, 2, 2, HY_WIDTH)
    w3f = w3[:, :, 0].reshape(FILTER_HIDDEN, 2 * HY_WIDTH).T
    w3b = w3[:, :, 1].reshape(FILTER_HIDDEN, 2 * HY_WIDTH).T
    deltas = np.abs(np.linspace(MIN_DECAY, MAX_DECAY, HY_WIDTH))
    dl = np.tile(deltas, 2).reshape(2 * HY_WIDTH, 1).astype(np.float32)
    col = lambda v: v.reshape(FILTER_HIDDEN, 1)
    zt = np.ascontiguousarray(np.stack([zfeat.T, zfeat[::-1].T]))
    t2 = np.ascontiguousarray(np.stack([t, t[::-1]]).reshape(2, 1, L))
    h2t = _fmlp(jnp.asarray(zt), w1.T, col(b_f1), col(freq_1), w_f2.T, col(b_f2), col(freq_2))
    wf, wi = (jnp.asarray(w).astype(BF16) for w in _dft_mats())
    tw = jnp.asarray(_twiddle(n1))
    kspec = _fspec(h2t, jnp.asarray(t2), w3f, w3b, jnp.asarray(dl), wf, tw, L, n1, ct, group)

    wsh = w_short.reshape(3, 3, HY_WIDTH).transpose(1, 0, 2)[..., None]
    bsh = b_short.reshape(3, HY_WIDTH, 1)
    yt = _hyena(hyt.reshape(B, 4, HY_WIDTH, L), wsh, bsh, kspec, hy_bias.reshape(2, HY_WIDTH, 1),
                wf, wi, tw, n1, ct, group)

    at = _attn(qt, k, vt, min(2048, L))

    return _final(x, yt, at, zat, g_norm.reshape(1, D_MODEL), w_gate, b_gate.reshape(1, 2 * D_MODEL),
                  w_hy_out.astype(BF16), w_attn_out.astype(BF16), w_out.astype(BF16), min(1024, L))


def kernel(x, g_norm, w_in, b_gate, w_short, b_short, w_f1, b_f1, freq_1, w_f2, b_f2, freq_2, w_f3, hy_bias, w_hy_out, g_cq, w_uq, g_ckv, w_ukv, g_qn, g_kn, w_attn_out, w_out):
    depth = g_norm.shape[0]
    for l in range(depth):
        x = _layer(x, g_norm[l], w_in[l], b_gate[l], w_short[l], b_short[l], w_f1[l], b_f1[l], freq_1[l], w_f2[l],
                   b_f2[l], freq_2[l], w_f3[l], hy_bias[l], w_hy_out[l], g_cq[l], w_uq[l], g_ckv[l], w_ukv[l],
                   g_qn[l], g_kn[l], w_attn_out[l], w_out[l])
    return x
```

```python
import functools
import math

import numpy as np
import jax
import jax.numpy as jnp
from jax import lax
from jax.experimental import pallas as pl
from jax.experimental.pallas import tpu as pltpu

D_MODEL = 1024
HY_WIDTH = 512
FILTER_EMB = 33
FILTER_HIDDEN = 64
DECAY_TARGET = 1e-2
FAST_DECAY = 0.3
SLOW_DECAY = 1.5
MIN_DECAY = math.log(DECAY_TARGET) / SLOW_DECAY
MAX_DECAY = math.log(DECAY_TARGET) / FAST_DECAY
MLA_HEADS = 8
HEAD_PAIRS = MLA_HEADS // 2
QK_NOPE = 64
QK_ROPE = 32
QK_DIM = QK_NOPE + QK_ROPE
V_DIM = 64
Q_LORA = 384
KV_LORA = 256
ATTN_WIDTH = MLA_HEADS * V_DIM
ROPE_THETA = 10000.0
EPS = 1e-6

LANES = 128
SUBLANES = 8
DFT_LANE = 256
V_ROWS = 80
KV_CHUNK = 256
Q_TILE = 256
VMEM_LIMIT = 56 * 1024 * 1024

F32 = jnp.float32
BF16 = jnp.bfloat16
_NT = (((1,), (1,)), ((), ()))
_NN = (((1,), (0,)), ((), ()))


def _cparams(sem):
    return pltpu.CompilerParams(dimension_semantics=sem, vmem_limit_bytes=VMEM_LIMIT)


def _resident(shape):
    nd = len(shape)
    return pl.BlockSpec(shape, lambda *_: (0,) * nd, pipeline_mode=pl.Buffered(1))


def _add(a, b):
    if a is None:
        return b
    if b is None:
        return a
    return a + b


def _sub(a, b):
    if b is None:
        return a
    if a is None:
        return -b
    return a - b


def _cadd(x, y):
    return (_add(x[0], y[0]), _add(x[1], y[1]))


def _csub(x, y):
    return (_sub(x[0], y[0]), _sub(x[1], y[1]))


def _scale(a, s):
    if a is None or s == 0.0:
        return None
    if s == 1.0:
        return a
    if s == -1.0:
        return -a
    return a * s


def _signed_sum(terms):
    pos = [a for sg, a in terms if a is not None and sg > 0]
    neg = [a for sg, a in terms if a is not None and sg < 0]
    if pos:
        r = pos[0]
        for a in pos[1:]:
            r = r + a
        for a in neg:
            r = r - a
        return r, 1.0
    if neg:
        r = neg[0]
        for a in neg[1:]:
            r = r + a
        return r, -1.0
    return None, 1.0


def _cmulc(x, w):
    wr, wi = float(np.real(w)), float(np.imag(w))
    if abs(wr) < 1e-15:
        wr = 0.0
    if abs(wi) < 1e-15:
        wi = 0.0
    xr, xi = x
    if wr != 0.0 and abs(abs(wr) - abs(wi)) < 1e-12:
        sr, si, c = np.sign(wr), np.sign(wi), abs(wr)
        re, sre = _signed_sum([(sr, xr), (-si, xi)])
        im, sim = _signed_sum([(si, xr), (sr, xi)])
        return (_scale(re, sre * c), _scale(im, sim * c))
    re = _sub(_scale(xr, wr), _scale(xi, wi))
    im = _add(_scale(xr, wi), _scale(xi, wr))
    return (re, im)


def _cdiff_mulc(x, y, w):
    wr, wi = float(np.real(w)), float(np.imag(w))
    (xr, xi), (yr, yi) = x, y
    unit = {(1, 0): ((1, xr, yr), (1, xi, yi)), (-1, 0): ((-1, xr, yr), (-1, xi, yi)),
            (0, 1): ((-1, xi, yi), (1, xr, yr)), (0, -1): ((1, xi, yi), (-1, xr, yr))}
    key = (int(round(wr)), int(round(wi)))
    if abs(wr - key[0]) < 1e-12 and abs(wi - key[1]) < 1e-12 and key in unit:
        parts = []
        for sg, a, b in unit[key]:
            v, sv = _signed_sum([(sg, a), (-sg, b)])
            parts.append(_scale(v, sv))
        return tuple(parts)
    return _cmulc(_csub(x, y), w)


def _cmul(x, y):
    xr, xi = x
    yr, yi = y
    re = _sub(None if xr is None else xr * yr, None if xi is None else xi * yi)
    im = _add(None if xr is None else xr * yi, None if xi is None else xi * yr)
    return (re, im)


def _fft_dif(xs, sign):
    n = len(xs)
    if n == 1:
        return xs
    half = n // 2
    a = [_cadd(xs[i], xs[i + half]) for i in range(half)]
    b = [_cdiff_mulc(xs[i], xs[i + half], np.exp(sign * 2j * np.pi * i / n)) for i in range(half)]
    ev = _fft_dif(a, sign)
    od = _fft_dif(b, sign)
    out = [None] * n
    out[0::2] = ev
    out[1::2] = od
    return out


def _dft_mats():
    n = np.arange(DFT_LANE)
    ang = 2.0 * np.pi * np.outer(n, n) / DFT_LANE
    fr, fi = np.cos(ang), -np.sin(ang)
    fwd = np.block([[fr, fi], [-fi, fr]]).astype(np.float32)
    inv = np.block([[fr, -fi], [fi, fr]]).astype(np.float32)
    return fwd, inv


def _twiddle(n1):
    n = n1 * DFT_LANE
    ang = 2.0 * np.pi * np.outer(np.arange(n1), np.arange(DFT_LANE)) / n
    tw = np.stack([np.cos(ang), -np.sin(ang)], axis=1)
    return np.ascontiguousarray(np.broadcast_to(tw[:, :, None, :], (n1, 2, SUBLANES, DFT_LANE))).astype(np.float32)


def _filter_features(L):
    t = np.linspace(0.0, 1.0, L)[:, None]
    bands = (FILTER_EMB - 1) // 2
    f = np.linspace(1e-4, bands - 1, bands)
    ang = (2.0 * np.pi / L) * np.arange(L)[:, None] * f[None, :]
    z = np.concatenate([t, np.cos(ang), -np.sin(ang)], axis=-1)
    zp = np.zeros((L, LANES), np.float32)
    zp[:, :FILTER_EMB] = z
    return zp, t[:, 0].astype(np.float32)


def _rope_tables(L):
    pos = np.arange(L, dtype=np.float64)
    inv_freq = ROPE_THETA ** (-np.arange(0, QK_ROPE, 2, dtype=np.float64) / QK_ROPE)
    ang = pos[:, None] * inv_freq[None, :]
    c, s = np.cos(ang), np.sin(ang)
    ctab = np.concatenate([np.ones((L, QK_NOPE)), c, c], axis=1)
    stab = np.concatenate([np.zeros((L, QK_NOPE)), -s, s], axis=1)
    return ctab.astype(np.float32), stab.astype(np.float32)


def _in_proj_kernel(x_ref, gn_ref, wt_hy_ref, w_cq_ref, w_ckv_ref, w_za_ref, gcq_ref, gckv_ref,
                    ct_ref, st_ref, ctt_ref, stt_ref, wqt_ref, wk_ref, wvt_ref,
                    gqc_ref, gqsc_ref, gk_ref, gks_ref, ones_ref,
                    hyt_ref, zat_ref, qt_ref, k_ref, vt_ref, *, tl, sub):
    half = QK_ROPE // 2
    scale = math.log2(math.e) / math.sqrt(QK_DIM)

    def latent(c, g_ref):
        r = lax.rsqrt(jnp.mean(c * c, axis=-1, keepdims=True) + EPS)
        return (c * r * g_ref[...]).astype(BF16)

    for r0 in range(0, tl, sub):
        rows = slice(r0, r0 + sub)
        x = x_ref[0, rows, :]
        ms = jnp.mean(x * x, axis=-1, keepdims=True)
        h = (x * lax.rsqrt(ms + EPS) * gn_ref[...]).astype(BF16)

        cqr = jnp.dot(h, w_cq_ref[...], preferred_element_type=F32)
        cq = latent(cqr[:, :Q_LORA], gcq_ref)
        ckv = latent(jnp.dot(h, w_ckv_ref[...], preferred_element_type=F32), gckv_ref)
        kr_a = cqr[:, Q_LORA:]
        qt_all = lax.dot_general(wqt_ref[...], cq, _NT, preferred_element_type=F32)
        ka_all = jnp.dot(ckv, wk_ref[...], preferred_element_type=F32)
        vt_all = lax.dot_general(wvt_ref[...], ckv, _NT, preferred_element_type=F32) + ones_ref[...]

        hyt_ref[0, :, rows] = lax.dot_general(wt_hy_ref[...], h, _NT, preferred_element_type=F32).astype(BF16)
        zat_ref[0, rows, :] = jnp.dot(h, w_za_ref[...], preferred_element_type=F32).astype(BF16)

        qc, qs = gqc_ref[...] * ctt_ref[:, rows], gqsc_ref[...] * stt_ref[:, rows]
        kc, ks = gk_ref[...] * ct_ref[rows, :], gks_ref[...] * st_ref[rows, :]
        lane = lax.broadcasted_iota(jnp.int32, kr_a.shape, 1)
        kr_w = jnp.where(lane < QK_NOPE + half, pltpu.roll(kr_a, LANES - half, 1), pltpu.roll(kr_a, half, 1))
        for hd in range(MLA_HEADS):
            qa = qt_all[hd * QK_DIM:(hd + 1) * QK_DIM]
            qw = jnp.concatenate([qa[:QK_NOPE], qa[QK_NOPE + half:], qa[QK_NOPE:QK_NOPE + half]], axis=0)
            r = lax.rsqrt(jnp.mean(qa * qa, axis=0, keepdims=True) + EPS) * scale
            qt_ref[0, hd, :, rows] = ((qa * qc + qw * qs) * r).astype(BF16)
            ka = ka_all[:, hd * LANES:(hd + 1) * LANES] + kr_a
            rk = lax.rsqrt(jnp.sum(ka * ka, axis=-1, keepdims=True) * (1.0 / QK_DIM) + EPS)
            k_ref[0, hd, rows, :] = ((ka * kc + kr_w * ks) * rk)[:, :QK_DIM].astype(BF16)
            vt_ref[0, hd, :, rows] = vt_all[hd * V_ROWS:(hd + 1) * V_ROWS].astype(BF16)


def _in_proj(x, gn, wt_hy, w_cq, w_ckv, w_za, gcq, gckv, ctab, stab, ctab_t, stab_t,
             wqt, wk, wvt, gqc, gqsc, gk, gks, tl):
    ones = np.zeros((MLA_HEADS, V_ROWS, 1), np.float32)
    ones[:, V_DIM, 0] = 1.0
    ones = jnp.asarray(ones.reshape(MLA_HEADS * V_ROWS, 1))
    B, L, _ = x.shape
    tab = pl.BlockSpec((tl, LANES), lambda b, i: (i, 0))
    tab_t = pl.BlockSpec((QK_DIM, tl), lambda b, i: (0, i))
    consts = (gn, wt_hy, w_cq, w_ckv, w_za, gcq, gckv)
    qkv_consts = (wqt, wk, wvt, gqc, gqsc, gk, gks, ones)
    return pl.pallas_call(
        functools.partial(_in_proj_kernel, tl=tl, sub=min(512, tl)),
        grid=(B, L // tl),
        in_specs=[pl.BlockSpec((1, tl, D_MODEL), lambda b, i: (b, i, 0))]
        + [_resident(a.shape) for a in consts] + [tab, tab, tab_t, tab_t] + [_resident(a.shape) for a in qkv_consts],
        out_specs=[
            pl.BlockSpec((1, 4 * HY_WIDTH, tl), lambda b, i: (b, 0, i)),
            pl.BlockSpec((1, tl, ATTN_WIDTH), lambda b, i: (b, i, 0)),
            pl.BlockSpec((1, MLA_HEADS, QK_DIM, tl), lambda b, i: (b, 0, 0, i)),
            pl.BlockSpec((1, MLA_HEADS, tl, QK_DIM), lambda b, i: (b, 0, i, 0)),
            pl.BlockSpec((1, MLA_HEADS, V_ROWS, tl), lambda b, i: (b, 0, 0, i)),
        ],
        out_shape=[
            jax.ShapeDtypeStruct((B, 4 * HY_WIDTH, L), BF16),
            jax.ShapeDtypeStruct((B, L, ATTN_WIDTH), BF16),
            jax.ShapeDtypeStruct((B, MLA_HEADS, QK_DIM, L), BF16),
            jax.ShapeDtypeStruct((B, MLA_HEADS, L, QK_DIM), BF16),
            jax.ShapeDtypeStruct((B, MLA_HEADS, V_ROWS, L), BF16),
        ],
        compiler_params=_cparams(("parallel", "parallel")),
        name="in_proj",
    )(x, *consts, ctab, stab, ctab_t, stab_t, *qkv_consts)


def _fmlp_kernel(zt_ref, w1t_ref, b1_ref, f1_ref, w2t_ref, b2_ref, f2_ref, out_ref):
    hp = lax.Precision.HIGHEST
    h = jnp.sin(f1_ref[...] * (jnp.dot(w1t_ref[...], zt_ref[0], precision=hp, preferred_element_type=F32) + b1_ref[...]))
    out_ref[0] = jnp.sin(f2_ref[...] * (jnp.dot(w2t_ref[...], h, precision=hp, preferred_element_type=F32) + b2_ref[...]))


def _fmlp(zt, w1t, b1, f1, w2t, b2, f2):
    _, _, L = zt.shape
    return pl.pallas_call(
        _fmlp_kernel,
        grid=(2,),
        in_specs=[pl.BlockSpec((1, LANES, L), lambda d: (d, 0, 0)), _resident(w1t.shape), _resident(b1.shape),
                  _resident(f1.shape), _resident(w2t.shape), _resident(b2.shape), _resident(f2.shape)],
        out_specs=pl.BlockSpec((1, FILTER_HIDDEN, L), lambda d: (d, 0, 0)),
        out_shape=jax.ShapeDtypeStruct((2, FILTER_HIDDEN, L), F32),
        compiler_params=_cparams(("parallel",)),
        name="fmlp",
    )(zt, w1t, b1, f1, w2t, b2, f2)


def _stage_a_forward(load, g_scr, tw_ref, n1, rows_total):
    def body(r, carry):
        rs = pl.ds(pl.multiple_of(r * SUBLANES, SUBLANES), SUBLANES)
        for half in range(DFT_LANE // LANES):
            ls = slice(half * LANES, (half + 1) * LANES)
            li = slice(DFT_LANE + half * LANES, DFT_LANE + (half + 1) * LANES)
            xs = [load(i, rs, ls) for i in range(n1)]
            xs = [(None, None) if x is None else x for x in xs]
            X = _fft_dif(xs, -1.0)
            for k1 in range(n1):
                if tw_ref is None:
                    yr, yi = X[k1]
                else:
                    yr, yi = _cmul(X[k1], (tw_ref[k1, 0, :, ls], tw_ref[k1, 1, :, ls]))
                g_scr[k1, rs, ls] = yr
                g_scr[k1, rs, li] = yi
        return carry

    lax.fori_loop(0, rows_total // SUBLANES, body, 0)


def _fspec_kernel(h_ref, t_ref, w3f_ref, w3b_ref, dl_ref, wf_ref, tw_ref, out_ref, k_scr, g_scr,
                  *, L, n1, rows, group):
    dl = dl_ref[...]
    for d, w_ref in enumerate((w3f_ref, w3b_ref)):
        k = jnp.dot(w_ref[...].astype(BF16), h_ref[d].astype(BF16), preferred_element_type=F32)
        k_scr[:, d * L:(d + 1) * L] = k * jnp.exp(-(dl * t_ref[d]))

    def load(i, rs, ls):
        return (k_scr[rs, i * DFT_LANE + ls.start:i * DFT_LANE + ls.stop], None)

    _stage_a_forward(load, g_scr, tw_ref, n1, rows)
    inv_n = 1.0 / (n1 * DFT_LANE)
    w = wf_ref[...]
    for g0 in range(0, n1, group):
        lhs = g_scr[g0:g0 + group].reshape(group * rows, 2 * DFT_LANE).astype(BF16)
        res = jnp.dot(lhs, w, preferred_element_type=F32)
        out_ref[0, g0:g0 + group] = res.reshape(group, rows, 2 * DFT_LANE) * inv_n


def _fspec(h2t, t2, w3f, w3b, dl, wf, tw, L, n1, rows, group):
    n_rows = w3f.shape[0]
    tiles_per_order = HY_WIDTH // rows
    row_spec = lambda shape: pl.BlockSpec(shape, lambda i: (i, 0))
    return pl.pallas_call(
        functools.partial(_fspec_kernel, L=L, n1=n1, rows=rows, group=group),
        grid=(n_rows // rows,),
        in_specs=[_resident(h2t.shape), _resident(t2.shape),
                  row_spec((rows, FILTER_HIDDEN)), row_spec((rows, FILTER_HIDDEN)), row_spec((rows, 1)),
                  _resident(wf.shape), _resident(tw.shape)],
        out_specs=pl.BlockSpec((1, n1, rows, 2 * DFT_LANE),
                               lambda i: (i // tiles_per_order, 0, i % tiles_per_order, 0)),
        out_shape=jax.ShapeDtypeStruct((n_rows // HY_WIDTH, n1, HY_WIDTH, 2 * DFT_LANE), F32),
        scratch_shapes=[pltpu.VMEM((rows, 2 * L), F32), pltpu.VMEM((n1, rows, 2 * DFT_LANE), F32)],
        compiler_params=_cparams(("parallel",)),
        name="fspec",
    )(h2t, t2, w3f, w3b, dl, wf, tw)


def _hyena_kernel(hy_ref, wsh_ref, bsh_ref, ksp_ref, bias_ref, wf_ref, wi_ref, tw_ref, out_ref,
                  u_scr, x1_scr, x2_scr, g_scr, *, L, n1, ct, group):
    lane = lax.broadcasted_iota(jnp.int32, (ct, LANES), 1)

    def short_conv(g, dst):
        for bi in range(2):
            p = hy_ref[bi, g].astype(F32)
            left = pltpu.roll(p, 1, 1)
            left = jnp.concatenate([jnp.where(lane == 0, 0.0, left[:, :LANES]), left[:, LANES:]], axis=1)
            right = pltpu.roll(p, L - 1, 1)
            right = jnp.concatenate([right[:, :L - LANES], jnp.where(lane == LANES - 1, 0.0, right[:, L - LANES:])],
                                    axis=1)
            conv = wsh_ref[g, 0] * left + wsh_ref[g, 1] * p + wsh_ref[g, 2] * right + bsh_ref[g]
            if g == 2:
                z = hy_ref[bi, 3].astype(F32)
                conv = conv * (z * jax.nn.sigmoid(z))
            dst[bi] = conv

    short_conv(0, u_scr)
    gate_scr = (x1_scr, x2_scr)

    half_blocks = n1 // 2
    wf = wf_ref[...]
    wi = wi_ref[...]
    re, im = slice(0, DFT_LANE), slice(DFT_LANE, 2 * DFT_LANE)

    def mxu_dft(x, w):
        res = jnp.dot(x.reshape(group * ct, 2 * DFT_LANE).astype(BF16), w, preferred_element_type=F32)
        res = res.reshape(group, ct, 2 * DFT_LANE)
        return res[..., re], res[..., im]

    for order in range(2):
        def load(i, rs, ls):
            if i >= half_blocks:
                return None
            cs = slice(i * DFT_LANE + ls.start, i * DFT_LANE + ls.stop)
            return (u_scr[0, rs, cs], u_scr[1, rs, cs])

        _stage_a_forward(load, g_scr, None, n1, ct)

        def twiddled(gs):
            g = g_scr[gs]
            gr, gi = g[..., re], g[..., im]
            twr = tw_ref[gs, 0, 0:1, :]
            twi = tw_ref[gs, 1, 0:1, :]
            return jnp.concatenate([gr * twr - gi * twi, gr * twi + gi * twr], axis=-1)

        def untwiddle(gs, q):
            qr, qi = q
            twr = tw_ref[gs, 0, 0:1, :]
            twi = tw_ref[gs, 1, 0:1, :]
            g_scr[gs, :, re] = qr * twr + qi * twi
            g_scr[gs, :, im] = qi * twr - qr * twi

        groups = [slice(g0, g0 + group) for g0 in range(0, n1, group)]
        spectra = [mxu_dft(twiddled(gs), wf) for gs in groups]
        short_conv(order + 1, gate_scr[order])
        prev = None
        for gs, (hr, hi) in zip(groups, spectra):
            kr, ki = ksp_ref[order, gs, :, re], ksp_ref[order, gs, :, im]
            q = mxu_dft(jnp.concatenate([hr * kr - hi * ki, hr * ki + hi * kr], axis=-1), wi)
            if prev is not None:
                untwiddle(*prev)
            prev = (gs, q)
        untwiddle(*prev)

        def body(r, carry, order=order):
            rs = pl.ds(pl.multiple_of(r * SUBLANES, SUBLANES), SUBLANES)
            bias = bias_ref[order, rs, :]
            for half in range(DFT_LANE // LANES):
                ls = slice(half * LANES, (half + 1) * LANES)
                li = slice(DFT_LANE + half * LANES, DFT_LANE + (half + 1) * LANES)
                q = [(g_scr[k1, rs, ls], g_scr[k1, rs, li]) for k1 in range(n1)]
                ev = _fft_dif(q[0::2], 1.0)
                od = _fft_dif(q[1::2], 1.0)
                for i in range(half_blocks):
                    y = _cadd(ev[i], _cmulc(od[i], np.exp(2j * np.pi * i / n1)))
                    cs = slice(i * DFT_LANE + ls.start, i * DFT_LANE + ls.stop)
                    for bi in range(2):
                        u = u_scr[bi, rs, cs]
                        conv = y[bi] + bias * u
                        if order == 0:
                            u_scr[bi, rs, cs] = x1_scr[bi, rs, cs] * conv
                        else:
                            out_ref[bi, rs, cs] = x2_scr[bi, rs, cs] * conv
            return carry

        lax.fori_loop(0, ct // SUBLANES, body, 0)


def _hyena(hyt4, wsh, bsh, kspec, bias, wf, wi, tw, n1, ct, group):
    B, _, C, L = hyt4.shape
    return pl.pallas_call(
        functools.partial(_hyena_kernel, L=L, n1=n1, ct=ct, group=group),
        grid=(C // ct, B // 2),
        in_specs=[
            pl.BlockSpec((2, 4, ct, L), lambda c, p: (p, 0, c, 0)),
            pl.BlockSpec((3, 3, ct, 1), lambda c, p: (0, 0, c, 0)),
            pl.BlockSpec((3, ct, 1), lambda c, p: (0, c, 0)),
            pl.BlockSpec((2, n1, ct, 2 * DFT_LANE), lambda c, p: (0, 0, c, 0)),
            pl.BlockSpec((2, ct, 1), lambda c, p: (0, c, 0)),
            _resident(wf.shape), _resident(wi.shape), _resident(tw.shape),
        ],
        out_specs=pl.BlockSpec((2, ct, L), lambda c, p: (p, c, 0)),
        out_shape=jax.ShapeDtypeStruct((B, C, L), F32),
        scratch_shapes=[pltpu.VMEM((2, ct, L), F32), pltpu.VMEM((2, ct, L), F32), pltpu.VMEM((2, ct, L), F32),
                        pltpu.VMEM((n1, ct, 2 * DFT_LANE), F32)],
        compiler_params=_cparams(("parallel", "parallel")),
        name="hyena",
    )(hyt4, wsh, bsh, kspec, bias, wf, wi, tw)


def _attn_kernel(qt_ref, k_ref, vt_ref, o_ref, *, L, tq):
    units = [(j, t0) for t0 in range(0, tq, Q_TILE) for j in range(2)]
    starts = list(range(0, L, KV_CHUNK))

    def scores(u, c0):
        j, t0 = u
        return jnp.dot(k_ref[0, j, c0:c0 + KV_CHUNK, :], qt_ref[0, j, :, t0:t0 + Q_TILE],
                       preferred_element_type=F32)

    m = {u: None for u in units}
    acc = {u: None for u in units}
    cur = {u: scores(u, starts[0]) for u in units}
    for i, c0 in enumerate(starts):
        nxt = {}
        for u in units:
            if i + 1 < len(starts):
                nxt[u] = scores(u, starts[i + 1])
            s = cur[u]
            mc = jnp.max(s, axis=0, keepdims=True)
            m_new = mc if m[u] is None else jnp.maximum(m[u], mc)
            p = jnp.exp2(s - m_new).astype(BF16)
            pv = jnp.dot(vt_ref[0, u[0], :, c0:c0 + KV_CHUNK], p, preferred_element_type=F32)
            acc[u] = pv if m[u] is None else jnp.exp2(m[u] - m_new) * acc[u] + pv
            m[u] = m_new
        cur = nxt
    for j, t0 in units:
        a = acc[(j, t0)]
        o_ref[0, j * V_DIM:(j + 1) * V_DIM, t0:t0 + Q_TILE] = (a[:V_DIM] / a[V_DIM:V_DIM + 1]).astype(BF16)


def _attn(qt, k, vt, tq):
    B, _, L, _ = k.shape
    return pl.pallas_call(
        functools.partial(_attn_kernel, L=L, tq=tq),
        grid=(B, HEAD_PAIRS, L // tq),
        in_specs=[pl.BlockSpec((1, 2, QK_DIM, tq), lambda b, p, i: (b, p, 0, i)),
                  pl.BlockSpec((1, 2, L, QK_DIM), lambda b, p, i: (b, p, 0, 0)),
                  pl.BlockSpec((1, 2, V_ROWS, L), lambda b, p, i: (b, p, 0, 0))],
        out_specs=pl.BlockSpec((1, 2 * V_DIM, tq), lambda b, p, i: (b, p, i)),
        out_shape=jax.ShapeDtypeStruct((B, ATTN_WIDTH, L), BF16),
        compiler_params=_cparams(("parallel", "parallel", "parallel")),
        name="attn",
    )(qt, k, vt)


def _final_kernel(x_ref, yt_ref, at_ref, zat_ref, gn_ref, wg_ref, bg_ref, why_ref, wat_ref, wout_ref, o_ref,
                  *, tl, sub):
    for r0 in range(0, tl, sub):
        rows = slice(r0, r0 + sub)
        x = x_ref[0, rows, :]
        h = (x * lax.rsqrt(jnp.mean(x * x, axis=-1, keepdims=True) + EPS) * gn_ref[...]).astype(BF16)
        gates = jax.nn.sigmoid(jnp.dot(h, wg_ref[...], preferred_element_type=F32) + bg_ref[...])
        y = yt_ref[0, :, rows].T.astype(BF16)
        u_hy = jnp.dot(y, why_ref[...], preferred_element_type=F32)
        z = zat_ref[0, rows, :].astype(F32)
        a = (at_ref[0, :, rows].astype(F32).T * (z * jax.nn.sigmoid(z))).astype(BF16)
        u_at = jnp.dot(a, wat_ref[...], preferred_element_type=F32)
        merged = gates[:, :D_MODEL] * u_hy + gates[:, D_MODEL:] * u_at
        o_ref[0, rows, :] = x + jnp.dot(merged.astype(BF16), wout_ref[...], preferred_element_type=F32)


def _final(x, yt, at, zat, gn, wg, bg, why, wat, wout, tl):
    B, L, _ = x.shape
    tok = lambda w: pl.BlockSpec((1, tl, w), lambda b, i: (b, i, 0))
    consts = (gn, wg, bg, why, wat, wout)
    return pl.pallas_call(
        functools.partial(_final_kernel, tl=tl, sub=min(512, tl)),
        grid=(B, L // tl),
        in_specs=[tok(D_MODEL), pl.BlockSpec((1, HY_WIDTH, tl), lambda b, i: (b, 0, i)),
                  pl.BlockSpec((1, ATTN_WIDTH, tl), lambda b, i: (b, 0, i)), tok(ATTN_WIDTH)]
        + [_resident(a.shape) for a in consts],
        out_specs=tok(D_MODEL),
        out_shape=jax.ShapeDtypeStruct((B, L, D_MODEL), F32),
        compiler_params=_cparams(("parallel", "parallel")),
        name="final",
    )(x, yt, at, zat, *consts)


def _layer(x, g_norm, w_in, b_gate, w_short, b_short, w_f1, b_f1, freq_1, w_f2, b_f2, freq_2, w_f3,
           hy_bias, w_hy_out, g_cq, w_uq, g_ckv, w_ukv, g_qn, g_kn, w_attn_out, w_out):
    B, L, _ = x.shape
    n = 2 * L
    n1 = n // DFT_LANE
    assert B % 2 == 0 and n1 * DFT_LANE == n and n1 >= 2 and (n1 & (n1 - 1)) == 0
    tl = min(1024, L)
    ct = 64
    group = min(8, n1)

    o_z, o_cq, o_ckv, o_kr, o_za = 3 * HY_WIDTH, 4 * HY_WIDTH, 4 * HY_WIDTH + Q_LORA, 4 * HY_WIDTH + Q_LORA + KV_LORA, \
        4 * HY_WIDTH + Q_LORA + KV_LORA + QK_ROPE
    del o_z
    w_in = w_in.astype(BF16)
    wt_hy = w_in[:, :4 * HY_WIDTH].T
    half = QK_ROPE // 2
    w_cq = jnp.concatenate([w_in[:, o_cq:o_ckv], jnp.zeros((D_MODEL, QK_NOPE), BF16), w_in[:, o_kr:o_za],
                            jnp.zeros((D_MODEL, LANES - QK_DIM), BF16)], axis=1)
    w_ckv = w_in[:, o_ckv:o_kr]
    w_za = w_in[:, o_za:o_za + ATTN_WIDTH]
    w_gate = w_in[:, o_za + ATTN_WIDTH:]

    perm = np.concatenate([np.arange(QK_NOPE), QK_NOPE + half + np.arange(half), QK_NOPE + np.arange(half)])
    wqt = w_uq.T.astype(BF16)
    wkv = w_ukv.reshape(KV_LORA, MLA_HEADS, QK_NOPE + V_DIM)
    wk = jnp.pad(wkv[:, :, :QK_NOPE], ((0, 0), (0, 0), (0, LANES - QK_NOPE)))
    wk = wk.reshape(KV_LORA, MLA_HEADS * LANES).astype(BF16)
    wvt = jnp.pad(wkv[:, :, QK_NOPE:], ((0, 0), (0, 0), (0, V_ROWS - V_DIM)))
    wvt = wvt.reshape(KV_LORA, MLA_HEADS * V_ROWS).T.astype(BF16)
    pad_lanes = lambda g: jnp.pad(g, ((0, 0), (0, LANES - QK_DIM)))
    gqc = g_qn.reshape(QK_DIM, 1)
    gqsc = gqc[perm]
    gk = g_kn.reshape(1, QK_DIM)
    gks = pad_lanes(gk[:, perm])
    gk = pad_lanes(gk)

    ctab, stab = _rope_tables(L)
    pad_tab = lambda tab: np.pad(tab, ((0, 0), (0, LANES - QK_DIM)))
    hyt, zat, qt, k, vt = _in_proj(
        x, g_norm.reshape(1, D_MODEL), wt_hy, w_cq, w_ckv, w_za, g_cq.reshape(1, Q_LORA),
        g_ckv.reshape(1, KV_LORA), jnp.asarray(pad_tab(ctab)), jnp.asarray(pad_tab(stab)),
        jnp.asarray(np.ascontiguousarray(ctab.T)), jnp.asarray(np.ascontiguousarray(stab.T)),
        wqt, wk, wvt, gqc, gqsc, gk, gks, tl)

    zfeat, t = _filter_features(L)
    w1 = jnp.concatenate([w_f1, jnp.zeros((LANES - FILTER_EMB, FILTER_HIDDEN), F32)], axis=0)
    w3 = w_f3.reshape(FILTER_HIDDEN, 2, 2, HY_WIDTH)
    w3f = w3[:, :, 0].reshape(FILTER_HIDDEN, 2 * HY_WIDTH).T
    w3b = w3[:, :, 1].reshape(FILTER_HIDDEN, 2 * HY_WIDTH).T
    deltas = np.abs(np.linspace(MIN_DECAY, MAX_DECAY, HY_WIDTH))
    dl = np.tile(deltas, 2).reshape(2 * HY_WIDTH, 1).astype(np.float32)
    col = lambda v: v.reshape(FILTER_HIDDEN, 1)
    zt = np.ascontiguousarray(np.stack([zfeat.T, zfeat[::-1].T]))
    t2 = np.ascontiguousarray(np.stack([t, t[::-1]]).reshape(2, 1, L))
    h2t = _fmlp(jnp.asarray(zt), w1.T, col(b_f1), col(freq_1), w_f2.T, col(b_f2), col(freq_2))
    wf, wi = (jnp.asarray(w).astype(BF16) for w in _dft_mats())
    tw = jnp.asarray(_twiddle(n1))
    kspec = _fspec(h2t, jnp.asarray(t2), w3f, w3b, jnp.asarray(dl), wf, tw, L, n1, ct, group)

    wsh = w_short.reshape(3, 3, HY_WIDTH).transpose(1, 0, 2)[..., None]
    bsh = b_short.reshape(3, HY_WIDTH, 1)
    yt = _hyena(hyt.reshape(B, 4, HY_WIDTH, L), wsh, bsh, kspec, hy_bias.reshape(2, HY_WIDTH, 1),
                wf, wi, tw, n1, ct, group)

    at = _attn(qt, k, vt, min(2048, L))

    return _final(x, yt, at, zat, g_norm.reshape(1, D_MODEL), w_gate, b_gate.reshape(1, 2 * D_MODEL),
                  w_hy_out.astype(BF16), w_attn_out.astype(BF16), w_out.astype(BF16), min(1024, L))


def kernel(x, g_norm, w_in, b_gate, w_short, b_short, w_f1, b_f1, freq_1, w_f2, b_f2, freq_2, w_f3, hy_bias, w_hy_out, g_cq, w_uq, g_ckv, w_ukv, g_qn, g_kn, w_attn_out, w_out):
    depth = g_norm.shape[0]
    for l in range(depth):
        x = _layer(x, g_norm[l], w_in[l], b_gate[l], w_short[l], b_short[l], w_f1[l], b_f1[l], freq_1[l], w_f2[l],
                   b_f2[l], freq_2[l], w_f3[l], hy_bias[l], w_hy_out[l], g_cq[l], w_uq[l], g_ckv[l], w_ukv[l],
                   g_qn[l], g_kn[l], w_attn_out[l], w_out[l])
    return x
```

```python
import functools
import math

import numpy as np
import jax
import jax.numpy as jnp
from jax import lax
from jax.experimental import pallas as pl
from jax.experimental.pallas import tpu as pltpu

D_MODEL = 1024
HY_WIDTH = 512
FILTER_EMB = 33
FILTER_HIDDEN = 64
DECAY_TARGET = 1e-2
FAST_DECAY = 0.3
SLOW_DECAY = 1.5
MIN_DECAY = math.log(DECAY_TARGET) / SLOW_DECAY
MAX_DECAY = math.log(DECAY_TARGET) / FAST_DECAY
MLA_HEADS = 8
HEAD_PAIRS = MLA_HEADS // 2
QK_NOPE = 64
QK_ROPE = 32
QK_DIM = QK_NOPE + QK_ROPE
V_DIM = 64
Q_LORA = 384
KV_LORA = 256
ATTN_WIDTH = MLA_HEADS * V_DIM
ROPE_THETA = 10000.0
EPS = 1e-6

LANES = 128
SUBLANES = 8
DFT_LANE = 256
V_ROWS = 80
KV_CHUNK = 256
Q_TILE = 256
VMEM_LIMIT = 56 * 1024 * 1024

F32 = jnp.float32
BF16 = jnp.bfloat16
_NT = (((1,), (1,)), ((), ()))


def _cparams(sem):
    return pltpu.CompilerParams(dimension_semantics=sem, vmem_limit_bytes=VMEM_LIMIT)


def _resident(shape):
    nd = len(shape)
    return pl.BlockSpec(shape, lambda *_: (0,) * nd, pipeline_mode=pl.Buffered(1))


def _add(a, b):
    if a is None:
        return b
    if b is None:
        return a
    return a + b


def _sub(a, b):
    if b is None:
        return a
    if a is None:
        return -b
    return a - b


def _cadd(x, y):
    return (_add(x[0], y[0]), _add(x[1], y[1]))


def _csub(x, y):
    return (_sub(x[0], y[0]), _sub(x[1], y[1]))


def _scale(a, s):
    if a is None or s == 0.0:
        return None
    if s == 1.0:
        return a
    if s == -1.0:
        return -a
    return a * s


def _signed_sum(terms):
    pos = [a for sg, a in terms if a is not None and sg > 0]
    neg = [a for sg, a in terms if a is not None and sg < 0]
    if pos:
        r = pos[0]
        for a in pos[1:]:
            r = r + a
        for a in neg:
            r = r - a
        return r, 1.0
    if neg:
        r = neg[0]
        for a in neg[1:]:
            r = r + a
        return r, -1.0
    return None, 1.0


def _cmulc(x, w):
    wr, wi = float(np.real(w)), float(np.imag(w))
    if abs(wr) < 1e-15:
        wr = 0.0
    if abs(wi) < 1e-15:
        wi = 0.0
    xr, xi = x
    if wr != 0.0 and abs(abs(wr) - abs(wi)) < 1e-12:
        sr, si, c = np.sign(wr), np.sign(wi), abs(wr)
        re, sre = _signed_sum([(sr, xr), (-si, xi)])
        im, sim = _signed_sum([(si, xr), (sr, xi)])
        return (_scale(re, sre * c), _scale(im, sim * c))
    re = _sub(_scale(xr, wr), _scale(xi, wi))
    im = _add(_scale(xr, wi), _scale(xi, wr))
    return (re, im)


def _cdiff_mulc(x, y, w):
    wr, wi = float(np.real(w)), float(np.imag(w))
    (xr, xi), (yr, yi) = x, y
    unit = {(1, 0): ((1, xr, yr), (1, xi, yi)), (-1, 0): ((-1, xr, yr), (-1, xi, yi)),
            (0, 1): ((-1, xi, yi), (1, xr, yr)), (0, -1): ((1, xi, yi), (-1, xr, yr))}
    key = (int(round(wr)), int(round(wi)))
    if abs(wr - key[0]) < 1e-12 and abs(wi - key[1]) < 1e-12 and key in unit:
        parts = []
        for sg, a, b in unit[key]:
            v, sv = _signed_sum([(sg, a), (-sg, b)])
            parts.append(_scale(v, sv))
        return tuple(parts)
    return _cmulc(_csub(x, y), w)


def _cmul(x, y):
    xr, xi = x
    yr, yi = y
    re = _sub(None if xr is None else xr * yr, None if xi is None else xi * yi)
    im = _add(None if xr is None else xr * yi, None if xi is None else xi * yr)
    return (re, im)


def _fft_dif(xs, sign):
    n = len(xs)
    if n == 1:
        return xs
    half = n // 2
    a = [_cadd(xs[i], xs[i + half]) for i in range(half)]
    b = [_cdiff_mulc(xs[i], xs[i + half], np.exp(sign * 2j * np.pi * i / n)) for i in range(half)]
    ev = _fft_dif(a, sign)
    od = _fft_dif(b, sign)
    out = [None] * n
    out[0::2] = ev
    out[1::2] = od
    return out


def _dft_mats():
    n = np.arange(DFT_LANE)
    ang = 2.0 * np.pi * np.outer(n, n) / DFT_LANE
    fr, fi = np.cos(ang), -np.sin(ang)
    fwd = np.block([[fr, fi], [-fi, fr]]).astype(np.float32)
    inv = np.block([[fr, -fi], [fi, fr]]).astype(np.float32)
    return fwd, inv


def _twiddle(n1):
    n = n1 * DFT_LANE
    ang = 2.0 * np.pi * np.outer(np.arange(n1), np.arange(DFT_LANE)) / n
    tw = np.stack([np.cos(ang), -np.sin(ang)], axis=1)
    return np.ascontiguousarray(np.broadcast_to(tw[:, :, None, :], (n1, 2, SUBLANES, DFT_LANE))).astype(np.float32)


def _filter_features(L):
    t = np.linspace(0.0, 1.0, L)[:, None]
    bands = (FILTER_EMB - 1) // 2
    f = np.linspace(1e-4, bands - 1, bands)
    ang = (2.0 * np.pi / L) * np.arange(L)[:, None] * f[None, :]
    z = np.concatenate([t, np.cos(ang), -np.sin(ang)], axis=-1)
    zp = np.zeros((L, LANES), np.float32)
    zp[:, :FILTER_EMB] = z
    return zp, t[:, 0].astype(np.float32)


def _rope_tables(L):
    pos = np.arange(L, dtype=np.float64)
    inv_freq = ROPE_THETA ** (-np.arange(0, QK_ROPE, 2, dtype=np.float64) / QK_ROPE)
    ang = pos[:, None] * inv_freq[None, :]
    c, s = np.cos(ang), np.sin(ang)
    ctab = np.concatenate([np.ones((L, QK_NOPE)), c, c], axis=1)
    stab = np.concatenate([np.zeros((L, QK_NOPE)), -s, s], axis=1)
    return ctab.astype(np.float32), stab.astype(np.float32)


def _in_proj_kernel(x_ref, gn_ref, wt_hy_ref, w_cq_ref, w_ckv_ref, w_za_ref, gcq_ref, gckv_ref,
                    ct_ref, st_ref, ctt_ref, stt_ref, wqt_ref, wk_ref, wvt_ref,
                    gqc_ref, gqsc_ref, gk_ref, gks_ref, ones_ref,
                    hyt_ref, zat_ref, qt_ref, k_ref, vt_ref, *, tl, sub):
    half = QK_ROPE // 2
    scale = math.log2(math.e) / math.sqrt(QK_DIM)

    def latent(c, g_ref):
        r = lax.rsqrt(jnp.mean(c * c, axis=-1, keepdims=True) + EPS)
        return (c * r * g_ref[...]).astype(BF16)

    for r0 in range(0, tl, sub):
        rows = slice(r0, r0 + sub)
        x = x_ref[0, rows, :]
        ms = jnp.mean(x * x, axis=-1, keepdims=True)
        h = (x * lax.rsqrt(ms + EPS) * gn_ref[...]).astype(BF16)

        cqr = jnp.dot(h, w_cq_ref[...], preferred_element_type=F32)
        cq = latent(cqr[:, :Q_LORA], gcq_ref)
        ckv = latent(jnp.dot(h, w_ckv_ref[...], preferred_element_type=F32), gckv_ref)
        kr_a = cqr[:, Q_LORA:]
        qt_all = lax.dot_general(wqt_ref[...], cq, _NT, preferred_element_type=F32)
        ka_all = jnp.dot(ckv, wk_ref[...], preferred_element_type=F32)
        vt_all = lax.dot_general(wvt_ref[...], ckv, _NT, preferred_element_type=F32) + ones_ref[...]

        hyt_ref[0, :, rows] = lax.dot_general(wt_hy_ref[...], h, _NT, preferred_element_type=F32).astype(BF16)
        zat_ref[0, rows, :] = jnp.dot(h, w_za_ref[...], preferred_element_type=F32).astype(BF16)

        qc, qs = gqc_ref[...] * ctt_ref[:, rows], gqsc_ref[...] * stt_ref[:, rows]
        kc, ks = gk_ref[...] * ct_ref[rows, :], gks_ref[...] * st_ref[rows, :]
        lane = lax.broadcasted_iota(jnp.int32, kr_a.shape, 1)
        kr_w = jnp.where(lane < QK_NOPE + half, pltpu.roll(kr_a, LANES - half, 1), pltpu.roll(kr_a, half, 1))
        for hd in range(MLA_HEADS):
            qa = qt_all[hd * QK_DIM:(hd + 1) * QK_DIM]
            qw = jnp.concatenate([qa[:QK_NOPE], qa[QK_NOPE + half:], qa[QK_NOPE:QK_NOPE + half]], axis=0)
            r = lax.rsqrt(jnp.mean(qa * qa, axis=0, keepdims=True) + EPS) * scale
            qt_ref[0, hd, :, rows] = ((qa * qc + qw * qs) * r).astype(BF16)
            ka = ka_all[:, hd * LANES:(hd + 1) * LANES] + kr_a
            rk = lax.rsqrt(jnp.sum(ka * ka, axis=-1, keepdims=True) * (1.0 / QK_DIM) + EPS)
            k_ref[0, hd, rows, :] = ((ka * kc + kr_w * ks) * rk)[:, :QK_DIM].astype(BF16)
            vt_ref[0, hd, :, rows] = vt_all[hd * V_ROWS:(hd + 1) * V_ROWS].astype(BF16)


def _in_proj(x, gn, wt_hy, w_cq, w_ckv, w_za, gcq, gckv, ctab, stab, ctab_t, stab_t,
             wqt, wk, wvt, gqc, gqsc, gk, gks, tl):
    ones = np.zeros((MLA_HEADS, V_ROWS, 1), np.float32)
    ones[:, V_DIM, 0] = 1.0
    ones = jnp.asarray(ones.reshape(MLA_HEADS * V_ROWS, 1))
    B, L, _ = x.shape
    tab = pl.BlockSpec((tl, LANES), lambda b, i: (i, 0))
    tab_t = pl.BlockSpec((QK_DIM, tl), lambda b, i: (0, i))
    consts = (gn, wt_hy, w_cq, w_ckv, w_za, gcq, gckv)
    qkv_consts = (wqt, wk, wvt, gqc, gqsc, gk, gks, ones)
    return pl.pallas_call(
        functools.partial(_in_proj_kernel, tl=tl, sub=min(512, tl)),
        grid=(B, L // tl),
        in_specs=[pl.BlockSpec((1, tl, D_MODEL), lambda b, i: (b, i, 0))]
        + [_resident(a.shape) for a in consts] + [tab, tab, tab_t, tab_t] + [_resident(a.shape) for a in qkv_consts],
        out_specs=[
            pl.BlockSpec((1, 4 * HY_WIDTH, tl), lambda b, i: (b, 0, i)),
            pl.BlockSpec((1, tl, ATTN_WIDTH), lambda b, i: (b, i, 0)),
            pl.BlockSpec((1, MLA_HEADS, QK_DIM, tl), lambda b, i: (b, 0, 0, i)),
            pl.BlockSpec((1, MLA_HEADS, tl, QK_DIM), lambda b, i: (b, 0, i, 0)),
            pl.BlockSpec((1, MLA_HEADS, V_ROWS, tl), lambda b, i: (b, 0, 0, i)),
        ],
        out_shape=[
            jax.ShapeDtypeStruct((B, 4 * HY_WIDTH, L), BF16),
            jax.ShapeDtypeStruct((B, L, ATTN_WIDTH), BF16),
            jax.ShapeDtypeStruct((B, MLA_HEADS, QK_DIM, L), BF16),
            jax.ShapeDtypeStruct((B, MLA_HEADS, L, QK_DIM), BF16),
            jax.ShapeDtypeStruct((B, MLA_HEADS, V_ROWS, L), BF16),
        ],
        compiler_params=_cparams(("parallel", "parallel")),
        name="in_proj",
    )(x, *consts, ctab, stab, ctab_t, stab_t, *qkv_consts)


def _fmlp_kernel(zt_ref, w1t_ref, b1_ref, f1_ref, w2t_ref, b2_ref, f2_ref, out_ref):
    hp = lax.Precision.HIGHEST
    h = jnp.sin(f1_ref[...] * (jnp.dot(w1t_ref[...], zt_ref[0], precision=hp, preferred_element_type=F32) + b1_ref[...]))
    out_ref[0] = jnp.sin(f2_ref[...] * (jnp.dot(w2t_ref[...], h, precision=hp, preferred_element_type=F32) + b2_ref[...]))


def _fmlp(zt, w1t, b1, f1, w2t, b2, f2):
    _, _, L = zt.shape
    return pl.pallas_call(
        _fmlp_kernel,
        grid=(2,),
        in_specs=[pl.BlockSpec((1, LANES, L), lambda d: (d, 0, 0)), _resident(w1t.shape), _resident(b1.shape),
                  _resident(f1.shape), _resident(w2t.shape), _resident(b2.shape), _resident(f2.shape)],
        out_specs=pl.BlockSpec((1, FILTER_HIDDEN, L), lambda d: (d, 0, 0)),
        out_shape=jax.ShapeDtypeStruct((2, FILTER_HIDDEN, L), F32),
        compiler_params=_cparams(("parallel",)),
        name="fmlp",
    )(zt, w1t, b1, f1, w2t, b2, f2)


def _stage_a_forward(load, g_scr, tw_ref, n1, rows_total):
    def body(r, carry):
        rs = pl.ds(pl.multiple_of(r * SUBLANES, SUBLANES), SUBLANES)
        for half in range(DFT_LANE // LANES):
            ls = slice(half * LANES, (half + 1) * LANES)
            li = slice(DFT_LANE + half * LANES, DFT_LANE + (half + 1) * LANES)
            xs = [load(i, rs, ls) for i in range(n1)]
            xs = [(None, None) if x is None else x for x in xs]
            X = _fft_dif(xs, -1.0)
            for k1 in range(n1):
                if tw_ref is None:
                    yr, yi = X[k1]
                else:
                    yr, yi = _cmul(X[k1], (tw_ref[k1, 0, :, ls], tw_ref[k1, 1, :, ls]))
                g_scr[k1, rs, ls] = yr
                g_scr[k1, rs, li] = yi
        return carry

    lax.fori_loop(0, rows_total // SUBLANES, body, 0)


def _fspec_kernel(h_ref, t_ref, w3f_ref, w3b_ref, dl_ref, wf_ref, tw_ref, out_ref, k_scr, g_scr,
                  *, L, n1, rows, group):
    dl = dl_ref[...]
    for d, w_ref in enumerate((w3f_ref, w3b_ref)):
        k = jnp.dot(w_ref[...].astype(BF16), h_ref[d].astype(BF16), preferred_element_type=F32)
        k_scr[:, d * L:(d + 1) * L] = k * jnp.exp(-(dl * t_ref[d]))

    def load(i, rs, ls):
        return (k_scr[rs, i * DFT_LANE + ls.start:i * DFT_LANE + ls.stop], None)

    _stage_a_forward(load, g_scr, tw_ref, n1, rows)
    inv_n = 1.0 / (n1 * DFT_LANE)
    w = wf_ref[...]
    for g0 in range(0, n1, group):
        lhs = g_scr[g0:g0 + group].reshape(group * rows, 2 * DFT_LANE).astype(BF16)
        res = jnp.dot(lhs, w, preferred_element_type=F32)
        out_ref[0, g0:g0 + group] = res.reshape(group, rows, 2 * DFT_LANE) * inv_n


def _fspec(h2t, t2, w3f, w3b, dl, wf, tw, L, n1, rows, group):
    n_rows = w3f.shape[0]
    tiles_per_order = HY_WIDTH // rows
    row_spec = lambda shape: pl.BlockSpec(shape, lambda i: (i, 0))
    return pl.pallas_call(
        functools.partial(_fspec_kernel, L=L, n1=n1, rows=rows, group=group),
        grid=(n_rows // rows,),
        in_specs=[_resident(h2t.shape), _resident(t2.shape),
                  row_spec((rows, FILTER_HIDDEN)), row_spec((rows, FILTER_HIDDEN)), row_spec((rows, 1)),
                  _resident(wf.shape), _resident(tw.shape)],
        out_specs=pl.BlockSpec((1, n1, rows, 2 * DFT_LANE),
                               lambda i: (i // tiles_per_order, 0, i % tiles_per_order, 0)),
        out_shape=jax.ShapeDtypeStruct((n_rows // HY_WIDTH, n1, HY_WIDTH, 2 * DFT_LANE), F32),
        scratch_shapes=[pltpu.VMEM((rows, 2 * L), F32), pltpu.VMEM((n1, rows, 2 * DFT_LANE), F32)],
        compiler_params=_cparams(("parallel",)),
        name="fspec",
    )(h2t, t2, w3f, w3b, dl, wf, tw)


def _hyena_kernel(hy_ref, wsh_ref, bsh_ref, ksp_ref, bias_ref, wf_ref, wi_ref, tw_ref, out_ref,
                  u_scr, x1_scr, x2_scr, g_scr, *, L, n1, ct, group):
    lane = lax.broadcasted_iota(jnp.int32, (ct, LANES), 1)

    def short_conv(g, dst):
        for bi in range(2):
            p = hy_ref[bi, g].astype(F32)
            left = pltpu.roll(p, 1, 1)
            left = jnp.concatenate([jnp.where(lane == 0, 0.0, left[:, :LANES]), left[:, LANES:]], axis=1)
            right = pltpu.roll(p, L - 1, 1)
            right = jnp.concatenate([right[:, :L - LANES], jnp.where(lane == LANES - 1, 0.0, right[:, L - LANES:])],
                                    axis=1)
            conv = wsh_ref[g, 0] * left + wsh_ref[g, 1] * p + wsh_ref[g, 2] * right + bsh_ref[g]
            if g == 2:
                z = hy_ref[bi, 3].astype(F32)
                conv = conv * (z * jax.nn.sigmoid(z))
            dst[bi] = conv

    short_conv(0, u_scr)
    gate_scr = (x1_scr, x2_scr)

    half_blocks = n1 // 2
    wf = wf_ref[...]
    wi = wi_ref[...]
    re, im = slice(0, DFT_LANE), slice(DFT_LANE, 2 * DFT_LANE)

    def mxu_dft(x, w):
        res = jnp.dot(x.reshape(group * ct, 2 * DFT_LANE).astype(BF16), w, preferred_element_type=F32)
        res = res.reshape(group, ct, 2 * DFT_LANE)
        return res[..., re], res[..., im]

    for order in range(2):
        def load(i, rs, ls):
            if i >= half_blocks:
                return None
            cs = slice(i * DFT_LANE + ls.start, i * DFT_LANE + ls.stop)
            return (u_scr[0, rs, cs], u_scr[1, rs, cs])

        _stage_a_forward(load, g_scr, None, n1, ct)

        def twiddled(gs):
            g = g_scr[gs]
            gr, gi = g[..., re], g[..., im]
            twr = tw_ref[gs, 0, 0:1, :]
            twi = tw_ref[gs, 1, 0:1, :]
            return jnp.concatenate([gr * twr - gi * twi, gr * twi + gi * twr], axis=-1)

        def untwiddle(gs, q):
            qr, qi = q
            twr = tw_ref[gs, 0, 0:1, :]
            twi = tw_ref[gs, 1, 0:1, :]
            g_scr[gs, :, re] = qr * twr + qi * twi
            g_scr[gs, :, im] = qi * twr - qr * twi

        groups = [slice(g0, g0 + group) for g0 in range(0, n1, group)]
        spectra = [mxu_dft(twiddled(gs), wf) for gs in groups]
        short_conv(order + 1, gate_scr[order])
        prev = None
        for gs, (hr, hi) in zip(groups, spectra):
            kr, ki = ksp_ref[order, gs, :, re], ksp_ref[order, gs, :, im]
            q = mxu_dft(jnp.concatenate([hr * kr - hi * ki, hr * ki + hi * kr], axis=-1), wi)
            if prev is not None:
                untwiddle(*prev)
            prev = (gs, q)
        untwiddle(*prev)

        def body(r, carry, order=order):
            rs = pl.ds(pl.multiple_of(r * SUBLANES, SUBLANES), SUBLANES)
            bias = bias_ref[order, rs, :]
            for half in range(DFT_LANE // LANES):
                ls = slice(half * LANES, (half + 1) * LANES)
                li = slice(DFT_LANE + half * LANES, DFT_LANE + (half + 1) * LANES)
                q = [(g_scr[k1, rs, ls], g_scr[k1, rs, li]) for k1 in range(n1)]
                ev = _fft_dif(q[0::2], 1.0)
                od = _fft_dif(q[1::2], 1.0)
                for i in range(half_blocks):
                    y = _cadd(ev[i], _cmulc(od[i], np.exp(2j * np.pi * i / n1)))
                    cs = slice(i * DFT_LANE + ls.start, i * DFT_LANE + ls.stop)
                    for bi in range(2):
                        u = u_scr[bi, rs, cs]
                        conv = y[bi] + bias * u
                        if order == 0:
                            u_scr[bi, rs, cs] = x1_scr[bi, rs, cs] * conv
                        else:
                            out_ref[bi, rs, cs] = x2_scr[bi, rs, cs] * conv
            return carry

        lax.fori_loop(0, ct // SUBLANES, body, 0)


def _hyena(hyt4, wsh, bsh, kspec, bias, wf, wi, tw, n1, ct, group):
    B, _, C, L = hyt4.shape
    return pl.pallas_call(
        functools.partial(_hyena_kernel, L=L, n1=n1, ct=ct, group=group),
        grid=(C // ct, B // 2),
        in_specs=[
            pl.BlockSpec((2, 4, ct, L), lambda c, p: (p, 0, c, 0)),
            pl.BlockSpec((3, 3, ct, 1), lambda c, p: (0, 0, c, 0)),
            pl.BlockSpec((3, ct, 1), lambda c, p: (0, c, 0)),
            pl.BlockSpec((2, n1, ct, 2 * DFT_LANE), lambda c, p: (0, 0, c, 0)),
            pl.BlockSpec((2, ct, 1), lambda c, p: (0, c, 0)),
            _resident(wf.shape), _resident(wi.shape), _resident(tw.shape),
        ],
        out_specs=pl.BlockSpec((2, ct, L), lambda c, p: (p, c, 0)),
        out_shape=jax.ShapeDtypeStruct((B, C, L), F32),
        scratch_shapes=[pltpu.VMEM((2, ct, L), F32), pltpu.VMEM((2, ct, L), F32), pltpu.VMEM((2, ct, L), F32),
                        pltpu.VMEM((n1, ct, 2 * DFT_LANE), F32)],
        compiler_params=_cparams(("parallel", "parallel")),
        name="hyena",
    )(hyt4, wsh, bsh, kspec, bias, wf, wi, tw)


def _attn_kernel(qt_ref, k_ref, vt_ref, o_ref, *, L, tq):
    units = [(j, t0) for t0 in range(0, tq, Q_TILE) for j in range(2)]
    starts = list(range(0, L, KV_CHUNK))

    def scores(u, c0):
        j, t0 = u
        return jnp.dot(k_ref[0, j, c0:c0 + KV_CHUNK, :], qt_ref[0, j, :, t0:t0 + Q_TILE],
                       preferred_element_type=F32)

    m = {u: None for u in units}
    acc = {u: None for u in units}
    cur = {u: scores(u, starts[0]) for u in units}
    for i, c0 in enumerate(starts):
        nxt = {}
        for u in units:
            if i + 1 < len(starts):
                nxt[u] = scores(u, starts[i + 1])
            s = cur[u]
            mc = jnp.max(s, axis=0, keepdims=True)
            m_new = mc if m[u] is None else jnp.maximum(m[u], mc)
            p = jnp.exp2(s - m_new).astype(BF16)
            pv = jnp.dot(vt_ref[0, u[0], :, c0:c0 + KV_CHUNK], p, preferred_element_type=F32)
            acc[u] = pv if m[u] is None else jnp.exp2(m[u] - m_new) * acc[u] + pv
            m[u] = m_new
        cur = nxt
    for j, t0 in units:
        a = acc[(j, t0)]
        o_ref[0, j * V_DIM:(j + 1) * V_DIM, t0:t0 + Q_TILE] = (a[:V_DIM] / a[V_DIM:V_DIM + 1]).astype(BF16)


def _attn(qt, k, vt, tq):
    B, _, L, _ = k.shape
    return pl.pallas_call(
        functools.partial(_attn_kernel, L=L, tq=tq),
        grid=(B, HEAD_PAIRS, L // tq),
        in_specs=[pl.BlockSpec((1, 2, QK_DIM, tq), lambda b, p, i: (b, p, 0, i)),
                  pl.BlockSpec((1, 2, L, QK_DIM), lambda b, p, i: (b, p, 0, 0)),
                  pl.BlockSpec((1, 2, V_ROWS, L), lambda b, p, i: (b, p, 0, 0))],
        out_specs=pl.BlockSpec((1, 2 * V_DIM, tq), lambda b, p, i: (b, p, i)),
        out_shape=jax.ShapeDtypeStruct((B, ATTN_WIDTH, L), BF16),
        compiler_params=_cparams(("parallel", "parallel", "parallel")),
        name="attn",
    )(qt, k, vt)


def _final_kernel(x_ref, yt_ref, at_ref, zat_ref, gn_ref, wg_ref, bg_ref, why_ref, wat_ref, wout_ref, o_ref,
                  *, tl, sub):
    for r0 in range(0, tl, sub):
        rows = slice(r0, r0 + sub)
        x = x_ref[0, rows, :]
        h = (x * lax.rsqrt(jnp.mean(x * x, axis=-1, keepdims=True) + EPS) * gn_ref[...]).astype(BF16)
        gates = jax.nn.sigmoid(jnp.dot(h, wg_ref[...], preferred_element_type=F32) + bg_ref[...])
        y = yt_ref[0, :, rows].T.astype(BF16)
        u_hy = jnp.dot(y, why_ref[...], preferred_element_type=F32)
        z = zat_ref[0, rows, :].astype(F32)
        a = (at_ref[0, :, rows].astype(F32).T * (z * jax.nn.sigmoid(z))).astype(BF16)
        u_at = jnp.dot(a, wat_ref[...], preferred_element_type=F32)
        merged = gates[:, :D_MODEL] * u_hy + gates[:, D_MODEL:] * u_at
        o_ref[0, rows, :] = x + jnp.dot(merged.astype(BF16), wout_ref[...], preferred_element_type=F32)


def _final(x, yt, at, zat, gn, wg, bg, why, wat, wout, tl):
    B, L, _ = x.shape
    tok = lambda w: pl.BlockSpec((1, tl, w), lambda b, i: (b, i, 0))
    consts = (gn, wg, bg, why, wat, wout)
    return pl.pallas_call(
        functools.partial(_final_kernel, tl=tl, sub=min(512, tl)),
        grid=(B, L // tl),
        in_specs=[tok(D_MODEL), pl.BlockSpec((1, HY_WIDTH, tl), lambda b, i: (b, 0, i)),
                  pl.BlockSpec((1, ATTN_WIDTH, tl), lambda b, i: (b, 0, i)), tok(ATTN_WIDTH)]
        + [_resident(a.shape) for a in consts],
        out_specs=tok(D_MODEL),
        out_shape=jax.ShapeDtypeStruct((B, L, D_MODEL), F32),
        compiler_params=_cparams(("parallel", "parallel")),
        name="final",
    )(x, yt, at, zat, *consts)


def _layer(x, g_norm, w_in, b_gate, w_short, b_short, w_f1, b_f1, freq_1, w_f2, b_f2, freq_2, w_f3,
           hy_bias, w_hy_out, g_cq, w_uq, g_ckv, w_ukv, g_qn, g_kn, w_attn_out, w_out):
    B, L, _ = x.shape
    n = 2 * L
    n1 = n // DFT_LANE
    assert B % 2 == 0 and n1 * DFT_LANE == n and n1 >= 2 and (n1 & (n1 - 1)) == 0
    tl = min(1024, L)
    ct = 64
    group = min(8, n1)

    o_z, o_cq, o_ckv, o_kr, o_za = 3 * HY_WIDTH, 4 * HY_WIDTH, 4 * HY_WIDTH + Q_LORA, 4 * HY_WIDTH + Q_LORA + KV_LORA, \
        4 * HY_WIDTH + Q_LORA + KV_LORA + QK_ROPE
    del o_z
    w_in = w_in.astype(BF16)
    wt_hy = w_in[:, :4 * HY_WIDTH].T
    half = QK_ROPE // 2
    w_cq = jnp.concatenate([w_in[:, o_cq:o_ckv], jnp.zeros((D_MODEL, QK_NOPE), BF16), w_in[:, o_kr:o_za],
                            jnp.zeros((D_MODEL, LANES - QK_DIM), BF16)], axis=1)
    w_ckv = w_in[:, o_ckv:o_kr]
    w_za = w_in[:, o_za:o_za + ATTN_WIDTH]
    w_gate = w_in[:, o_za + ATTN_WIDTH:]

    perm = np.concatenate([np.arange(QK_NOPE), QK_NOPE + half + np.arange(half), QK_NOPE + np.arange(half)])
    wqt = w_uq.T.astype(BF16)
    wkv = w_ukv.reshape(KV_LORA, MLA_HEADS, QK_NOPE + V_DIM)
    wk = jnp.pad(wkv[:, :, :QK_NOPE], ((0, 0), (0, 0), (0, LANES - QK_NOPE)))
    wk = wk.reshape(KV_LORA, MLA_HEADS * LANES).astype(BF16)
    wvt = jnp.pad(wkv[:, :, QK_NOPE:], ((0, 0), (0, 0), (0, V_ROWS - V_DIM)))
    wvt = wvt.reshape(KV_LORA, MLA_HEADS * V_ROWS).T.astype(BF16)
    pad_lanes = lambda g: jnp.pad(g, ((0, 0), (0, LANES - QK_DIM)))
    gqc = g_qn.reshape(QK_DIM, 1)
    gqsc = gqc[perm]
    gk = g_kn.reshape(1, QK_DIM)
    gks = pad_lanes(gk[:, perm])
    gk = pad_lanes(gk)

    ctab, stab = _rope_tables(L)
    pad_tab = lambda tab: np.pad(tab, ((0, 0), (0, LANES - QK_DIM)))
    hyt, zat, qt, k, vt = _in_proj(
        x, g_norm.reshape(1, D_MODEL), wt_hy, w_cq, w_ckv, w_za, g_cq.reshape(1, Q_LORA),
        g_ckv.reshape(1, KV_LORA), jnp.asarray(pad_tab(ctab)), jnp.asarray(pad_tab(stab)),
        jnp.asarray(np.ascontiguousarray(ctab.T)), jnp.asarray(np.ascontiguousarray(stab.T)),
        wqt, wk, wvt, gqc, gqsc, gk, gks, tl)

    zfeat, t = _filter_features(L)
    w1 = jnp.concatenate([w_f1, jnp.zeros((LANES - FILTER_EMB, FILTER_HIDDEN), F32)], axis=0)
    w3 = w_f3.reshape(FILTER_HIDDEN, 2, 2, HY_WIDTH)
    w3f = w3[:, :, 0].reshape(FILTER_HIDDEN, 2 * HY_WIDTH).T
    w3b = w3[:, :, 1].reshape(FILTER_HIDDEN, 2 * HY_WIDTH).T
    deltas = np.abs(np.linspace(MIN_DECAY, MAX_DECAY, HY_WIDTH))
    dl = np.tile(deltas, 2).reshape(2 * HY_WIDTH, 1).astype(np.float32)
    col = lambda v: v.reshape(FILTER_HIDDEN, 1)
    zt = np.ascontiguousarray(np.stack([zfeat.T, zfeat[::-1].T]))
    t2 = np.ascontiguousarray(np.stack([t, t[::-1]]).reshape(2, 1, L))
    h2t = _fmlp(jnp.asarray(zt), w1.T, col(b_f1), col(freq_1), w_f2.T, col(b_f2), col(freq_2))
    wf, wi = (jnp.asarray(w).astype(BF16) for w in _dft_mats())
    tw = jnp.asarray(_twiddle(n1))
    kspec = _fspec(h2t, jnp.asarray(t2), w3f, w3b, jnp.asarray(dl), wf, tw, L, n1, 2 * ct, group)

    wsh = w_short.reshape(3, 3, HY_WIDTH).transpose(1, 0, 2)[..., None]
    bsh = b_short.reshape(3, HY_WIDTH, 1)
    yt = _hyena(hyt.reshape(B, 4, HY_WIDTH, L), wsh, bsh, kspec, hy_bias.reshape(2, HY_WIDTH, 1),
                wf, wi, tw, n1, ct, group)

    at = _attn(qt, k, vt, min(2048, L))

    return _final(x, yt, at, zat, g_norm.reshape(1, D_MODEL), w_gate, b_gate.reshape(1, 2 * D_MODEL),
                  w_hy_out.astype(BF16), w_attn_out.astype(BF16), w_out.astype(BF16), min(1024, L))


def kernel(x, g_norm, w_in, b_gate, w_short, b_short, w_f1, b_f1, freq_1, w_f2, b_f2, freq_2, w_f3, hy_bias, w_hy_out, g_cq, w_uq, g_ckv, w_ukv, g_qn, g_kn, w_attn_out, w_out):
    depth = g_norm.shape[0]
    for l in range(depth):
        x = _layer(x, g_norm[l], w_in[l], b_gate[l], w_short[l], b_short[l], w_f1[l], b_f1[l], freq_1[l], w_f2[l],
                   b_f2[l], freq_2[l], w_f3[l], hy_bias[l], w_hy_out[l], g_cq[l], w_uq[l], g_ckv[l], w_ukv[l],
                   g_qn[l], g_kn[l], w_attn_out[l], w_out[l])
    return x
```

```python
import functools
import math

import numpy as np
import jax
import jax.numpy as jnp
from jax import lax
from jax.experimental import pallas as pl
from jax.experimental.pallas import tpu as pltpu

D_MODEL = 1024
HY_WIDTH = 512
FILTER_EMB = 33
FILTER_HIDDEN = 64
DECAY_TARGET = 1e-2
FAST_DECAY = 0.3
SLOW_DECAY = 1.5
MIN_DECAY = math.log(DECAY_TARGET) / SLOW_DECAY
MAX_DECAY = math.log(DECAY_TARGET) / FAST_DECAY
MLA_HEADS = 8
HEAD_PAIRS = MLA_HEADS // 2
QK_NOPE = 64
QK_ROPE = 32
QK_DIM = QK_NOPE + QK_ROPE
V_DIM = 64
Q_LORA = 384
KV_LORA = 256
ATTN_WIDTH = MLA_HEADS * V_DIM
ROPE_THETA = 10000.0
EPS = 1e-6

LANES = 128
SUBLANES = 8
DFT_LANE = 256
V_ROWS = 80
KV_CHUNK = 256
Q_TILE = 256
VMEM_LIMIT = 56 * 1024 * 1024

F32 = jnp.float32
BF16 = jnp.bfloat16
_NT = (((1,), (1,)), ((), ()))


def _cparams(sem):
    return pltpu.CompilerParams(dimension_semantics=sem, vmem_limit_bytes=VMEM_LIMIT)


def _resident(shape):
    nd = len(shape)
    return pl.BlockSpec(shape, lambda *_: (0,) * nd, pipeline_mode=pl.Buffered(1))


def _add(a, b):
    if a is None:
        return b
    if b is None:
        return a
    return a + b


def _sub(a, b):
    if b is None:
        return a
    if a is None:
        return -b
    return a - b


def _cadd(x, y):
    return (_add(x[0], y[0]), _add(x[1], y[1]))


def _csub(x, y):
    return (_sub(x[0], y[0]), _sub(x[1], y[1]))


def _scale(a, s):
    if a is None or s == 0.0:
        return None
    if s == 1.0:
        return a
    if s == -1.0:
        return -a
    return a * s


def _signed_sum(terms):
    pos = [a for sg, a in terms if a is not None and sg > 0]
    neg = [a for sg, a in terms if a is not None and sg < 0]
    if pos:
        r = pos[0]
        for a in pos[1:]:
            r = r + a
        for a in neg:
            r = r - a
        return r, 1.0
    if neg:
        r = neg[0]
        for a in neg[1:]:
            r = r + a
        return r, -1.0
    return None, 1.0


def _cmulc(x, w):
    wr, wi = float(np.real(w)), float(np.imag(w))
    if abs(wr) < 1e-15:
        wr = 0.0
    if abs(wi) < 1e-15:
        wi = 0.0
    xr, xi = x
    if wr != 0.0 and abs(abs(wr) - abs(wi)) < 1e-12:
        sr, si, c = np.sign(wr), np.sign(wi), abs(wr)
        re, sre = _signed_sum([(sr, xr), (-si, xi)])
        im, sim = _signed_sum([(si, xr), (sr, xi)])
        return (_scale(re, sre * c), _scale(im, sim * c))
    re = _sub(_scale(xr, wr), _scale(xi, wi))
    im = _add(_scale(xr, wi), _scale(xi, wr))
    return (re, im)


def _cdiff_mulc(x, y, w):
    wr, wi = float(np.real(w)), float(np.imag(w))
    (xr, xi), (yr, yi) = x, y
    unit = {(1, 0): ((1, xr, yr), (1, xi, yi)), (-1, 0): ((-1, xr, yr), (-1, xi, yi)),
            (0, 1): ((-1, xi, yi), (1, xr, yr)), (0, -1): ((1, xi, yi), (-1, xr, yr))}
    key = (int(round(wr)), int(round(wi)))
    if abs(wr - key[0]) < 1e-12 and abs(wi - key[1]) < 1e-12 and key in unit:
        parts = []
        for sg, a, b in unit[key]:
            v, sv = _signed_sum([(sg, a), (-sg, b)])
            parts.append(_scale(v, sv))
        return tuple(parts)
    return _cmulc(_csub(x, y), w)


def _cmul(x, y):
    xr, xi = x
    yr, yi = y
    re = _sub(None if xr is None else xr * yr, None if xi is None else xi * yi)
    im = _add(None if xr is None else xr * yi, None if xi is None else xi * yr)
    return (re, im)


def _fft_dif(xs, sign):
    n = len(xs)
    if n == 1:
        return xs
    half = n // 2
    a = [_cadd(xs[i], xs[i + half]) for i in range(half)]
    b = [_cdiff_mulc(xs[i], xs[i + half], np.exp(sign * 2j * np.pi * i / n)) for i in range(half)]
    ev = _fft_dif(a, sign)
    od = _fft_dif(b, sign)
    out = [None] * n
    out[0::2] = ev
    out[1::2] = od
    return out


def _dft_mats():
    n = np.arange(DFT_LANE)
    ang = 2.0 * np.pi * np.outer(n, n) / DFT_LANE
    fr, fi = np.cos(ang), -np.sin(ang)
    fwd = np.block([[fr, fi], [-fi, fr]]).astype(np.float32)
    inv = np.block([[fr, -fi], [fi, fr]]).astype(np.float32)
    return fwd, inv


def _twiddle(n1):
    n = n1 * DFT_LANE
    ang = 2.0 * np.pi * np.outer(np.arange(n1), np.arange(DFT_LANE)) / n
    tw = np.stack([np.cos(ang), -np.sin(ang)], axis=1)
    return np.ascontiguousarray(np.broadcast_to(tw[:, :, None, :], (n1, 2, SUBLANES, DFT_LANE))).astype(np.float32)


def _filter_features(L):
    t = np.linspace(0.0, 1.0, L)[:, None]
    bands = (FILTER_EMB - 1) // 2
    f = np.linspace(1e-4, bands - 1, bands)
    ang = (2.0 * np.pi / L) * np.arange(L)[:, None] * f[None, :]
    z = np.concatenate([t, np.cos(ang), -np.sin(ang)], axis=-1)
    zp = np.zeros((L, LANES), np.float32)
    zp[:, :FILTER_EMB] = z
    return zp, t[:, 0].astype(np.float32)


def _rope_tables(L):
    pos = np.arange(L, dtype=np.float64)
    inv_freq = ROPE_THETA ** (-np.arange(0, QK_ROPE, 2, dtype=np.float64) / QK_ROPE)
    ang = pos[:, None] * inv_freq[None, :]
    c, s = np.cos(ang), np.sin(ang)
    ctab = np.concatenate([np.ones((L, QK_NOPE)), c, c], axis=1)
    stab = np.concatenate([np.zeros((L, QK_NOPE)), -s, s], axis=1)
    return ctab.astype(np.float32), stab.astype(np.float32)


def _in_proj_kernel(x_ref, gn_ref, wt_hy_ref, w_cq_ref, w_ckv_ref, w_za_ref, gcq_ref, gckv_ref,
                    ct_ref, st_ref, ctt_ref, stt_ref, wqt_ref, wk_ref, wvt_ref,
                    gqc_ref, gqsc_ref, gk_ref, gks_ref, ones_ref,
                    hyt_ref, zat_ref, qt_ref, k_ref, vt_ref, *, tl, sub):
    half = QK_ROPE // 2
    scale = math.log2(math.e) / math.sqrt(QK_DIM)

    def latent(c, g_ref):
        r = lax.rsqrt(jnp.mean(c * c, axis=-1, keepdims=True) + EPS)
        return (c * r * g_ref[...]).astype(BF16)

    for r0 in range(0, tl, sub):
        rows = slice(r0, r0 + sub)
        x = x_ref[0, rows, :]
        ms = jnp.mean(x * x, axis=-1, keepdims=True)
        h = (x * lax.rsqrt(ms + EPS) * gn_ref[...]).astype(BF16)

        cqr = jnp.dot(h, w_cq_ref[...], preferred_element_type=F32)
        cq = latent(cqr[:, :Q_LORA], gcq_ref)
        ckv = latent(jnp.dot(h, w_ckv_ref[...], preferred_element_type=F32), gckv_ref)
        kr_a = cqr[:, Q_LORA:]
        qt_all = lax.dot_general(wqt_ref[...], cq, _NT, preferred_element_type=F32)
        ka_all = jnp.dot(ckv, wk_ref[...], preferred_element_type=F32)
        vt_all = lax.dot_general(wvt_ref[...], ckv, _NT, preferred_element_type=F32) + ones_ref[...]

        hyt_ref[0, :, rows] = lax.dot_general(wt_hy_ref[...], h, _NT, preferred_element_type=F32).astype(BF16)
        zat_ref[0, rows, :] = jnp.dot(h, w_za_ref[...], preferred_element_type=F32).astype(BF16)

        qc, qs = gqc_ref[...] * ctt_ref[:, rows], gqsc_ref[...] * stt_ref[:, rows]
        kc, ks = gk_ref[...] * ct_ref[rows, :], gks_ref[...] * st_ref[rows, :]
        lane = lax.broadcasted_iota(jnp.int32, kr_a.shape, 1)
        kr_w = jnp.where(lane < QK_NOPE + half, pltpu.roll(kr_a, LANES - half, 1), pltpu.roll(kr_a, half, 1))
        for hd in range(MLA_HEADS):
            qa = qt_all[hd * QK_DIM:(hd + 1) * QK_DIM]
            qw = jnp.concatenate([qa[:QK_NOPE], qa[QK_NOPE + half:], qa[QK_NOPE:QK_NOPE + half]], axis=0)
            r = lax.rsqrt(jnp.mean(qa * qa, axis=0, keepdims=True) + EPS) * scale
            qt_ref[0, hd, :, rows] = ((qa * qc + qw * qs) * r).astype(BF16)
            ka = ka_all[:, hd * LANES:(hd + 1) * LANES] + kr_a
            rk = lax.rsqrt(jnp.sum(ka * ka, axis=-1, keepdims=True) * (1.0 / QK_DIM) + EPS)
            k_ref[0, hd, rows, :] = ((ka * kc + kr_w * ks) * rk)[:, :QK_DIM].astype(BF16)
            vt_ref[0, hd, :, rows] = vt_all[hd * V_ROWS:(hd + 1) * V_ROWS].astype(BF16)


def _in_proj(x, gn, wt_hy, w_cq, w_ckv, w_za, gcq, gckv, ctab, stab, ctab_t, stab_t,
             wqt, wk, wvt, gqc, gqsc, gk, gks, tl):
    ones = np.zeros((MLA_HEADS, V_ROWS, 1), np.float32)
    ones[:, V_DIM, 0] = 1.0
    ones = jnp.asarray(ones.reshape(MLA_HEADS * V_ROWS, 1))
    B, L, _ = x.shape
    tab = pl.BlockSpec((tl, LANES), lambda b, i: (i, 0))
    tab_t = pl.BlockSpec((QK_DIM, tl), lambda b, i: (0, i))
    consts = (gn, wt_hy, w_cq, w_ckv, w_za, gcq, gckv)
    qkv_consts = (wqt, wk, wvt, gqc, gqsc, gk, gks, ones)
    return pl.pallas_call(
        functools.partial(_in_proj_kernel, tl=tl, sub=min(512, tl)),
        grid=(B, L // tl),
        in_specs=[pl.BlockSpec((1, tl, D_MODEL), lambda b, i: (b, i, 0))]
        + [_resident(a.shape) for a in consts] + [tab, tab, tab_t, tab_t] + [_resident(a.shape) for a in qkv_consts],
        out_specs=[
            pl.BlockSpec((1, 4 * HY_WIDTH, tl), lambda b, i: (b, 0, i)),
            pl.BlockSpec((1, tl, ATTN_WIDTH), lambda b, i: (b, i, 0)),
            pl.BlockSpec((1, MLA_HEADS, QK_DIM, tl), lambda b, i: (b, 0, 0, i)),
            pl.BlockSpec((1, MLA_HEADS, tl, QK_DIM), lambda b, i: (b, 0, i, 0)),
            pl.BlockSpec((1, MLA_HEADS, V_ROWS, tl), lambda b, i: (b, 0, 0, i)),
        ],
        out_shape=[
            jax.ShapeDtypeStruct((B, 4 * HY_WIDTH, L), BF16),
            jax.ShapeDtypeStruct((B, L, ATTN_WIDTH), BF16),
            jax.ShapeDtypeStruct((B, MLA_HEADS, QK_DIM, L), BF16),
            jax.ShapeDtypeStruct((B, MLA_HEADS, L, QK_DIM), BF16),
            jax.ShapeDtypeStruct((B, MLA_HEADS, V_ROWS, L), BF16),
        ],
        compiler_params=_cparams(("parallel", "parallel")),
        name="in_proj",
    )(x, *consts, ctab, stab, ctab_t, stab_t, *qkv_consts)


def _fmlp_kernel(zt_ref, w1t_ref, b1_ref, f1_ref, w2t_ref, b2_ref, f2_ref, out_ref):
    hp = lax.Precision.HIGHEST
    h = jnp.sin(f1_ref[...] * (jnp.dot(w1t_ref[...], zt_ref[0], precision=hp, preferred_element_type=F32) + b1_ref[...]))
    out_ref[0] = jnp.sin(f2_ref[...] * (jnp.dot(w2t_ref[...], h, precision=hp, preferred_element_type=F32) + b2_ref[...]))


def _fmlp(zt, w1t, b1, f1, w2t, b2, f2):
    _, _, L = zt.shape
    return pl.pallas_call(
        _fmlp_kernel,
        grid=(2,),
        in_specs=[pl.BlockSpec((1, LANES, L), lambda d: (d, 0, 0)), _resident(w1t.shape), _resident(b1.shape),
                  _resident(f1.shape), _resident(w2t.shape), _resident(b2.shape), _resident(f2.shape)],
        out_specs=pl.BlockSpec((1, FILTER_HIDDEN, L), lambda d: (d, 0, 0)),
        out_shape=jax.ShapeDtypeStruct((2, FILTER_HIDDEN, L), F32),
        compiler_params=_cparams(("parallel",)),
        name="fmlp",
    )(zt, w1t, b1, f1, w2t, b2, f2)


def _stage_a_forward(load, g_scr, tw_ref, n1, rows_total):
    def body(r, carry):
        rs = pl.ds(pl.multiple_of(r * SUBLANES, SUBLANES), SUBLANES)
        for half in range(DFT_LANE // LANES):
            ls = slice(half * LANES, (half + 1) * LANES)
            li = slice(DFT_LANE + half * LANES, DFT_LANE + (half + 1) * LANES)
            xs = [load(i, rs, ls) for i in range(n1)]
            xs = [(None, None) if x is None else x for x in xs]
            X = _fft_dif(xs, -1.0)
            for k1 in range(n1):
                if tw_ref is None:
                    yr, yi = X[k1]
                else:
                    yr, yi = _cmul(X[k1], (tw_ref[k1, 0, :, ls], tw_ref[k1, 1, :, ls]))
                g_scr[k1, rs, ls] = yr
                g_scr[k1, rs, li] = yi
        return carry

    lax.fori_loop(0, rows_total // SUBLANES, body, 0)


def _fspec_kernel(h_ref, t_ref, w3f_ref, w3b_ref, dl_ref, wf_ref, tw_ref, out_ref, k_scr, g_scr,
                  *, L, n1, rows, group):
    dl = dl_ref[...]
    for d, w_ref in enumerate((w3f_ref, w3b_ref)):
        k = jnp.dot(w_ref[...].astype(BF16), h_ref[d].astype(BF16), preferred_element_type=F32)
        k_scr[:, d * L:(d + 1) * L] = k * jnp.exp(-(dl * t_ref[d]))

    def load(i, rs, ls):
        return (k_scr[rs, i * DFT_LANE + ls.start:i * DFT_LANE + ls.stop], None)

    _stage_a_forward(load, g_scr, tw_ref, n1, rows)
    inv_n = 1.0 / (n1 * DFT_LANE)
    w = wf_ref[...]
    for g0 in range(0, n1, group):
        lhs = g_scr[g0:g0 + group].reshape(group * rows, 2 * DFT_LANE).astype(BF16)
        res = jnp.dot(lhs, w, preferred_element_type=F32)
        out_ref[0, g0:g0 + group] = res.reshape(group, rows, 2 * DFT_LANE) * inv_n


def _fspec(h2t, t2, w3f, w3b, dl, wf, tw, L, n1, rows, group):
    n_rows = w3f.shape[0]
    tiles_per_order = HY_WIDTH // rows
    row_spec = lambda shape: pl.BlockSpec(shape, lambda i: (i, 0))
    return pl.pallas_call(
        functools.partial(_fspec_kernel, L=L, n1=n1, rows=rows, group=group),
        grid=(n_rows // rows,),
        in_specs=[_resident(h2t.shape), _resident(t2.shape),
                  row_spec((rows, FILTER_HIDDEN)), row_spec((rows, FILTER_HIDDEN)), row_spec((rows, 1)),
                  _resident(wf.shape), _resident(tw.shape)],
        out_specs=pl.BlockSpec((1, n1, rows, 2 * DFT_LANE),
                               lambda i: (i // tiles_per_order, 0, i % tiles_per_order, 0)),
        out_shape=jax.ShapeDtypeStruct((n_rows // HY_WIDTH, n1, HY_WIDTH, 2 * DFT_LANE), F32),
        scratch_shapes=[pltpu.VMEM((rows, 2 * L), F32), pltpu.VMEM((n1, rows, 2 * DFT_LANE), F32)],
        compiler_params=_cparams(("parallel",)),
        name="fspec",
    )(h2t, t2, w3f, w3b, dl, wf, tw)


def _hyena_kernel(hy_ref, wsh_ref, bsh_ref, ksp_ref, bias_ref, wf_ref, wi_ref, tw_ref, out_ref,
                  u_scr, x1_scr, x2_scr, g_scr, *, L, n1, ct, group):
    lane = lax.broadcasted_iota(jnp.int32, (ct, LANES), 1)

    def short_conv(g, dst):
        for bi in range(2):
            p = hy_ref[bi, g].astype(F32)
            left = pltpu.roll(p, 1, 1)
            left = jnp.concatenate([jnp.where(lane == 0, 0.0, left[:, :LANES]), left[:, LANES:]], axis=1)
            right = pltpu.roll(p, L - 1, 1)
            right = jnp.concatenate([right[:, :L - LANES], jnp.where(lane == LANES - 1, 0.0, right[:, L - LANES:])],
                                    axis=1)
            conv = wsh_ref[g, 0] * left + wsh_ref[g, 1] * p + wsh_ref[g, 2] * right + bsh_ref[g]
            if g == 2:
                z = hy_ref[bi, 3].astype(F32)
                conv = conv * (z * jax.nn.sigmoid(z))
            dst[bi] = conv

    short_conv(0, u_scr)
    gate_scr = (x1_scr, x2_scr)

    half_blocks = n1 // 2
    wf = wf_ref[...]
    wi = wi_ref[...]
    re, im = slice(0, DFT_LANE), slice(DFT_LANE, 2 * DFT_LANE)

    def mxu_dft(x, w):
        res = jnp.dot(x.reshape(group * ct, 2 * DFT_LANE).astype(BF16), w, preferred_element_type=F32)
        res = res.reshape(group, ct, 2 * DFT_LANE)
        return res[..., re], res[..., im]

    for order in range(2):
        def load(i, rs, ls):
            if i >= half_blocks:
                return None
            cs = slice(i * DFT_LANE + ls.start, i * DFT_LANE + ls.stop)
            return (u_scr[0, rs, cs], u_scr[1, rs, cs])

        _stage_a_forward(load, g_scr, None, n1, ct)

        def twiddled(gs):
            g = g_scr[gs]
            gr, gi = g[..., re], g[..., im]
            twr = tw_ref[gs, 0, 0:1, :]
            twi = tw_ref[gs, 1, 0:1, :]
            return jnp.concatenate([gr * twr - gi * twi, gr * twi + gi * twr], axis=-1)

        def untwiddle(gs, q):
            qr, qi = q
            twr = tw_ref[gs, 0, 0:1, :]
            twi = tw_ref[gs, 1, 0:1, :]
            g_scr[gs, :, re] = qr * twr + qi * twi
            g_scr[gs, :, im] = qi * twr - qr * twi

        groups = [slice(g0, g0 + group) for g0 in range(0, n1, group)]
        spectra = [mxu_dft(twiddled(gs), wf) for gs in groups]
        short_conv(order + 1, gate_scr[order])
        prev = None
        for gs, (hr, hi) in zip(groups, spectra):
            kr, ki = ksp_ref[order, gs, :, re], ksp_ref[order, gs, :, im]
            q = mxu_dft(jnp.concatenate([hr * kr - hi * ki, hr * ki + hi * kr], axis=-1), wi)
            if prev is not None:
                untwiddle(*prev)
            prev = (gs, q)
        untwiddle(*prev)

        def body(r, carry, order=order):
            rs = pl.ds(pl.multiple_of(r * SUBLANES, SUBLANES), SUBLANES)
            bias = bias_ref[order, rs, :]
            for half in range(DFT_LANE // LANES):
                ls = slice(half * LANES, (half + 1) * LANES)
                li = slice(DFT_LANE + half * LANES, DFT_LANE + (half + 1) * LANES)
                q = [(g_scr[k1, rs, ls], g_scr[k1, rs, li]) for k1 in range(n1)]
                ev = _fft_dif(q[0::2], 1.0)
                od = _fft_dif(q[1::2], 1.0)
                for i in range(half_blocks):
                    y = _cadd(ev[i], _cmulc(od[i], np.exp(2j * np.pi * i / n1)))
                    cs = slice(i * DFT_LANE + ls.start, i * DFT_LANE + ls.stop)
                    for bi in range(2):
                        u = u_scr[bi, rs, cs]
                        conv = y[bi] + bias * u
                        if order == 0:
                            u_scr[bi, rs, cs] = x1_scr[bi, rs, cs] * conv
                        else:
                            out_ref[bi, rs, cs] = x2_scr[bi, rs, cs] * conv
            return carry

        lax.fori_loop(0, ct // SUBLANES, body, 0)


def _hyena(hyt4, wsh, bsh, kspec, bias, wf, wi, tw, n1, ct, group):
    B, _, C, L = hyt4.shape
    return pl.pallas_call(
        functools.partial(_hyena_kernel, L=L, n1=n1, ct=ct, group=group),
        grid=(C // ct, B // 2),
        in_specs=[
            pl.BlockSpec((2, 4, ct, L), lambda c, p: (p, 0, c, 0)),
            pl.BlockSpec((3, 3, ct, 1), lambda c, p: (0, 0, c, 0)),
            pl.BlockSpec((3, ct, 1), lambda c, p: (0, c, 0)),
            pl.BlockSpec((2, n1, ct, 2 * DFT_LANE), lambda c, p: (0, 0, c, 0)),
            pl.BlockSpec((2, ct, 1), lambda c, p: (0, c, 0)),
            _resident(wf.shape), _resident(wi.shape), _resident(tw.shape),
        ],
        out_specs=pl.BlockSpec((2, ct, L), lambda c, p: (p, c, 0)),
        out_shape=jax.ShapeDtypeStruct((B, C, L), F32),
        scratch_shapes=[pltpu.VMEM((2, ct, L), F32), pltpu.VMEM((2, ct, L), F32), pltpu.VMEM((2, ct, L), F32),
                        pltpu.VMEM((n1, ct, 2 * DFT_LANE), F32)],
        compiler_params=_cparams(("parallel", "parallel")),
        name="hyena",
    )(hyt4, wsh, bsh, kspec, bias, wf, wi, tw)


def _attn_kernel(qt_ref, k_ref, vt_ref, o_ref, *, L, tq):
    units = [(j, t0) for t0 in range(0, tq, Q_TILE) for j in range(2)]
    starts = list(range(0, L, KV_CHUNK))

    def scores(u, c0):
        j, t0 = u
        return jnp.dot(k_ref[0, j, c0:c0 + KV_CHUNK, :], qt_ref[0, j, :, t0:t0 + Q_TILE],
                       preferred_element_type=F32)

    m = {u: None for u in units}
    acc = {u: None for u in units}
    cur = {u: scores(u, starts[0]) for u in units}
    for i, c0 in enumerate(starts):
        nxt = {}
        for u in units:
            if i + 1 < len(starts):
                nxt[u] = scores(u, starts[i + 1])
            s = cur[u]
            mc = jnp.max(s, axis=0, keepdims=True)
            m_new = mc if m[u] is None else jnp.maximum(m[u], mc)
            p = jnp.exp2(s - m_new).astype(BF16)
            pv = jnp.dot(vt_ref[0, u[0], :, c0:c0 + KV_CHUNK], p, preferred_element_type=F32)
            acc[u] = pv if m[u] is None else jnp.exp2(m[u] - m_new) * acc[u] + pv
            m[u] = m_new
        cur = nxt
    for j, t0 in units:
        a = acc[(j, t0)]
        o_ref[0, j * V_DIM:(j + 1) * V_DIM, t0:t0 + Q_TILE] = (a[:V_DIM] / a[V_DIM:V_DIM + 1]).astype(BF16)


def _attn(qt, k, vt, tq):
    B, _, L, _ = k.shape
    return pl.pallas_call(
        functools.partial(_attn_kernel, L=L, tq=tq),
        grid=(B, HEAD_PAIRS, L // tq),
        in_specs=[pl.BlockSpec((1, 2, QK_DIM, tq), lambda b, p, i: (b, p, 0, i)),
                  pl.BlockSpec((1, 2, L, QK_DIM), lambda b, p, i: (b, p, 0, 0)),
                  pl.BlockSpec((1, 2, V_ROWS, L), lambda b, p, i: (b, p, 0, 0))],
        out_specs=pl.BlockSpec((1, 2 * V_DIM, tq), lambda b, p, i: (b, p, i)),
        out_shape=jax.ShapeDtypeStruct((B, ATTN_WIDTH, L), BF16),
        compiler_params=_cparams(("parallel", "parallel", "parallel")),
        name="attn",
    )(qt, k, vt)


def _final_kernel(x_ref, yt_ref, at_ref, zat_ref, gn_ref, wg_ref, bg_ref, why_ref, wat_ref, wout_ref, o_ref,
                  *, tl, sub):
    for r0 in range(0, tl, sub):
        rows = slice(r0, r0 + sub)
        x = x_ref[0, rows, :]
        h = (x * lax.rsqrt(jnp.mean(x * x, axis=-1, keepdims=True) + EPS) * gn_ref[...]).astype(BF16)
        gates = jax.nn.sigmoid(jnp.dot(h, wg_ref[...], preferred_element_type=F32) + bg_ref[...])
        y = yt_ref[0, :, rows].T.astype(BF16)
        u_hy = jnp.dot(y, why_ref[...], preferred_element_type=F32)
        z = zat_ref[0, rows, :].astype(F32)
        a = (at_ref[0, :, rows].astype(F32).T * (z * jax.nn.sigmoid(z))).astype(BF16)
        u_at = jnp.dot(a, wat_ref[...], preferred_element_type=F32)
        merged = gates[:, :D_MODEL] * u_hy + gates[:, D_MODEL:] * u_at
        o_ref[0, rows, :] = x + jnp.dot(merged.astype(BF16), wout_ref[...], preferred_element_type=F32)


def _final(x, yt, at, zat, gn, wg, bg, why, wat, wout, tl):
    B, L, _ = x.shape
    tok = lambda w: pl.BlockSpec((1, tl, w), lambda b, i: (b, i, 0))
    consts = (gn, wg, bg, why, wat, wout)
    return pl.pallas_call(
        functools.partial(_final_kernel, tl=tl, sub=min(256, tl)),
        grid=(B, L // tl),
        in_specs=[tok(D_MODEL), pl.BlockSpec((1, HY_WIDTH, tl), lambda b, i: (b, 0, i)),
                  pl.BlockSpec((1, ATTN_WIDTH, tl), lambda b, i: (b, 0, i)), tok(ATTN_WIDTH)]
        + [_resident(a.shape) for a in consts],
        out_specs=tok(D_MODEL),
        out_shape=jax.ShapeDtypeStruct((B, L, D_MODEL), F32),
        compiler_params=_cparams(("parallel", "parallel")),
        name="final",
    )(x, yt, at, zat, *consts)


def _layer(x, g_norm, w_in, b_gate, w_short, b_short, w_f1, b_f1, freq_1, w_f2, b_f2, freq_2, w_f3,
           hy_bias, w_hy_out, g_cq, w_uq, g_ckv, w_ukv, g_qn, g_kn, w_attn_out, w_out):
    B, L, _ = x.shape
    n = 2 * L
    n1 = n // DFT_LANE
    assert B % 2 == 0 and n1 * DFT_LANE == n and n1 >= 2 and (n1 & (n1 - 1)) == 0
    tl = min(1024, L)
    ct = 64
    group = min(8, n1)

    o_z, o_cq, o_ckv, o_kr, o_za = 3 * HY_WIDTH, 4 * HY_WIDTH, 4 * HY_WIDTH + Q_LORA, 4 * HY_WIDTH + Q_LORA + KV_LORA, \
        4 * HY_WIDTH + Q_LORA + KV_LORA + QK_ROPE
    del o_z
    w_in = w_in.astype(BF16)
    wt_hy = w_in[:, :4 * HY_WIDTH].T
    half = QK_ROPE // 2
    w_cq = jnp.concatenate([w_in[:, o_cq:o_ckv], jnp.zeros((D_MODEL, QK_NOPE), BF16), w_in[:, o_kr:o_za],
                            jnp.zeros((D_MODEL, LANES - QK_DIM), BF16)], axis=1)
    w_ckv = w_in[:, o_ckv:o_kr]
    w_za = w_in[:, o_za:o_za + ATTN_WIDTH]
    w_gate = w_in[:, o_za + ATTN_WIDTH:]

    perm = np.concatenate([np.arange(QK_NOPE), QK_NOPE + half + np.arange(half), QK_NOPE + np.arange(half)])
    wqt = w_uq.T.astype(BF16)
    wkv = w_ukv.reshape(KV_LORA, MLA_HEADS, QK_NOPE + V_DIM)
    wk = jnp.pad(wkv[:, :, :QK_NOPE], ((0, 0), (0, 0), (0, LANES - QK_NOPE)))
    wk = wk.reshape(KV_LORA, MLA_HEADS * LANES).astype(BF16)
    wvt = jnp.pad(wkv[:, :, QK_NOPE:], ((0, 0), (0, 0), (0, V_ROWS - V_DIM)))
    wvt = wvt.reshape(KV_LORA, MLA_HEADS * V_ROWS).T.astype(BF16)
    pad_lanes = lambda g: jnp.pad(g, ((0, 0), (0, LANES - QK_DIM)))
    gqc = g_qn.reshape(QK_DIM, 1)
    gqsc = gqc[perm]
    gk = g_kn.reshape(1, QK_DIM)
    gks = pad_lanes(gk[:, perm])
    gk = pad_lanes(gk)

    ctab, stab = _rope_tables(L)
    pad_tab = lambda tab: np.pad(tab, ((0, 0), (0, LANES - QK_DIM)))
    hyt, zat, qt, k, vt = _in_proj(
        x, g_norm.reshape(1, D_MODEL), wt_hy, w_cq, w_ckv, w_za, g_cq.reshape(1, Q_LORA),
        g_ckv.reshape(1, KV_LORA), jnp.asarray(pad_tab(ctab)), jnp.asarray(pad_tab(stab)),
        jnp.asarray(np.ascontiguousarray(ctab.T)), jnp.asarray(np.ascontiguousarray(stab.T)),
        wqt, wk, wvt, gqc, gqsc, gk, gks, tl)

    zfeat, t = _filter_features(L)
    w1 = jnp.concatenate([w_f1, jnp.zeros((LANES - FILTER_EMB, FILTER_HIDDEN), F32)], axis=0)
    w3 = w_f3.reshape(FILTER_HIDDEN, 2, 2, HY_WIDTH)
    w3f = w3[:, :, 0].reshape(FILTER_HIDDEN, 2 * HY_WIDTH).T
    w3b = w3[:, :, 1].reshape(FILTER_HIDDEN, 2 * HY_WIDTH).T
    deltas = np.abs(np.linspace(MIN_DECAY, MAX_DECAY, HY_WIDTH))
    dl = np.tile(deltas, 2).reshape(2 * HY_WIDTH, 1).astype(np.float32)
    col = lambda v: v.reshape(FILTER_HIDDEN, 1)
    zt = np.ascontiguousarray(np.stack([zfeat.T, zfeat[::-1].T]))
    t2 = np.ascontiguousarray(np.stack([t, t[::-1]]).reshape(2, 1, L))
    h2t = _fmlp(jnp.asarray(zt), w1.T, col(b_f1), col(freq_1), w_f2.T, col(b_f2), col(freq_2))
    wf, wi = (jnp.asarray(w).astype(BF16) for w in _dft_mats())
    tw = jnp.asarray(_twiddle(n1))
    kspec = _fspec(h2t, jnp.asarray(t2), w3f, w3b, jnp.asarray(dl), wf, tw, L, n1, 2 * ct, group)

    wsh = w_short.reshape(3, 3, HY_WIDTH).transpose(1, 0, 2)[..., None]
    bsh = b_short.reshape(3, HY_WIDTH, 1)
    yt = _hyena(hyt.reshape(B, 4, HY_WIDTH, L), wsh, bsh, kspec, hy_bias.reshape(2, HY_WIDTH, 1),
                wf, wi, tw, n1, ct, group)

    at = _attn(qt, k, vt, min(2048, L))

    return _final(x, yt, at, zat, g_norm.reshape(1, D_MODEL), w_gate, b_gate.reshape(1, 2 * D_MODEL),
                  w_hy_out.astype(BF16), w_attn_out.astype(BF16), w_out.astype(BF16), min(1024, L))


def kernel(x, g_norm, w_in, b_gate, w_short, b_short, w_f1, b_f1, freq_1, w_f2, b_f2, freq_2, w_f3, hy_bias, w_hy_out, g_cq, w_uq, g_ckv, w_ukv, g_qn, g_kn, w_attn_out, w_out):
    depth = g_norm.shape[0]
    for l in range(depth):
        x = _layer(x, g_norm[l], w_in[l], b_gate[l], w_short[l], b_short[l], w_f1[l], b_f1[l], freq_1[l], w_f2[l],
                   b_f2[l], freq_2[l], w_f3[l], hy_bias[l], w_hy_out[l], g_cq[l], w_uq[l], g_ckv[l], w_ukv[l],
                   g_qn[l], g_kn[l], w_attn_out[l], w_out[l])
    return x
```

```python
import functools
import math

import numpy as np
import jax
import jax.numpy as jnp
from jax import lax
from jax.experimental import pallas as pl
from jax.experimental.pallas import tpu as pltpu

D_MODEL = 1024
HY_WIDTH = 512
FILTER_EMB = 33
FILTER_HIDDEN = 64
DECAY_TARGET = 1e-2
FAST_DECAY = 0.3
SLOW_DECAY = 1.5
MIN_DECAY = math.log(DECAY_TARGET) / SLOW_DECAY
MAX_DECAY = math.log(DECAY_TARGET) / FAST_DECAY
MLA_HEADS = 8
HEAD_PAIRS = MLA_HEADS // 2
QK_NOPE = 64
QK_ROPE = 32
QK_DIM = QK_NOPE + QK_ROPE
V_DIM = 64
Q_LORA = 384
KV_LORA = 256
ATTN_WIDTH = MLA_HEADS * V_DIM
ROPE_THETA = 10000.0
EPS = 1e-6

LANES = 128
SUBLANES = 8
DFT_LANE = 256
V_ROWS = 80
KV_CHUNK = 256
Q_TILE = 256
VMEM_LIMIT = 56 * 1024 * 1024

F32 = jnp.float32
BF16 = jnp.bfloat16
_NT = (((1,), (1,)), ((), ()))


def _cparams(sem):
    return pltpu.CompilerParams(dimension_semantics=sem, vmem_limit_bytes=VMEM_LIMIT)


def _resident(shape):
    nd = len(shape)
    return pl.BlockSpec(shape, lambda *_: (0,) * nd, pipeline_mode=pl.Buffered(1))


def _add(a, b):
    if a is None:
        return b
    if b is None:
        return a
    return a + b


def _sub(a, b):
    if b is None:
        return a
    if a is None:
        return -b
    return a - b


def _cadd(x, y):
    return (_add(x[0], y[0]), _add(x[1], y[1]))


def _csub(x, y):
    return (_sub(x[0], y[0]), _sub(x[1], y[1]))


def _scale(a, s):
    if a is None or s == 0.0:
        return None
    if s == 1.0:
        return a
    if s == -1.0:
        return -a
    return a * s


def _signed_sum(terms):
    pos = [a for sg, a in terms if a is not None and sg > 0]
    neg = [a for sg, a in terms if a is not None and sg < 0]
    if pos:
        r = pos[0]
        for a in pos[1:]:
            r = r + a
        for a in neg:
            r = r - a
        return r, 1.0
    if neg:
        r = neg[0]
        for a in neg[1:]:
            r = r + a
        return r, -1.0
    return None, 1.0


def _cmulc(x, w):
    wr, wi = float(np.real(w)), float(np.imag(w))
    if abs(wr) < 1e-15:
        wr = 0.0
    if abs(wi) < 1e-15:
        wi = 0.0
    xr, xi = x
    if wr != 0.0 and abs(abs(wr) - abs(wi)) < 1e-12:
        sr, si, c = np.sign(wr), np.sign(wi), abs(wr)
        re, sre = _signed_sum([(sr, xr), (-si, xi)])
        im, sim = _signed_sum([(si, xr), (sr, xi)])
        return (_scale(re, sre * c), _scale(im, sim * c))
    re = _sub(_scale(xr, wr), _scale(xi, wi))
    im = _add(_scale(xr, wi), _scale(xi, wr))
    return (re, im)


def _cdiff_mulc(x, y, w):
    wr, wi = float(np.real(w)), float(np.imag(w))
    (xr, xi), (yr, yi) = x, y
    unit = {(1, 0): ((1, xr, yr), (1, xi, yi)), (-1, 0): ((-1, xr, yr), (-1, xi, yi)),
            (0, 1): ((-1, xi, yi), (1, xr, yr)), (0, -1): ((1, xi, yi), (-1, xr, yr))}
    key = (int(round(wr)), int(round(wi)))
    if abs(wr - key[0]) < 1e-12 and abs(wi - key[1]) < 1e-12 and key in unit:
        parts = []
        for sg, a, b in unit[key]:
            v, sv = _signed_sum([(sg, a), (-sg, b)])
            parts.append(_scale(v, sv))
        return tuple(parts)
    return _cmulc(_csub(x, y), w)


def _cmul(x, y):
    xr, xi = x
    yr, yi = y
    re = _sub(None if xr is None else xr * yr, None if xi is None else xi * yi)
    im = _add(None if xr is None else xr * yi, None if xi is None else xi * yr)
    return (re, im)


def _fft_dif(xs, sign):
    n = len(xs)
    if n == 1:
        return xs
    half = n // 2
    a = [_cadd(xs[i], xs[i + half]) for i in range(half)]
    b = [_cdiff_mulc(xs[i], xs[i + half], np.exp(sign * 2j * np.pi * i / n)) for i in range(half)]
    ev = _fft_dif(a, sign)
    od = _fft_dif(b, sign)
    out = [None] * n
    out[0::2] = ev
    out[1::2] = od
    return out


def _dft_mats():
    n = np.arange(DFT_LANE)
    ang = 2.0 * np.pi * np.outer(n, n) / DFT_LANE
    fr, fi = np.cos(ang), -np.sin(ang)
    fwd = np.block([[fr, fi], [-fi, fr]]).astype(np.float32)
    inv = np.block([[fr, -fi], [fi, fr]]).astype(np.float32)
    return fwd, inv


def _twiddle(n1):
    n = n1 * DFT_LANE
    ang = 2.0 * np.pi * np.outer(np.arange(n1), np.arange(DFT_LANE)) / n
    tw = np.stack([np.cos(ang), -np.sin(ang)], axis=1)
    return np.ascontiguousarray(np.broadcast_to(tw[:, :, None, :], (n1, 2, SUBLANES, DFT_LANE))).astype(np.float32)


def _filter_features(L):
    t = np.linspace(0.0, 1.0, L)[:, None]
    bands = (FILTER_EMB - 1) // 2
    f = np.linspace(1e-4, bands - 1, bands)
    ang = (2.0 * np.pi / L) * np.arange(L)[:, None] * f[None, :]
    z = np.concatenate([t, np.cos(ang), -np.sin(ang)], axis=-1)
    zp = np.zeros((L, LANES), np.float32)
    zp[:, :FILTER_EMB] = z
    return zp, t[:, 0].astype(np.float32)


def _rope_tables(L):
    pos = np.arange(L, dtype=np.float64)
    inv_freq = ROPE_THETA ** (-np.arange(0, QK_ROPE, 2, dtype=np.float64) / QK_ROPE)
    ang = pos[:, None] * inv_freq[None, :]
    c, s = np.cos(ang), np.sin(ang)
    ctab = np.concatenate([np.ones((L, QK_NOPE)), c, c], axis=1)
    stab = np.concatenate([np.zeros((L, QK_NOPE)), -s, s], axis=1)
    return ctab.astype(np.float32), stab.astype(np.float32)


def _in_proj_kernel(x_ref, gn_ref, wt_hy_ref, w_cq_ref, w_ckv_ref, w_za_ref, gcq_ref, gckv_ref,
                    ct_ref, st_ref, ctt_ref, stt_ref, wqt_ref, wk_ref, wvt_ref,
                    gqc_ref, gqsc_ref, gk_ref, gks_ref, ones_ref,
                    hyt_ref, zat_ref, qt_ref, k_ref, vt_ref, *, tl, sub):
    half = QK_ROPE // 2
    scale = math.log2(math.e) / math.sqrt(QK_DIM)

    def latent(c, g_ref):
        r = lax.rsqrt(jnp.mean(c * c, axis=-1, keepdims=True) + EPS)
        return (c * r * g_ref[...]).astype(BF16)

    for r0 in range(0, tl, sub):
        rows = slice(r0, r0 + sub)
        x = x_ref[0, rows, :]
        ms = jnp.mean(x * x, axis=-1, keepdims=True)
        h = (x * lax.rsqrt(ms + EPS) * gn_ref[...]).astype(BF16)

        cqr = jnp.dot(h, w_cq_ref[...], preferred_element_type=F32)
        cq = latent(cqr[:, :Q_LORA], gcq_ref)
        ckv = latent(jnp.dot(h, w_ckv_ref[...], preferred_element_type=F32), gckv_ref)
        kr_a = cqr[:, Q_LORA:]
        qt_all = lax.dot_general(wqt_ref[...], cq, _NT, preferred_element_type=F32)
        ka_all = jnp.dot(ckv, wk_ref[...], preferred_element_type=F32)
        vt_all = lax.dot_general(wvt_ref[...], ckv, _NT, preferred_element_type=F32) + ones_ref[...]

        hyt_ref[0, :, rows] = lax.dot_general(wt_hy_ref[...], h, _NT, preferred_element_type=F32).astype(BF16)
        zat_ref[0, rows, :] = jnp.dot(h, w_za_ref[...], preferred_element_type=F32).astype(BF16)

        qc, qs = gqc_ref[...] * ctt_ref[:, rows], gqsc_ref[...] * stt_ref[:, rows]
        kc, ks = gk_ref[...] * ct_ref[rows, :], gks_ref[...] * st_ref[rows, :]
        lane = lax.broadcasted_iota(jnp.int32, kr_a.shape, 1)
        kr_w = jnp.where(lane < QK_NOPE + half, pltpu.roll(kr_a, LANES - half, 1), pltpu.roll(kr_a, half, 1))
        for hd in range(MLA_HEADS):
            qa = qt_all[hd * QK_DIM:(hd + 1) * QK_DIM]
            qw = jnp.concatenate([qa[:QK_NOPE], qa[QK_NOPE + half:], qa[QK_NOPE:QK_NOPE + half]], axis=0)
            r = lax.rsqrt(jnp.mean(qa * qa, axis=0, keepdims=True) + EPS) * scale
            qt_ref[0, hd, :, rows] = ((qa * qc + qw * qs) * r).astype(BF16)
            ka = ka_all[:, hd * LANES:(hd + 1) * LANES] + kr_a
            rk = lax.rsqrt(jnp.sum(ka * ka, axis=-1, keepdims=True) * (1.0 / QK_DIM) + EPS)
            k_ref[0, hd, rows, :] = ((ka * kc + kr_w * ks) * rk)[:, :QK_DIM].astype(BF16)
            vt_ref[0, hd, :, rows] = vt_all[hd * V_ROWS:(hd + 1) * V_ROWS].astype(BF16)


def _in_proj(x, gn, wt_hy, w_cq, w_ckv, w_za, gcq, gckv, ctab, stab, ctab_t, stab_t,
             wqt, wk, wvt, gqc, gqsc, gk, gks, tl):
    ones = np.zeros((MLA_HEADS, V_ROWS, 1), np.float32)
    ones[:, V_DIM, 0] = 1.0
    ones = jnp.asarray(ones.reshape(MLA_HEADS * V_ROWS, 1))
    B, L, _ = x.shape
    tab = pl.BlockSpec((tl, LANES), lambda b, i: (i, 0))
    tab_t = pl.BlockSpec((QK_DIM, tl), lambda b, i: (0, i))
    consts = (gn, wt_hy, w_cq, w_ckv, w_za, gcq, gckv)
    qkv_consts = (wqt, wk, wvt, gqc, gqsc, gk, gks, ones)
    return pl.pallas_call(
        functools.partial(_in_proj_kernel, tl=tl, sub=min(512, tl)),
        grid=(B, L // tl),
        in_specs=[pl.BlockSpec((1, tl, D_MODEL), lambda b, i: (b, i, 0))]
        + [_resident(a.shape) for a in consts] + [tab, tab, tab_t, tab_t] + [_resident(a.shape) for a in qkv_consts],
        out_specs=[
            pl.BlockSpec((1, 4 * HY_WIDTH, tl), lambda b, i: (b, 0, i)),
            pl.BlockSpec((1, tl, ATTN_WIDTH), lambda b, i: (b, i, 0)),
            pl.BlockSpec((1, MLA_HEADS, QK_DIM, tl), lambda b, i: (b, 0, 0, i)),
            pl.BlockSpec((1, MLA_HEADS, tl, QK_DIM), lambda b, i: (b, 0, i, 0)),
            pl.BlockSpec((1, MLA_HEADS, V_ROWS, tl), lambda b, i: (b, 0, 0, i)),
        ],
        out_shape=[
            jax.ShapeDtypeStruct((B, 4 * HY_WIDTH, L), BF16),
            jax.ShapeDtypeStruct((B, L, ATTN_WIDTH), BF16),
            jax.ShapeDtypeStruct((B, MLA_HEADS, QK_DIM, L), BF16),
            jax.ShapeDtypeStruct((B, MLA_HEADS, L, QK_DIM), BF16),
            jax.ShapeDtypeStruct((B, MLA_HEADS, V_ROWS, L), BF16),
        ],
        compiler_params=_cparams(("parallel", "parallel")),
        name="in_proj",
    )(x, *consts, ctab, stab, ctab_t, stab_t, *qkv_consts)


def _fmlp_kernel(zt_ref, w1t_ref, b1_ref, f1_ref, w2t_ref, b2_ref, f2_ref, out_ref):
    hp = lax.Precision.HIGHEST
    h = jnp.sin(f1_ref[...] * (jnp.dot(w1t_ref[...], zt_ref[0], precision=hp, preferred_element_type=F32) + b1_ref[...]))
    out_ref[0] = jnp.sin(f2_ref[...] * (jnp.dot(w2t_ref[...], h, precision=hp, preferred_element_type=F32) + b2_ref[...]))


def _fmlp(zt, w1t, b1, f1, w2t, b2, f2):
    _, _, L = zt.shape
    return pl.pallas_call(
        _fmlp_kernel,
        grid=(2,),
        in_specs=[pl.BlockSpec((1, LANES, L), lambda d: (d, 0, 0)), _resident(w1t.shape), _resident(b1.shape),
                  _resident(f1.shape), _resident(w2t.shape), _resident(b2.shape), _resident(f2.shape)],
        out_specs=pl.BlockSpec((1, FILTER_HIDDEN, L), lambda d: (d, 0, 0)),
        out_shape=jax.ShapeDtypeStruct((2, FILTER_HIDDEN, L), F32),
        compiler_params=_cparams(("parallel",)),
        name="fmlp",
    )(zt, w1t, b1, f1, w2t, b2, f2)


def _stage_a_forward(load, g_scr, tw_ref, n1, rows_total):
    def body(r, carry):
        rs = pl.ds(pl.multiple_of(r * SUBLANES, SUBLANES), SUBLANES)
        for half in range(DFT_LANE // LANES):
            ls = slice(half * LANES, (half + 1) * LANES)
            li = slice(DFT_LANE + half * LANES, DFT_LANE + (half + 1) * LANES)
            xs = [load(i, rs, ls) for i in range(n1)]
            xs = [(None, None) if x is None else x for x in xs]
            X = _fft_dif(xs, -1.0)
            for k1 in range(n1):
                if tw_ref is None:
                    yr, yi = X[k1]
                else:
                    yr, yi = _cmul(X[k1], (tw_ref[k1, 0, :, ls], tw_ref[k1, 1, :, ls]))
                g_scr[k1, rs, ls] = yr
                g_scr[k1, rs, li] = yi
        return carry

    lax.fori_loop(0, rows_total // SUBLANES, body, 0)


def _fspec_kernel(h_ref, t_ref, w3f_ref, w3b_ref, dl_ref, wf_ref, tw_ref, out_ref, k_scr, g_scr,
                  *, L, n1, rows, group):
    dl = dl_ref[...]
    for d, w_ref in enumerate((w3f_ref, w3b_ref)):
        k = jnp.dot(w_ref[...].astype(BF16), h_ref[d].astype(BF16), preferred_element_type=F32)
        k_scr[:, d * L:(d + 1) * L] = k * jnp.exp(-(dl * t_ref[d]))

    def load(i, rs, ls):
        return (k_scr[rs, i * DFT_LANE + ls.start:i * DFT_LANE + ls.stop], None)

    _stage_a_forward(load, g_scr, tw_ref, n1, rows)
    inv_n = 1.0 / (n1 * DFT_LANE)
    w = wf_ref[...]
    for g0 in range(0, n1, group):
        lhs = g_scr[g0:g0 + group].reshape(group * rows, 2 * DFT_LANE).astype(BF16)
        res = jnp.dot(lhs, w, preferred_element_type=F32)
        out_ref[0, g0:g0 + group] = res.reshape(group, rows, 2 * DFT_LANE) * inv_n


def _fspec(h2t, t2, w3f, w3b, dl, wf, tw, L, n1, rows, group):
    n_rows = w3f.shape[0]
    tiles_per_order = HY_WIDTH // rows
    row_spec = lambda shape: pl.BlockSpec(shape, lambda i: (i, 0))
    return pl.pallas_call(
        functools.partial(_fspec_kernel, L=L, n1=n1, rows=rows, group=group),
        grid=(n_rows // rows,),
        in_specs=[_resident(h2t.shape), _resident(t2.shape),
                  row_spec((rows, FILTER_HIDDEN)), row_spec((rows, FILTER_HIDDEN)), row_spec((rows, 1)),
                  _resident(wf.shape), _resident(tw.shape)],
        out_specs=pl.BlockSpec((1, n1, rows, 2 * DFT_LANE),
                               lambda i: (i // tiles_per_order, 0, i % tiles_per_order, 0)),
        out_shape=jax.ShapeDtypeStruct((n_rows // HY_WIDTH, n1, HY_WIDTH, 2 * DFT_LANE), F32),
        scratch_shapes=[pltpu.VMEM((rows, 2 * L), F32), pltpu.VMEM((n1, rows, 2 * DFT_LANE), F32)],
        compiler_params=_cparams(("parallel",)),
        name="fspec",
    )(h2t, t2, w3f, w3b, dl, wf, tw)


def _hyena_kernel(hy_ref, wsh_ref, bsh_ref, ksp_ref, bias_ref, wf_ref, wi_ref, tw_ref, out_ref,
                  u_scr, x1_scr, x2_scr, g_scr, *, L, n1, ct, group):
    lane = lax.broadcasted_iota(jnp.int32, (ct, LANES), 1)

    def short_conv(g, dst):
        for bi in range(2):
            p = hy_ref[bi, g].astype(F32)
            left = pltpu.roll(p, 1, 1)
            left = jnp.concatenate([jnp.where(lane == 0, 0.0, left[:, :LANES]), left[:, LANES:]], axis=1)
            right = pltpu.roll(p, L - 1, 1)
            right = jnp.concatenate([right[:, :L - LANES], jnp.where(lane == LANES - 1, 0.0, right[:, L - LANES:])],
                                    axis=1)
            conv = wsh_ref[g, 0] * left + wsh_ref[g, 1] * p + wsh_ref[g, 2] * right + bsh_ref[g]
            if g == 2:
                z = hy_ref[bi, 3].astype(F32)
                conv = conv * (z * jax.nn.sigmoid(z))
            dst[bi] = conv

    short_conv(0, u_scr)
    gate_scr = (x1_scr, x2_scr)

    half_blocks = n1 // 2
    wf = wf_ref[...]
    wi = wi_ref[...]
    re, im = slice(0, DFT_LANE), slice(DFT_LANE, 2 * DFT_LANE)

    def mxu_dft(x, w):
        res = jnp.dot(x.reshape(group * ct, 2 * DFT_LANE).astype(BF16), w, preferred_element_type=F32)
        res = res.reshape(group, ct, 2 * DFT_LANE)
        return res[..., re], res[..., im]

    for order in range(2):
        def load(i, rs, ls):
            if i >= half_blocks:
                return None
            cs = slice(i * DFT_LANE + ls.start, i * DFT_LANE + ls.stop)
            return (u_scr[0, rs, cs], u_scr[1, rs, cs])

        _stage_a_forward(load, g_scr, None, n1, ct)

        def twiddled(gs):
            g = g_scr[gs]
            gr, gi = g[..., re], g[..., im]
            twr = tw_ref[gs, 0, 0:1, :]
            twi = tw_ref[gs, 1, 0:1, :]
            return jnp.concatenate([gr * twr - gi * twi, gr * twi + gi * twr], axis=-1)

        def untwiddle(gs, q):
            qr, qi = q
            twr = tw_ref[gs, 0, 0:1, :]
            twi = tw_ref[gs, 1, 0:1, :]
            g_scr[gs, :, re] = qr * twr + qi * twi
            g_scr[gs, :, im] = qi * twr - qr * twi

        groups = [slice(g0, g0 + group) for g0 in range(0, n1, group)]
        spectra = [mxu_dft(twiddled(gs), wf) for gs in groups]
        short_conv(order + 1, gate_scr[order])
        prev = None
        for gs, (hr, hi) in zip(groups, spectra):
            kr, ki = ksp_ref[order, gs, :, re], ksp_ref[order, gs, :, im]
            q = mxu_dft(jnp.concatenate([hr * kr - hi * ki, hr * ki + hi * kr], axis=-1), wi)
            if prev is not None:
                untwiddle(*prev)
            prev = (gs, q)
        untwiddle(*prev)

        def body(r, carry, order=order):
            rs = pl.ds(pl.multiple_of(r * SUBLANES, SUBLANES), SUBLANES)
            bias = bias_ref[order, rs, :]
            for half in range(DFT_LANE // LANES):
                ls = slice(half * LANES, (half + 1) * LANES)
                li = slice(DFT_LANE + half * LANES, DFT_LANE + (half + 1) * LANES)
                q = [(g_scr[k1, rs, ls], g_scr[k1, rs, li]) for k1 in range(n1)]
                ev = _fft_dif(q[0::2], 1.0)
                od = _fft_dif(q[1::2], 1.0)
                for i in range(half_blocks):
                    y = _cadd(ev[i], _cmulc(od[i], np.exp(2j * np.pi * i / n1)))
                    cs = slice(i * DFT_LANE + ls.start, i * DFT_LANE + ls.stop)
                    for bi in range(2):
                        u = u_scr[bi, rs, cs]
                        conv = y[bi] + bias * u
                        if order == 0:
                            u_scr[bi, rs, cs] = x1_scr[bi, rs, cs] * conv
                        else:
                            out_ref[bi, rs, cs] = x2_scr[bi, rs, cs] * conv
            return carry

        lax.fori_loop(0, ct // SUBLANES, body, 0)


def _hyena(hyt4, wsh, bsh, kspec, bias, wf, wi, tw, n1, ct, group):
    B, _, C, L = hyt4.shape
    return pl.pallas_call(
        functools.partial(_hyena_kernel, L=L, n1=n1, ct=ct, group=group),
        grid=(C // ct, B // 2),
        in_specs=[
            pl.BlockSpec((2, 4, ct, L), lambda c, p: (p, 0, c, 0)),
            pl.BlockSpec((3, 3, ct, 1), lambda c, p: (0, 0, c, 0)),
            pl.BlockSpec((3, ct, 1), lambda c, p: (0, c, 0)),
            pl.BlockSpec((2, n1, ct, 2 * DFT_LANE), lambda c, p: (0, 0, c, 0)),
            pl.BlockSpec((2, ct, 1), lambda c, p: (0, c, 0)),
            _resident(wf.shape), _resident(wi.shape), _resident(tw.shape),
        ],
        out_specs=pl.BlockSpec((2, ct, L), lambda c, p: (p, c, 0)),
        out_shape=jax.ShapeDtypeStruct((B, C, L), F32),
        scratch_shapes=[pltpu.VMEM((2, ct, L), F32), pltpu.VMEM((2, ct, L), F32), pltpu.VMEM((2, ct, L), F32),
                        pltpu.VMEM((n1, ct, 2 * DFT_LANE), F32)],
        compiler_params=_cparams(("parallel", "parallel")),
        name="hyena",
    )(hyt4, wsh, bsh, kspec, bias, wf, wi, tw)


def _attn_kernel(qt_ref, k_ref, vt_ref, o_ref, *, L, tq):
    units = [(j, t0) for t0 in range(0, tq, Q_TILE) for j in range(2)]
    starts = list(range(0, L, KV_CHUNK))

    def scores(u, c0):
        j, t0 = u
        return jnp.dot(k_ref[0, j, c0:c0 + KV_CHUNK, :], qt_ref[0, j, :, t0:t0 + Q_TILE],
                       preferred_element_type=F32)

    m = {u: None for u in units}
    acc = {u: None for u in units}
    cur = {u: scores(u, starts[0]) for u in units}
    for i, c0 in enumerate(starts):
        nxt = {}
        for u in units:
            if i + 1 < len(starts):
                nxt[u] = scores(u, starts[i + 1])
            s = cur[u]
            mc = jnp.max(s, axis=0, keepdims=True)
            m_new = mc if m[u] is None else jnp.maximum(m[u], mc)
            p = jnp.exp2(s - m_new).astype(BF16)
            pv = jnp.dot(vt_ref[0, u[0], :, c0:c0 + KV_CHUNK], p, preferred_element_type=F32)
            acc[u] = pv if m[u] is None else jnp.exp2(m[u] - m_new) * acc[u] + pv
            m[u] = m_new
        cur = nxt
    for j, t0 in units:
        a = acc[(j, t0)]
        o_ref[0, j * V_DIM:(j + 1) * V_DIM, t0:t0 + Q_TILE] = (a[:V_DIM] / a[V_DIM:V_DIM + 1]).astype(BF16)


def _attn(qt, k, vt, tq):
    B, _, L, _ = k.shape
    return pl.pallas_call(
        functools.partial(_attn_kernel, L=L, tq=tq),
        grid=(B, HEAD_PAIRS, L // tq),
        in_specs=[pl.BlockSpec((1, 2, QK_DIM, tq), lambda b, p, i: (b, p, 0, i)),
                  pl.BlockSpec((1, 2, L, QK_DIM), lambda b, p, i: (b, p, 0, 0)),
                  pl.BlockSpec((1, 2, V_ROWS, L), lambda b, p, i: (b, p, 0, 0))],
        out_specs=pl.BlockSpec((1, 2 * V_DIM, tq), lambda b, p, i: (b, p, i)),
        out_shape=jax.ShapeDtypeStruct((B, ATTN_WIDTH, L), BF16),
        compiler_params=_cparams(("parallel", "parallel", "parallel")),
        name="attn",
    )(qt, k, vt)


def _final_kernel(x_ref, yt_ref, at_ref, zat_ref, gn_ref, wg_ref, bg_ref, why_ref, wat_ref, wout_ref, o_ref,
                  *, tl, sub):
    for r0 in range(0, tl, sub):
        rows = slice(r0, r0 + sub)
        x = x_ref[0, rows, :]
        h = (x * lax.rsqrt(jnp.mean(x * x, axis=-1, keepdims=True) + EPS) * gn_ref[...]).astype(BF16)
        gates = jax.nn.sigmoid(jnp.dot(h, wg_ref[...], preferred_element_type=F32) + bg_ref[...])
        y = yt_ref[0, :, rows].T.astype(BF16)
        u_hy = jnp.dot(y, why_ref[...], preferred_element_type=F32)
        z = zat_ref[0, rows, :].astype(F32)
        a = (at_ref[0, :, rows].astype(F32).T * (z * jax.nn.sigmoid(z))).astype(BF16)
        u_at = jnp.dot(a, wat_ref[...], preferred_element_type=F32)
        merged = gates[:, :D_MODEL] * u_hy + gates[:, D_MODEL:] * u_at
        o_ref[0, rows, :] = x + jnp.dot(merged.astype(BF16), wout_ref[...], preferred_element_type=F32)


def _final(x, yt, at, zat, gn, wg, bg, why, wat, wout, tl):
    B, L, _ = x.shape
    tok = lambda w: pl.BlockSpec((1, tl, w), lambda b, i: (b, i, 0))
    consts = (gn, wg, bg, why, wat, wout)
    return pl.pallas_call(
        functools.partial(_final_kernel, tl=tl, sub=min(512, tl)),
        grid=(B, L // tl),
        in_specs=[tok(D_MODEL), pl.BlockSpec((1, HY_WIDTH, tl), lambda b, i: (b, 0, i)),
                  pl.BlockSpec((1, ATTN_WIDTH, tl), lambda b, i: (b, 0, i)), tok(ATTN_WIDTH)]
        + [_resident(a.shape) for a in consts],
        out_specs=tok(D_MODEL),
        out_shape=jax.ShapeDtypeStruct((B, L, D_MODEL), F32),
        compiler_params=_cparams(("parallel", "parallel")),
        name="final",
    )(x, yt, at, zat, *consts)


def _wt_kernel(w_ref, o_ref):
    o_ref[...] = w_ref[...].T.astype(BF16)


def _transposed_columns(w, n_cols, tile):
    K = w.shape[0]
    return pl.pallas_call(
        _wt_kernel,
        grid=(n_cols // tile,),
        in_specs=[pl.BlockSpec((K, tile), lambda j: (0, j))],
        out_specs=pl.BlockSpec((tile, K), lambda j: (j, 0)),
        out_shape=jax.ShapeDtypeStruct((n_cols, K), BF16),
        compiler_params=_cparams(("parallel",)),
        name="wt",
    )(w)


def _layer(x, g_norm, w_in, b_gate, w_short, b_short, w_f1, b_f1, freq_1, w_f2, b_f2, freq_2, w_f3,
           hy_bias, w_hy_out, g_cq, w_uq, g_ckv, w_ukv, g_qn, g_kn, w_attn_out, w_out):
    B, L, _ = x.shape
    n = 2 * L
    n1 = n // DFT_LANE
    assert B % 2 == 0 and n1 * DFT_LANE == n and n1 >= 2 and (n1 & (n1 - 1)) == 0
    tl = min(1024, L)
    ct = 64
    group = min(8, n1)

    o_z, o_cq, o_ckv, o_kr, o_za = 3 * HY_WIDTH, 4 * HY_WIDTH, 4 * HY_WIDTH + Q_LORA, 4 * HY_WIDTH + Q_LORA + KV_LORA, \
        4 * HY_WIDTH + Q_LORA + KV_LORA + QK_ROPE
    del o_z
    wt_hy = _transposed_columns(w_in, 4 * HY_WIDTH, HY_WIDTH)
    w_in = w_in.astype(BF16)
    half = QK_ROPE // 2
    w_cq = jnp.concatenate([w_in[:, o_cq:o_ckv], jnp.zeros((D_MODEL, QK_NOPE), BF16), w_in[:, o_kr:o_za],
                            jnp.zeros((D_MODEL, LANES - QK_DIM), BF16)], axis=1)
    w_ckv = w_in[:, o_ckv:o_kr]
    w_za = w_in[:, o_za:o_za + ATTN_WIDTH]
    w_gate = w_in[:, o_za + ATTN_WIDTH:]

    perm = np.concatenate([np.arange(QK_NOPE), QK_NOPE + half + np.arange(half), QK_NOPE + np.arange(half)])
    wqt = w_uq.T.astype(BF16)
    wkv = w_ukv.reshape(KV_LORA, MLA_HEADS, QK_NOPE + V_DIM)
    wk = jnp.pad(wkv[:, :, :QK_NOPE], ((0, 0), (0, 0), (0, LANES - QK_NOPE)))
    wk = wk.reshape(KV_LORA, MLA_HEADS * LANES).astype(BF16)
    wvt = jnp.pad(wkv[:, :, QK_NOPE:], ((0, 0), (0, 0), (0, V_ROWS - V_DIM)))
    wvt = wvt.reshape(KV_LORA, MLA_HEADS * V_ROWS).T.astype(BF16)
    pad_lanes = lambda g: jnp.pad(g, ((0, 0), (0, LANES - QK_DIM)))
    gqc = g_qn.reshape(QK_DIM, 1)
    gqsc = gqc[perm]
    gk = g_kn.reshape(1, QK_DIM)
    gks = pad_lanes(gk[:, perm])
    gk = pad_lanes(gk)

    ctab, stab = _rope_tables(L)
    pad_tab = lambda tab: np.pad(tab, ((0, 0), (0, LANES - QK_DIM)))
    hyt, zat, qt, k, vt = _in_proj(
        x, g_norm.reshape(1, D_MODEL), wt_hy, w_cq, w_ckv, w_za, g_cq.reshape(1, Q_LORA),
        g_ckv.reshape(1, KV_LORA), jnp.asarray(pad_tab(ctab)), jnp.asarray(pad_tab(stab)),
        jnp.asarray(np.ascontiguousarray(ctab.T)), jnp.asarray(np.ascontiguousarray(stab.T)),
        wqt, wk, wvt, gqc, gqsc, gk, gks, tl)

    zfeat, t = _filter_features(L)
    w1 = jnp.concatenate([w_f1, jnp.zeros((LANES - FILTER_EMB, FILTER_HIDDEN), F32)], axis=0)
    w3 = w_f3.reshape(FILTER_HIDDEN, 2, 2, HY_WIDTH)
    w3f = w3[:, :, 0].reshape(FILTER_HIDDEN, 2 * HY_WIDTH).T
    w3b = w3[:, :, 1].reshape(FILTER_HIDDEN, 2 * HY_WIDTH).T
    deltas = np.abs(np.linspace(MIN_DECAY, MAX_DECAY, HY_WIDTH))
    dl = np.tile(deltas, 2).reshape(2 * HY_WIDTH, 1).astype(np.float32)
    col = lambda v: v.reshape(FILTER_HIDDEN, 1)
    zt = np.ascontiguousarray(np.stack([zfeat.T, zfeat[::-1].T]))
    t2 = np.ascontiguousarray(np.stack([t, t[::-1]]).reshape(2, 1, L))
    h2t = _fmlp(jnp.asarray(zt), w1.T, col(b_f1), col(freq_1), w_f2.T, col(b_f2), col(freq_2))
    wf, wi = (jnp.asarray(w).astype(BF16) for w in _dft_mats())
    tw = jnp.asarray(_twiddle(n1))
    kspec = _fspec(h2t, jnp.asarray(t2), w3f, w3b, jnp.asarray(dl), wf, tw, L, n1, 2 * ct, group)

    wsh = w_short.reshape(3, 3, HY_WIDTH).transpose(1, 0, 2)[..., None]
    bsh = b_short.reshape(3, HY_WIDTH, 1)
    yt = _hyena(hyt.reshape(B, 4, HY_WIDTH, L), wsh, bsh, kspec, hy_bias.reshape(2, HY_WIDTH, 1),
                wf, wi, tw, n1, ct, group)

    at = _attn(qt, k, vt, min(2048, L))

    return _final(x, yt, at, zat, g_norm.reshape(1, D_MODEL), w_gate, b_gate.reshape(1, 2 * D_MODEL),
                  w_hy_out.astype(BF16), w_attn_out.astype(BF16), w_out.astype(BF16), min(1024, L))


def kernel(x, g_norm, w_in, b_gate, w_short, b_short, w_f1, b_f1, freq_1, w_f2, b_f2, freq_2, w_f3, hy_bias, w_hy_out, g_cq, w_uq, g_ckv, w_ukv, g_qn, g_kn, w_attn_out, w_out):
    depth = g_norm.shape[0]
    for l in range(depth):
        x = _layer(x, g_norm[l], w_in[l], b_gate[l], w_short[l], b_short[l], w_f1[l], b_f1[l], freq_1[l], w_f2[l],
                   b_f2[l], freq_2[l], w_f3[l], hy_bias[l], w_hy_out[l], g_cq[l], w_uq[l], g_ckv[l], w_ukv[l],
                   g_qn[l], g_kn[l], w_attn_out[l], w_out[l])
    return x
```

```python
import functools
import math

import numpy as np
import jax
import jax.numpy as jnp
from jax import lax
from jax.experimental import pallas as pl
from jax.experimental.pallas import tpu as pltpu

D_MODEL = 1024
HY_WIDTH = 512
FILTER_EMB = 33
FILTER_HIDDEN = 64
DECAY_TARGET = 1e-2
FAST_DECAY = 0.3
SLOW_DECAY = 1.5
MIN_DECAY = math.log(DECAY_TARGET) / SLOW_DECAY
MAX_DECAY = math.log(DECAY_TARGET) / FAST_DECAY
MLA_HEADS = 8
HEAD_PAIRS = MLA_HEADS // 2
QK_NOPE = 64
QK_ROPE = 32
QK_DIM = QK_NOPE + QK_ROPE
V_DIM = 64
Q_LORA = 384
KV_LORA = 256
ATTN_WIDTH = MLA_HEADS * V_DIM
ROPE_THETA = 10000.0
EPS = 1e-6

LANES = 128
SUBLANES = 8
DFT_LANE = 256
V_ROWS = 80
KV_CHUNK = 256
Q_TILE = 256
VMEM_LIMIT = 56 * 1024 * 1024

F32 = jnp.float32
BF16 = jnp.bfloat16
_NT = (((1,), (1,)), ((), ()))


def _cparams(sem):
    return pltpu.CompilerParams(dimension_semantics=sem, vmem_limit_bytes=VMEM_LIMIT)


def _resident(shape):
    nd = len(shape)
    return pl.BlockSpec(shape, lambda *_: (0,) * nd, pipeline_mode=pl.Buffered(1))


def _add(a, b):
    if a is None:
        return b
    if b is None:
        return a
    return a + b


def _sub(a, b):
    if b is None:
        return a
    if a is None:
        return -b
    return a - b


def _cadd(x, y):
    return (_add(x[0], y[0]), _add(x[1], y[1]))


def _csub(x, y):
    return (_sub(x[0], y[0]), _sub(x[1], y[1]))


def _scale(a, s):
    if a is None or s == 0.0:
        return None
    if s == 1.0:
        return a
    if s == -1.0:
        return -a
    return a * s


def _signed_sum(terms):
    pos = [a for sg, a in terms if a is not None and sg > 0]
    neg = [a for sg, a in terms if a is not None and sg < 0]
    if pos:
        r = pos[0]
        for a in pos[1:]:
            r = r + a
        for a in neg:
            r = r - a
        return r, 1.0
    if neg:
        r = neg[0]
        for a in neg[1:]:
            r = r + a
        return r, -1.0
    return None, 1.0


def _cmulc(x, w):
    wr, wi = float(np.real(w)), float(np.imag(w))
    if abs(wr) < 1e-15:
        wr = 0.0
    if abs(wi) < 1e-15:
        wi = 0.0
    xr, xi = x
    if wr != 0.0 and abs(abs(wr) - abs(wi)) < 1e-12:
        sr, si, c = np.sign(wr), np.sign(wi), abs(wr)
        re, sre = _signed_sum([(sr, xr), (-si, xi)])
        im, sim = _signed_sum([(si, xr), (sr, xi)])
        return (_scale(re, sre * c), _scale(im, sim * c))
    re = _sub(_scale(xr, wr), _scale(xi, wi))
    im = _add(_scale(xr, wi), _scale(xi, wr))
    return (re, im)


def _cdiff_mulc(x, y, w):
    wr, wi = float(np.real(w)), float(np.imag(w))
    (xr, xi), (yr, yi) = x, y
    unit = {(1, 0): ((1, xr, yr), (1, xi, yi)), (-1, 0): ((-1, xr, yr), (-1, xi, yi)),
            (0, 1): ((-1, xi, yi), (1, xr, yr)), (0, -1): ((1, xi, yi), (-1, xr, yr))}
    key = (int(round(wr)), int(round(wi)))
    if abs(wr - key[0]) < 1e-12 and abs(wi - key[1]) < 1e-12 and key in unit:
        parts = []
        for sg, a, b in unit[key]:
            v, sv = _signed_sum([(sg, a), (-sg, b)])
            parts.append(_scale(v, sv))
        return tuple(parts)
    return _cmulc(_csub(x, y), w)


def _cmul(x, y):
    xr, xi = x
    yr, yi = y
    re = _sub(None if xr is None else xr * yr, None if xi is None else xi * yi)
    im = _add(None if xr is None else xr * yi, None if xi is None else xi * yr)
    return (re, im)


def _fft_dif(xs, sign):
    n = len(xs)
    if n == 1:
        return xs
    half = n // 2
    a = [_cadd(xs[i], xs[i + half]) for i in range(half)]
    b = [_cdiff_mulc(xs[i], xs[i + half], np.exp(sign * 2j * np.pi * i / n)) for i in range(half)]
    ev = _fft_dif(a, sign)
    od = _fft_dif(b, sign)
    out = [None] * n
    out[0::2] = ev
    out[1::2] = od
    return out


def _dft_mats():
    n = np.arange(DFT_LANE)
    ang = 2.0 * np.pi * np.outer(n, n) / DFT_LANE
    fr, fi = np.cos(ang), -np.sin(ang)
    fwd = np.block([[fr, fi], [-fi, fr]]).astype(np.float32)
    inv = np.block([[fr, -fi], [fi, fr]]).astype(np.float32)
    return fwd, inv


def _twiddle(n1):
    n = n1 * DFT_LANE
    ang = 2.0 * np.pi * np.outer(np.arange(n1), np.arange(DFT_LANE)) / n
    tw = np.stack([np.cos(ang), -np.sin(ang)], axis=1)
    return np.ascontiguousarray(np.broadcast_to(tw[:, :, None, :], (n1, 2, SUBLANES, DFT_LANE))).astype(np.float32)


def _filter_features(L):
    t = np.linspace(0.0, 1.0, L)[:, None]
    bands = (FILTER_EMB - 1) // 2
    f = np.linspace(1e-4, bands - 1, bands)
    ang = (2.0 * np.pi / L) * np.arange(L)[:, None] * f[None, :]
    z = np.concatenate([t, np.cos(ang), -np.sin(ang)], axis=-1)
    zp = np.zeros((L, LANES), np.float32)
    zp[:, :FILTER_EMB] = z
    return zp, t[:, 0].astype(np.float32)


def _rope_tables(L):
    pos = np.arange(L, dtype=np.float64)
    inv_freq = ROPE_THETA ** (-np.arange(0, QK_ROPE, 2, dtype=np.float64) / QK_ROPE)
    ang = pos[:, None] * inv_freq[None, :]
    c, s = np.cos(ang), np.sin(ang)
    ctab = np.concatenate([np.ones((L, QK_NOPE)), c, c], axis=1)
    stab = np.concatenate([np.zeros((L, QK_NOPE)), -s, s], axis=1)
    return ctab.astype(np.float32), stab.astype(np.float32)


def _in_proj_kernel(x_ref, gn_ref, wt_hy_ref, w_cq_ref, w_ckv_ref, w_za_ref, gcq_ref, gckv_ref,
                    ct_ref, st_ref, ctt_ref, stt_ref, wqt_ref, wk_ref, wvt_ref,
                    gqc_ref, gqsc_ref, gk_ref, gks_ref, ones_ref,
                    hyt_ref, zat_ref, qt_ref, k_ref, vt_ref, *, tl, sub):
    half = QK_ROPE // 2
    scale = math.log2(math.e) / math.sqrt(QK_DIM)

    def latent(c, g_ref):
        r = lax.rsqrt(jnp.mean(c * c, axis=-1, keepdims=True) + EPS)
        return (c * r * g_ref[...]).astype(BF16)

    for r0 in range(0, tl, sub):
        rows = slice(r0, r0 + sub)
        x = x_ref[0, rows, :]
        ms = jnp.mean(x * x, axis=-1, keepdims=True)
        h = (x * lax.rsqrt(ms + EPS) * gn_ref[...]).astype(BF16)

        cqr = jnp.dot(h, w_cq_ref[...], preferred_element_type=F32)
        cq = latent(cqr[:, :Q_LORA], gcq_ref)
        ckv = latent(jnp.dot(h, w_ckv_ref[...], preferred_element_type=F32), gckv_ref)
        kr_a = cqr[:, Q_LORA:]
        qt_all = lax.dot_general(wqt_ref[...], cq, _NT, preferred_element_type=F32)
        ka_all = jnp.dot(ckv, wk_ref[...], preferred_element_type=F32)
        vt_all = lax.dot_general(wvt_ref[...], ckv, _NT, preferred_element_type=F32) + ones_ref[...]

        hyt_ref[0, :, rows] = lax.dot_general(wt_hy_ref[...], h, _NT, preferred_element_type=F32).astype(BF16)
        zat_ref[0, rows, :] = jnp.dot(h, w_za_ref[...], preferred_element_type=F32).astype(BF16)

        qc, qs = gqc_ref[...] * ctt_ref[:, rows], gqsc_ref[...] * stt_ref[:, rows]
        kc, ks = gk_ref[...] * ct_ref[rows, :], gks_ref[...] * st_ref[rows, :]
        lane = lax.broadcasted_iota(jnp.int32, kr_a.shape, 1)
        kr_w = jnp.where(lane < QK_NOPE + half, pltpu.roll(kr_a, LANES - half, 1), pltpu.roll(kr_a, half, 1))
        for hd in range(MLA_HEADS):
            qa = qt_all[hd * QK_DIM:(hd + 1) * QK_DIM]
            qw = jnp.concatenate([qa[:QK_NOPE], qa[QK_NOPE + half:], qa[QK_NOPE:QK_NOPE + half]], axis=0)
            r = lax.rsqrt(jnp.mean(qa * qa, axis=0, keepdims=True) + EPS) * scale
            qt_ref[0, hd, :, rows] = ((qa * qc + qw * qs) * r).astype(BF16)
            ka = ka_all[:, hd * LANES:(hd + 1) * LANES] + kr_a
            rk = lax.rsqrt(jnp.sum(ka * ka, axis=-1, keepdims=True) * (1.0 / QK_DIM) + EPS)
            k_ref[0, hd, rows, :] = ((ka * kc + kr_w * ks) * rk)[:, :QK_DIM].astype(BF16)
            vt_ref[0, hd, :, rows] = vt_all[hd * V_ROWS:(hd + 1) * V_ROWS].astype(BF16)


def _in_proj(x, gn, wt_hy, w_cq, w_ckv, w_za, gcq, gckv, ctab, stab, ctab_t, stab_t,
             wqt, wk, wvt, gqc, gqsc, gk, gks, tl):
    ones = np.zeros((MLA_HEADS, V_ROWS, 1), np.float32)
    ones[:, V_DIM, 0] = 1.0
    ones = jnp.asarray(ones.reshape(MLA_HEADS * V_ROWS, 1))
    B, L, _ = x.shape
    tab = pl.BlockSpec((tl, LANES), lambda b, i: (i, 0))
    tab_t = pl.BlockSpec((QK_DIM, tl), lambda b, i: (0, i))
    consts = (gn, wt_hy, w_cq, w_ckv, w_za, gcq, gckv)
    qkv_consts = (wqt, wk, wvt, gqc, gqsc, gk, gks, ones)
    return pl.pallas_call(
        functools.partial(_in_proj_kernel, tl=tl, sub=min(512, tl)),
        grid=(B, L // tl),
        in_specs=[pl.BlockSpec((1, tl, D_MODEL), lambda b, i: (b, i, 0))]
        + [_resident(a.shape) for a in consts] + [tab, tab, tab_t, tab_t] + [_resident(a.shape) for a in qkv_consts],
        out_specs=[
            pl.BlockSpec((1, 4 * HY_WIDTH, tl), lambda b, i: (b, 0, i)),
            pl.BlockSpec((1, tl, ATTN_WIDTH), lambda b, i: (b, i, 0)),
            pl.BlockSpec((1, MLA_HEADS, QK_DIM, tl), lambda b, i: (b, 0, 0, i)),
            pl.BlockSpec((1, MLA_HEADS, tl, QK_DIM), lambda b, i: (b, 0, i, 0)),
            pl.BlockSpec((1, MLA_HEADS, V_ROWS, tl), lambda b, i: (b, 0, 0, i)),
        ],
        out_shape=[
            jax.ShapeDtypeStruct((B, 4 * HY_WIDTH, L), BF16),
            jax.ShapeDtypeStruct((B, L, ATTN_WIDTH), BF16),
            jax.ShapeDtypeStruct((B, MLA_HEADS, QK_DIM, L), BF16),
            jax.ShapeDtypeStruct((B, MLA_HEADS, L, QK_DIM), BF16),
            jax.ShapeDtypeStruct((B, MLA_HEADS, V_ROWS, L), BF16),
        ],
        compiler_params=_cparams(("parallel", "parallel")),
        name="in_proj",
    )(x, *consts, ctab, stab, ctab_t, stab_t, *qkv_consts)


def _fmlp_kernel(zt_ref, w1t_ref, b1_ref, f1_ref, w2t_ref, b2_ref, f2_ref, out_ref):
    hp = lax.Precision.HIGHEST
    h = jnp.sin(f1_ref[...] * (jnp.dot(w1t_ref[...], zt_ref[0], precision=hp, preferred_element_type=F32) + b1_ref[...]))
    out_ref[0] = jnp.sin(f2_ref[...] * (jnp.dot(w2t_ref[...], h, precision=hp, preferred_element_type=F32) + b2_ref[...]))


def _fmlp(zt, w1t, b1, f1, w2t, b2, f2):
    _, _, L = zt.shape
    return pl.pallas_call(
        _fmlp_kernel,
        grid=(2,),
        in_specs=[pl.BlockSpec((1, LANES, L), lambda d: (d, 0, 0)), _resident(w1t.shape), _resident(b1.shape),
                  _resident(f1.shape), _resident(w2t.shape), _resident(b2.shape), _resident(f2.shape)],
        out_specs=pl.BlockSpec((1, FILTER_HIDDEN, L), lambda d: (d, 0, 0)),
        out_shape=jax.ShapeDtypeStruct((2, FILTER_HIDDEN, L), F32),
        compiler_params=_cparams(("parallel",)),
        name="fmlp",
    )(zt, w1t, b1, f1, w2t, b2, f2)


def _stage_a_forward(load, g_scr, tw_ref, n1, rows_total):
    def body(r, carry):
        rs = pl.ds(pl.multiple_of(r * SUBLANES, SUBLANES), SUBLANES)
        for half in range(DFT_LANE // LANES):
            ls = slice(half * LANES, (half + 1) * LANES)
            li = slice(DFT_LANE + half * LANES, DFT_LANE + (half + 1) * LANES)
            xs = [load(i, rs, ls) for i in range(n1)]
            xs = [(None, None) if x is None else x for x in xs]
            X = _fft_dif(xs, -1.0)
            for k1 in range(n1):
                if tw_ref is None:
                    yr, yi = X[k1]
                else:
                    yr, yi = _cmul(X[k1], (tw_ref[k1, 0, :, ls], tw_ref[k1, 1, :, ls]))
                g_scr[k1, rs, ls] = yr
                g_scr[k1, rs, li] = yi
        return carry

    lax.fori_loop(0, rows_total // SUBLANES, body, 0)


def _fspec_kernel(h_ref, t_ref, w3f_ref, w3b_ref, dl_ref, wf_ref, tw_ref, out_ref, k_scr, g_scr,
                  *, L, n1, rows, group):
    dl = dl_ref[...]
    for d, w_ref in enumerate((w3f_ref, w3b_ref)):
        k = jnp.dot(w_ref[...].astype(BF16), h_ref[d].astype(BF16), preferred_element_type=F32)
        k_scr[:, d * L:(d + 1) * L] = k * jnp.exp(-(dl * t_ref[d]))

    def load(i, rs, ls):
        return (k_scr[rs, i * DFT_LANE + ls.start:i * DFT_LANE + ls.stop], None)

    _stage_a_forward(load, g_scr, tw_ref, n1, rows)
    inv_n = 1.0 / (n1 * DFT_LANE)
    w = wf_ref[...]
    for g0 in range(0, n1, group):
        lhs = g_scr[g0:g0 + group].reshape(group * rows, 2 * DFT_LANE).astype(BF16)
        res = jnp.dot(lhs, w, preferred_element_type=F32)
        out_ref[0, g0:g0 + group] = res.reshape(group, rows, 2 * DFT_LANE) * inv_n


def _fspec(h2t, t2, w3f, w3b, dl, wf, tw, L, n1, rows, group):
    n_rows = w3f.shape[0]
    tiles_per_order = HY_WIDTH // rows
    row_spec = lambda shape: pl.BlockSpec(shape, lambda i: (i, 0))
    return pl.pallas_call(
        functools.partial(_fspec_kernel, L=L, n1=n1, rows=rows, group=group),
        grid=(n_rows // rows,),
        in_specs=[_resident(h2t.shape), _resident(t2.shape),
                  row_spec((rows, FILTER_HIDDEN)), row_spec((rows, FILTER_HIDDEN)), row_spec((rows, 1)),
                  _resident(wf.shape), _resident(tw.shape)],
        out_specs=pl.BlockSpec((1, n1, rows, 2 * DFT_LANE),
                               lambda i: (i // tiles_per_order, 0, i % tiles_per_order, 0)),
        out_shape=jax.ShapeDtypeStruct((n_rows // HY_WIDTH, n1, HY_WIDTH, 2 * DFT_LANE), F32),
        scratch_shapes=[pltpu.VMEM((rows, 2 * L), F32), pltpu.VMEM((n1, rows, 2 * DFT_LANE), F32)],
        compiler_params=_cparams(("parallel",)),
        name="fspec",
    )(h2t, t2, w3f, w3b, dl, wf, tw)


def _hyena_kernel(hy_ref, wsh_ref, bsh_ref, ksp_ref, bias_ref, wf_ref, wi_ref, tw_ref, out_ref,
                  u_scr, x1_scr, x2_scr, g_scr, *, L, n1, ct, group):
    lane = lax.broadcasted_iota(jnp.int32, (ct, LANES), 1)

    def short_conv(g, dst):
        for bi in range(2):
            p = hy_ref[bi, g].astype(F32)
            left = pltpu.roll(p, 1, 1)
            left = jnp.concatenate([jnp.where(lane == 0, 0.0, left[:, :LANES]), left[:, LANES:]], axis=1)
            right = pltpu.roll(p, L - 1, 1)
            right = jnp.concatenate([right[:, :L - LANES], jnp.where(lane == LANES - 1, 0.0, right[:, L - LANES:])],
                                    axis=1)
            conv = wsh_ref[g, 0] * left + wsh_ref[g, 1] * p + wsh_ref[g, 2] * right + bsh_ref[g]
            if g == 2:
                z = hy_ref[bi, 3].astype(F32)
                conv = conv * (z * jax.nn.sigmoid(z))
            dst[bi] = conv

    short_conv(0, u_scr)
    gate_scr = (x1_scr, x2_scr)

    half_blocks = n1 // 2
    wf = wf_ref[...]
    wi = wi_ref[...]
    re, im = slice(0, DFT_LANE), slice(DFT_LANE, 2 * DFT_LANE)

    def mxu_dft(x, w):
        res = jnp.dot(x.reshape(group * ct, 2 * DFT_LANE).astype(BF16), w, preferred_element_type=F32)
        res = res.reshape(group, ct, 2 * DFT_LANE)
        return res[..., re], res[..., im]

    for order in range(2):
        def load(i, rs, ls):
            if i >= half_blocks:
                return None
            cs = slice(i * DFT_LANE + ls.start, i * DFT_LANE + ls.stop)
            return (u_scr[0, rs, cs], u_scr[1, rs, cs])

        _stage_a_forward(load, g_scr, None, n1, ct)

        def twiddled(gs):
            g = g_scr[gs]
            gr, gi = g[..., re], g[..., im]
            twr = tw_ref[gs, 0, 0:1, :]
            twi = tw_ref[gs, 1, 0:1, :]
            return jnp.concatenate([gr * twr - gi * twi, gr * twi + gi * twr], axis=-1)

        def untwiddle(gs, q):
            qr, qi = q
            twr = tw_ref[gs, 0, 0:1, :]
            twi = tw_ref[gs, 1, 0:1, :]
            g_scr[gs, :, re] = qr * twr + qi * twi
            g_scr[gs, :, im] = qi * twr - qr * twi

        groups = [slice(g0, g0 + group) for g0 in range(0, n1, group)]
        spectra = [mxu_dft(twiddled(gs), wf) for gs in groups]
        short_conv(order + 1, gate_scr[order])
        prev = None
        for gs, (hr, hi) in zip(groups, spectra):
            kr, ki = ksp_ref[order, gs, :, re], ksp_ref[order, gs, :, im]
            q = mxu_dft(jnp.concatenate([hr * kr - hi * ki, hr * ki + hi * kr], axis=-1), wi)
            if prev is not None:
                untwiddle(*prev)
            prev = (gs, q)
        untwiddle(*prev)

        def body(r, carry, order=order):
            rs = pl.ds(pl.multiple_of(r * SUBLANES, SUBLANES), SUBLANES)
            bias = bias_ref[order, rs, :]
            for half in range(DFT_LANE // LANES):
                ls = slice(half * LANES, (half + 1) * LANES)
                li = slice(DFT_LANE + half * LANES, DFT_LANE + (half + 1) * LANES)
                q = [(g_scr[k1, rs, ls], g_scr[k1, rs, li]) for k1 in range(n1)]
                ev = _fft_dif(q[0::2], 1.0)
                od = _fft_dif(q[1::2], 1.0)
                for i in range(half_blocks):
                    y = _cadd(ev[i], _cmulc(od[i], np.exp(2j * np.pi * i / n1)))
                    cs = slice(i * DFT_LANE + ls.start, i * DFT_LANE + ls.stop)
                    for bi in range(2):
                        u = u_scr[bi, rs, cs]
                        conv = y[bi] + bias * u
                        if order == 0:
                            u_scr[bi, rs, cs] = x1_scr[bi, rs, cs] * conv
                        else:
                            out_ref[bi, rs, cs] = x2_scr[bi, rs, cs] * conv
            return carry

        lax.fori_loop(0, ct // SUBLANES, body, 0)


def _hyena(hyt4, wsh, bsh, kspec, bias, wf, wi, tw, n1, ct, group):
    B, _, C, L = hyt4.shape
    return pl.pallas_call(
        functools.partial(_hyena_kernel, L=L, n1=n1, ct=ct, group=group),
        grid=(C // ct, B // 2),
        in_specs=[
            pl.BlockSpec((2, 4, ct, L), lambda c, p: (p, 0, c, 0)),
            pl.BlockSpec((3, 3, ct, 1), lambda c, p: (0, 0, c, 0)),
            pl.BlockSpec((3, ct, 1), lambda c, p: (0, c, 0)),
            pl.BlockSpec((2, n1, ct, 2 * DFT_LANE), lambda c, p: (0, 0, c, 0)),
            pl.BlockSpec((2, ct, 1), lambda c, p: (0, c, 0)),
            _resident(wf.shape), _resident(wi.shape), _resident(tw.shape),
        ],
        out_specs=pl.BlockSpec((2, ct, L), lambda c, p: (p, c, 0)),
        out_shape=jax.ShapeDtypeStruct((B, C, L), F32),
        scratch_shapes=[pltpu.VMEM((2, ct, L), F32), pltpu.VMEM((2, ct, L), F32), pltpu.VMEM((2, ct, L), F32),
                        pltpu.VMEM((n1, ct, 2 * DFT_LANE), F32)],
        compiler_params=_cparams(("parallel", "parallel")),
        name="hyena",
    )(hyt4, wsh, bsh, kspec, bias, wf, wi, tw)


def _attn_kernel(qt_ref, k_ref, vt_ref, o_ref, *, L, tq):
    units = [(j, t0) for t0 in range(0, tq, Q_TILE) for j in range(2)]
    starts = list(range(0, L, KV_CHUNK))

    def scores(u, c0):
        j, t0 = u
        return jnp.dot(k_ref[0, j, c0:c0 + KV_CHUNK, :], qt_ref[0, j, :, t0:t0 + Q_TILE],
                       preferred_element_type=F32)

    m = {u: None for u in units}
    acc = {u: None for u in units}
    cur = {u: scores(u, starts[0]) for u in units}
    for i, c0 in enumerate(starts):
        nxt = {}
        for u in units:
            if i + 1 < len(starts):
                nxt[u] = scores(u, starts[i + 1])
            s = cur[u]
            mc = jnp.max(s, axis=0, keepdims=True)
            m_new = mc if m[u] is None else jnp.maximum(m[u], mc)
            p = jnp.exp2(s - m_new).astype(BF16)
            pv = jnp.dot(vt_ref[0, u[0], :, c0:c0 + KV_CHUNK], p, preferred_element_type=F32)
            acc[u] = pv if m[u] is None else jnp.exp2(m[u] - m_new) * acc[u] + pv
            m[u] = m_new
        cur = nxt
    for j, t0 in units:
        a = acc[(j, t0)]
        o_ref[0, j * V_DIM:(j + 1) * V_DIM, t0:t0 + Q_TILE] = (a[:V_DIM] / a[V_DIM:V_DIM + 1]).astype(BF16)


def _attn(qt, k, vt, tq):
    B, _, L, _ = k.shape
    return pl.pallas_call(
        functools.partial(_attn_kernel, L=L, tq=tq),
        grid=(B, HEAD_PAIRS, L // tq),
        in_specs=[pl.BlockSpec((1, 2, QK_DIM, tq), lambda b, p, i: (b, p, 0, i)),
                  pl.BlockSpec((1, 2, L, QK_DIM), lambda b, p, i: (b, p, 0, 0)),
                  pl.BlockSpec((1, 2, V_ROWS, L), lambda b, p, i: (b, p, 0, 0))],
        out_specs=pl.BlockSpec((1, 2 * V_DIM, tq), lambda b, p, i: (b, p, i)),
        out_shape=jax.ShapeDtypeStruct((B, ATTN_WIDTH, L), BF16),
        compiler_params=_cparams(("parallel", "parallel", "parallel")),
        name="attn",
    )(qt, k, vt)


def _final_kernel(x_ref, yt_ref, at_ref, zat_ref, gn_ref, wg_ref, bg_ref, why_ref, wat_ref, wout_ref, o_ref,
                  *, tl, sub):
    for r0 in range(0, tl, sub):
        rows = slice(r0, r0 + sub)
        x = x_ref[0, rows, :]
        h = (x * lax.rsqrt(jnp.mean(x * x, axis=-1, keepdims=True) + EPS) * gn_ref[...]).astype(BF16)
        gates = jax.nn.sigmoid(jnp.dot(h, wg_ref[...], preferred_element_type=F32) + bg_ref[...])
        y = yt_ref[0, :, rows].T.astype(BF16)
        u_hy = jnp.dot(y, why_ref[...], preferred_element_type=F32)
        z = zat_ref[0, rows, :].astype(F32)
        a = (at_ref[0, :, rows].astype(F32).T * (z * jax.nn.sigmoid(z))).astype(BF16)
        u_at = jnp.dot(a, wat_ref[...], preferred_element_type=F32)
        merged = gates[:, :D_MODEL] * u_hy + gates[:, D_MODEL:] * u_at
        o_ref[0, rows, :] = x + jnp.dot(merged.astype(BF16), wout_ref[...], preferred_element_type=F32)


def _final(x, yt, at, zat, gn, wg, bg, why, wat, wout, tl):
    B, L, _ = x.shape
    tok = lambda w: pl.BlockSpec((1, tl, w), lambda b, i: (b, i, 0))
    consts = (gn, wg, bg, why, wat, wout)
    return pl.pallas_call(
        functools.partial(_final_kernel, tl=tl, sub=min(512, tl)),
        grid=(B, L // tl),
        in_specs=[tok(D_MODEL), pl.BlockSpec((1, HY_WIDTH, tl), lambda b, i: (b, 0, i)),
                  pl.BlockSpec((1, ATTN_WIDTH, tl), lambda b, i: (b, 0, i)), tok(ATTN_WIDTH)]
        + [_resident(a.shape) for a in consts],
        out_specs=tok(D_MODEL),
        out_shape=jax.ShapeDtypeStruct((B, L, D_MODEL), F32),
        compiler_params=_cparams(("parallel", "parallel")),
        name="final",
    )(x, yt, at, zat, *consts)


def _layer(x, g_norm, w_in, b_gate, w_short, b_short, w_f1, b_f1, freq_1, w_f2, b_f2, freq_2, w_f3,
           hy_bias, w_hy_out, g_cq, w_uq, g_ckv, w_ukv, g_qn, g_kn, w_attn_out, w_out):
    B, L, _ = x.shape
    n = 2 * L
    n1 = n // DFT_LANE
    assert B % 2 == 0 and n1 * DFT_LANE == n and n1 >= 2 and (n1 & (n1 - 1)) == 0
    tl = min(1024, L)
    ct = 64
    group = min(8, n1)

    o_cq, o_ckv, o_kr, o_za = 0, Q_LORA, Q_LORA + KV_LORA, Q_LORA + KV_LORA + QK_ROPE
    wt_in = w_in.T.astype(BF16)
    wt_hy = wt_in[:4 * HY_WIDTH]
    w_in = wt_in[4 * HY_WIDTH:].T
    half = QK_ROPE // 2
    w_cq = jnp.concatenate([w_in[:, o_cq:o_ckv], jnp.zeros((D_MODEL, QK_NOPE), BF16), w_in[:, o_kr:o_za],
                            jnp.zeros((D_MODEL, LANES - QK_DIM), BF16)], axis=1)
    w_ckv = w_in[:, o_ckv:o_kr]
    w_za = w_in[:, o_za:o_za + ATTN_WIDTH]
    w_gate = w_in[:, o_za + ATTN_WIDTH:]

    perm = np.concatenate([np.arange(QK_NOPE), QK_NOPE + half + np.arange(half), QK_NOPE + np.arange(half)])
    wqt = w_uq.T.astype(BF16)
    wkv = w_ukv.reshape(KV_LORA, MLA_HEADS, QK_NOPE + V_DIM)
    wk = jnp.pad(wkv[:, :, :QK_NOPE], ((0, 0), (0, 0), (0, LANES - QK_NOPE)))
    wk = wk.reshape(KV_LORA, MLA_HEADS * LANES).astype(BF16)
    wvt = jnp.pad(wkv[:, :, QK_NOPE:], ((0, 0), (0, 0), (0, V_ROWS - V_DIM)))
    wvt = wvt.reshape(KV_LORA, MLA_HEADS * V_ROWS).T.astype(BF16)
    pad_lanes = lambda g: jnp.pad(g, ((0, 0), (0, LANES - QK_DIM)))
    gqc = g_qn.reshape(QK_DIM, 1)
    gqsc = gqc[perm]
    gk = g_kn.reshape(1, QK_DIM)
    gks = pad_lanes(gk[:, perm])
    gk = pad_lanes(gk)

    ctab, stab = _rope_tables(L)
    pad_tab = lambda tab: np.pad(tab, ((0, 0), (0, LANES - QK_DIM)))
    hyt, zat, qt, k, vt = _in_proj(
        x, g_norm.reshape(1, D_MODEL), wt_hy, w_cq, w_ckv, w_za, g_cq.reshape(1, Q_LORA),
        g_ckv.reshape(1, KV_LORA), jnp.asarray(pad_tab(ctab)), jnp.asarray(pad_tab(stab)),
        jnp.asarray(np.ascontiguousarray(ctab.T)), jnp.asarray(np.ascontiguousarray(stab.T)),
        wqt, wk, wvt, gqc, gqsc, gk, gks, tl)

    zfeat, t = _filter_features(L)
    w1 = jnp.concatenate([w_f1, jnp.zeros((LANES - FILTER_EMB, FILTER_HIDDEN), F32)], axis=0)
    w3 = w_f3.reshape(FILTER_HIDDEN, 2, 2, HY_WIDTH)
    w3f = w3[:, :, 0].reshape(FILTER_HIDDEN, 2 * HY_WIDTH).T
    w3b = w3[:, :, 1].reshape(FILTER_HIDDEN, 2 * HY_WIDTH).T
    deltas = np.abs(np.linspace(MIN_DECAY, MAX_DECAY, HY_WIDTH))
    dl = np.tile(deltas, 2).reshape(2 * HY_WIDTH, 1).astype(np.float32)
    col = lambda v: v.reshape(FILTER_HIDDEN, 1)
    zt = np.ascontiguousarray(np.stack([zfeat.T, zfeat[::-1].T]))
    t2 = np.ascontiguousarray(np.stack([t, t[::-1]]).reshape(2, 1, L))
    h2t = _fmlp(jnp.asarray(zt), w1.T, col(b_f1), col(freq_1), w_f2.T, col(b_f2), col(freq_2))
    wf, wi = (jnp.asarray(w).astype(BF16) for w in _dft_mats())
    tw = jnp.asarray(_twiddle(n1))
    kspec = _fspec(h2t, jnp.asarray(t2), w3f, w3b, jnp.asarray(dl), wf, tw, L, n1, 2 * ct, group)

    wsh = w_short.reshape(3, 3, HY_WIDTH).transpose(1, 0, 2)[..., None]
    bsh = b_short.reshape(3, HY_WIDTH, 1)
    yt = _hyena(hyt.reshape(B, 4, HY_WIDTH, L), wsh, bsh, kspec, hy_bias.reshape(2, HY_WIDTH, 1),
                wf, wi, tw, n1, ct, group)

    at = _attn(qt, k, vt, min(2048, L))

    return _final(x, yt, at, zat, g_norm.reshape(1, D_MODEL), w_gate, b_gate.reshape(1, 2 * D_MODEL),
                  w_hy_out.astype(BF16), w_attn_out.astype(BF16), w_out.astype(BF16), min(1024, L))


def kernel(x, g_norm, w_in, b_gate, w_short, b_short, w_f1, b_f1, freq_1, w_f2, b_f2, freq_2, w_f3, hy_bias, w_hy_out, g_cq, w_uq, g_ckv, w_ukv, g_qn, g_kn, w_attn_out, w_out):
    depth = g_norm.shape[0]
    for l in range(depth):
        x = _layer(x, g_norm[l], w_in[l], b_gate[l], w_short[l], b_short[l], w_f1[l], b_f1[l], freq_1[l], w_f2[l],
                   b_f2[l], freq_2[l], w_f3[l], hy_bias[l], w_hy_out[l], g_cq[l], w_uq[l], g_ckv[l], w_ukv[l],
                   g_qn[l], g_kn[l], w_attn_out[l], w_out[l])
    return x
```
